```python
import jax
import jax.numpy as jnp
from jax import lax
import numpy as np

D_MODEL = 2048
BATCH = 4
SEQ = 2048
DEPTH = 4
DEC_BATCH = 8
DEC_SEQ = 1
PAST_LEN = 16384
PAGE_SIZE = 128

HEAD_DIM = 128
A_WIDTH = D_MODEL // 2
A_GROUPS = A_WIDTH // HEAD_DIM
B_WIDTH = D_MODEL - A_WIDTH
N_HEADS = B_WIDTH // HEAD_DIM
N_KV = 2
HPG = N_HEADS // N_KV
CHUNK = 128
CMP_BLOCK = 32
CMP_STRIDE = 16
CMP_HIDDEN = 256
SLC_BLOCK = 64
N_SEL = 16
WINDOW = 512
WIN_QB = 128
SLC_QB = 32
ROPE_DIM = HEAD_DIM // 4
ROPE_THETA = 500000.0
D_FF = 4 * D_MODEL
EPS = 1e-6
FORCE = 1e4
SCALE = HEAD_DIM ** -0.5
KV_COLS = N_KV * 2 * HEAD_DIM
IN_COLS = 2 * A_WIDTH + B_WIDTH + 3 * KV_COLS + 3 * N_HEADS
SPLITS = [A_WIDTH, 2 * A_WIDTH, 2 * A_WIDTH + B_WIDTH, 2 * A_WIDTH + B_WIDTH + KV_COLS, 2 * A_WIDTH + B_WIDTH + 2 * KV_COLS, 2 * A_WIDTH + B_WIDTH + 3 * KV_COLS]

kernel_name = 'hymba_sgu_nsa_decoder_step'


def rmsnorm(x, g):
    xf = x.astype(jnp.float32)
    y = xf * lax.rsqrt(jnp.mean(xf * xf, axis=-1, keepdims=True) + EPS) * g.astype(jnp.float32)
    return y.astype(x.dtype)


def layernorm(x, g, b):
    xf = x.astype(jnp.float32)
    mu = jnp.mean(xf, axis=-1, keepdims=True)
    var = jnp.mean(jnp.square(xf - mu), axis=-1, keepdims=True)
    y = (xf - mu) * lax.rsqrt(var + EPS) * g.astype(jnp.float32) + b.astype(jnp.float32)
    return y.astype(x.dtype)


def masked_softmax(s, mask):
    s = jnp.where(mask, s, -jnp.inf)
    m = jnp.max(s, axis=-1, keepdims=True)
    m = jnp.where(jnp.isfinite(m), m, 0.0)
    e = jnp.where(mask, jnp.exp(s - m), 0.0)
    return e / jnp.maximum(jnp.sum(e, axis=-1, keepdims=True), 1e-30)


def rope(x, pos):
    half = ROPE_DIM // 2
    inv = ROPE_THETA ** (-jnp.arange(half, dtype=jnp.float32) / half)
    ang = pos.astype(jnp.float32)[:, None] * inv[None, :]
    shp = (pos.shape[0],) + (1,) * (x.ndim - 3) + (half,)
    cos = jnp.cos(ang).reshape(shp)
    sin = jnp.sin(ang).reshape(shp)
    xf = x.astype(jnp.float32)
    x1, x2 = xf[..., :half], xf[..., half:ROPE_DIM]
    return jnp.concatenate([x1 * cos - x2 * sin, x2 * cos + x1 * sin, xf[..., ROPE_DIM:]], axis=-1).astype(x.dtype)


def rope_k(kv, pos):
    return jnp.stack([rope(kv[..., 0, :], pos), kv[..., 1, :]], axis=-2)


def chunk_sgu(u, v, ws, bs):
    B, T, _ = v.shape
    tp = -(-T // CHUNK) * CHUNK
    vr = jnp.pad(v, ((0, 0), (0, tp - T), (0, 0))).reshape(B, tp // CHUNK, CHUNK, A_GROUPS, HEAD_DIM)
    w = ws * jnp.tril(jnp.ones((CHUNK, CHUNK), ws.dtype))
    mixed = jnp.einsum('gpq,bnqgd->bnpgd', w, vr) + bs.T[:, :, None]
    return u * mixed.reshape(B, tp, A_WIDTH)[:, :T]


def compress(rows, pe, w1, w2):
    B, T = rows.shape[:2]
    tp = -(-T // CMP_STRIDE) * CMP_STRIDE
    r = jnp.pad(rows, ((0, 0), (0, tp - T), (0, 0), (0, 0))).reshape(B, tp // CMP_STRIDE, CMP_STRIDE, N_KV, HEAD_DIM)
    ratio = CMP_BLOCK // CMP_STRIDE
    nc = tp // CMP_STRIDE - ratio + 1
    blk = jnp.concatenate([r[:, i:i + nc] for i in range(ratio)], axis=2) + pe[:, None, :]
    flat = blk.transpose(0, 1, 3, 2, 4).reshape(B, nc, N_KV, CMP_BLOCK * HEAD_DIM)
    return jax.nn.gelu(flat @ w1) @ w2


def cmp_attend(q, pos, kc, vc):
    nc = kc.shape[1]
    s = jnp.einsum('btghd,bngd->btghn', q, kc).astype(jnp.float32) * SCALE
    end = jnp.arange(nc) * CMP_STRIDE + CMP_BLOCK - 1
    mask = (end[None, :] <= pos[:, None])[None, :, None, None, :]
    p = masked_softmax(s, mask)
    o = jnp.einsum('btghn,bngd->btghd', p.astype(vc.dtype), vc)
    return o, p


def select_blocks(p, pos, n_blk):
    imp = jnp.sum(p, axis=3)
    per = SLC_BLOCK // CMP_STRIDE
    inside = (SLC_BLOCK - CMP_BLOCK) // CMP_STRIDE + 1
    nc = imp.shape[-1]
    imp = jnp.pad(imp, ((0, 0), (0, 0), (0, 0), (0, n_blk * per - nc)))
    imp = jnp.sum(imp.reshape(imp.shape[:3] + (n_blk, per))[..., :inside], axis=-1)
    j = jnp.arange(n_blk)
    cur = (pos // SLC_BLOCK)[None, :, None, None]
    forced = (j == 0) | (j == cur) | (j == cur - 1)
    score = jnp.where(j <= cur, jnp.where(forced, FORCE, imp), -FORCE)
    _, idx = lax.top_k(score, min(N_SEL, n_blk))
    return idx, idx <= cur


def slc_attend(q, pos, blk, idx, valid):
    kpos = idx[..., None] * SLC_BLOCK + jnp.arange(SLC_BLOCK)
    mask = valid[..., None] & (kpos <= pos[None, :, None, None, None])
    s = jnp.einsum('btghd,btgkld->btghkl', q, blk[..., 0, :]).astype(jnp.float32) * SCALE
    B, T, G, H, K, L = s.shape
    p = masked_softmax(s.reshape(B, T, G, H, K * L), mask.reshape(B, T, G, 1, K * L))
    return jnp.einsum('btghn,btgnd->btghd', p.astype(blk.dtype), blk[..., 1, :].reshape(B, T, G, K * L, HEAD_DIM))


def slc_prompt(q, pos, kv, idx, valid):
    B, S = q.shape[:2]
    kvb = kv.reshape(B, S // SLC_BLOCK, SLC_BLOCK, N_KV, 2, HEAD_DIM)
    bidx = jnp.arange(B)[:, None, None, None]
    gidx = jnp.arange(N_KV)[None, None, :, None]
    nq = S // SLC_QB

    def blocks(a):
        return jnp.moveaxis(a.reshape((B, nq, SLC_QB) + a.shape[2:]), 1, 0)

    def one(args):
        qb, pb, ib, vb = args
        blk = kvb[bidx, ib, :, gidx]
        return slc_attend(qb, pb, blk, ib, vb)

    o = lax.map(one, (blocks(q), pos.reshape(nq, SLC_QB), blocks(idx), blocks(valid)))
    return jnp.moveaxis(o, 0, 1).reshape(q.shape)


def slc_sample(q, pos, kv_new, pool, page_table, idx, valid):
    DB, T = q.shape[:2]
    halves = PAGE_SIZE // SLC_BLOCK
    nb_past = page_table.shape[1] * halves
    pool_r = pool.reshape(pool.shape[0], halves, SLC_BLOCK, N_KV, 2, HEAD_DIM)
    n_new = -(-T // SLC_BLOCK)
    new_r = jnp.pad(kv_new, ((0, 0), (0, n_new * SLC_BLOCK - T), (0, 0), (0, 0), (0, 0))).reshape(DB, n_new, SLC_BLOCK, N_KV, 2, HEAD_DIM)
    bidx = jnp.arange(DB)[:, None, None, None]
    gidx = jnp.arange(N_KV)[None, None, :, None]
    jp = jnp.minimum(idx, nb_past - 1)
    phys = page_table[bidx, jp // halves]
    past_blk = pool_r[phys, jp % halves, :, gidx]
    jn = jnp.clip(idx - nb_past, 0, n_new - 1)
    new_blk = new_r[bidx, jn, :, gidx]
    blk = jnp.where((idx >= nb_past)[..., None, None, None], new_blk, past_blk)
    return slc_attend(q, pos, blk, idx, valid)


def win_prompt(q, kv):
    B, S = q.shape[:2]
    nb = S // WIN_QB
    npv = WINDOW // WIN_QB
    kw = (npv + 1) * WIN_QB
    kvp = jnp.pad(kv, ((0, 0), (npv * WIN_QB, 0), (0, 0), (0, 0), (0, 0)))
    kvr = kvp.reshape(B, nb + npv, WIN_QB, N_KV, 2, HEAD_DIM)
    band = jnp.concatenate([kvr[:, i:i + nb] for i in range(npv + 1)], axis=2)
    qpos = jnp.arange(S).reshape(nb, WIN_QB)
    kpos = jnp.arange(nb)[:, None] * WIN_QB - npv * WIN_QB + jnp.arange(kw)[None, :]
    diff = qpos[:, :, None] - kpos[:, None, :]
    mask = (kpos[:, None, :] >= 0) & (diff >= 0) & (diff < WINDOW)
    qr = q.reshape(B, nb, WIN_QB, N_KV, HPG, HEAD_DIM)
    s = jnp.einsum('bnqghd,bnkgd->bnghqk', qr, band[..., 0, :]).astype(jnp.float32) * SCALE
    p = masked_softmax(s, mask[None, :, None, None])
    o = jnp.einsum('bnghqk,bnkgd->bnqghd', p.astype(kv.dtype), band[..., 1, :])
    return o.reshape(B, S, N_KV, HPG, HEAD_DIM)


def win_sample(q, pos, kv_new, buf):
    wb = buf.shape[1]
    kv = jnp.concatenate([buf, kv_new], axis=1)
    kpos = jnp.concatenate([PAST_LEN - wb + jnp.arange(wb), pos])
    diff = pos[:, None] - kpos[None, :]
    mask = (diff >= 0) & (diff < WINDOW)
    s = jnp.einsum('btghd,bkgd->btghk', q, kv[..., 0, :]).astype(jnp.float32) * SCALE
    p = masked_softmax(s, mask[None, :, None, None, :])
    return jnp.einsum('btghk,bkgd->btghd', p.astype(kv.dtype), kv[..., 1, :])


def mixer(h, pos, lp, past):
    B, T, _ = h.shape
    u, v, q, kv_c, kv_s, kv_w, gt = jnp.split(h @ lp['w_in'], SPLITS, axis=-1)
    v = layernorm(jax.nn.gelu(v), lp['ln_v_g'], lp['ln_v_b'])
    a_out = chunk_sgu(jax.nn.gelu(u), v, lp['sgu_w'], lp['sgu_b'])
    q = q.reshape(B, T, N_KV, HPG, HEAD_DIM)
    q_rot = rope(q, pos)
    kv_c = kv_c.reshape(B, T, N_KV, 2, HEAD_DIM)
    kv_s = rope_k(kv_s.reshape(B, T, N_KV, 2, HEAD_DIM), pos)
    kv_w = rope_k(kv_w.reshape(B, T, N_KV, 2, HEAD_DIM), pos)
    if past is None:
        kv_c_all = kv_c
    else:
        rows = past['cmp'][past['page_table']]
        kv_c_all = jnp.concatenate([rows.reshape((B, -1) + rows.shape[3:]), kv_c], axis=1)
    kc = compress(kv_c_all[..., 0, :], lp['cmp_pe_k'], lp['cmp_w1_k'], lp['cmp_w2_k'])
    vc = compress(kv_c_all[..., 1, :], lp['cmp_pe_v'], lp['cmp_w1_v'], lp['cmp_w2_v'])
    o_c, p_c = cmp_attend(q, pos, kc, vc)
    n_blk = -(-kv_c_all.shape[1] // SLC_BLOCK)
    idx, valid = select_blocks(p_c, pos, n_blk)
    if past is None:
        o_s = slc_prompt(q_rot, pos, kv_s, idx, valid)
        o_w = win_prompt(q_rot, kv_w)
    else:
        o_s = slc_sample(q_rot, pos, kv_s, past['slc'], past['page_table'], idx, valid)
        o_w = win_sample(q_rot, pos, kv_w, past['win'])
    g = jax.nn.sigmoid(gt.astype(jnp.float32)).astype(h.dtype).reshape(B, T, N_KV, HPG, 3)
    b_out = (g[..., 0:1] * o_c + g[..., 1:2] * o_s + g[..., 2:3] * o_w).reshape(B, T, B_WIDTH)
    merged = jnp.concatenate([rmsnorm(a_out, lp['out_norm_a']), rmsnorm(b_out, lp['out_norm_b'])], axis=-1)
    return merged @ lp['w_out'], (kv_c, kv_s, kv_w, v)


def layer(x, c, pos, lp, past):
    mod = jax.nn.silu(c) @ lp['w_ada'] + lp['b_ada']
    sh1, sc1, gt1, sh2, sc2, gt2 = [m[:, None, :] for m in jnp.split(mod, 6, axis=-1)]
    h = rmsnorm(x, lp['pre_mix']) * (1.0 + sc1) + sh1
    m, st = mixer(h, pos, lp, past)
    x = x + gt1 * rmsnorm(m, lp['post_mix'])
    h = rmsnorm(x, lp['pre_ffn']) * (1.0 + sc2) + sh2
    f = jnp.square(jax.nn.relu(h @ lp['w_up'])) @ lp['w_down']
    x = x + gt2 * rmsnorm(f, lp['post_ffn'])
    return x, st


def setup_inputs(seed: int = 0) -> dict:
    key = jax.random.key(seed)
    ks = jax.random.split(key, 32)

    def nrm(k, shape, s):
        return jax.random.normal(k, shape, jnp.float32) * s

    n_pages = PAST_LEN // PAGE_SIZE
    n_phys = (DEC_BATCH * n_pages * 5) // 4
    w_buf = min(WINDOW, PAST_LEN)
    page_table = jax.random.permutation(ks[5], n_phys)[:DEC_BATCH * n_pages].reshape(DEC_BATCH, n_pages).astype(jnp.int32)
    return {
        'x_prompt': nrm(ks[0], (BATCH, SEQ, D_MODEL), 1.0),
        'x_sample': nrm(ks[1], (DEC_BATCH, DEC_SEQ, D_MODEL), 1.0),
        'cache_cmp': nrm(ks[2], (DEPTH, n_phys, PAGE_SIZE, N_KV, 2, HEAD_DIM), 1.0),
        'cache_slc': nrm(ks[3], (DEPTH, n_phys, PAGE_SIZE, N_KV, 2, HEAD_DIM), 1.0),
        'cache_win': nrm(ks[4], (DEPTH, DEC_BATCH, w_buf, N_KV, 2, HEAD_DIM), 1.0),
        'page_table': page_table,
        'c_prompt': nrm(ks[6], (BATCH, D_MODEL), 1.0),
        'c_sample': nrm(ks[7], (DEC_BATCH, D_MODEL), 1.0),
        'w_ada': nrm(ks[8], (DEPTH, D_MODEL, 6 * D_MODEL), 0.5 * D_MODEL ** -0.5),
        'b_ada': nrm(ks[9], (DEPTH, 6 * D_MODEL), 0.02),
        'norm_pre_mix': 1.0 + nrm(ks[10], (DEPTH, D_MODEL), 0.05),
        'norm_post_mix': 1.0 + nrm(ks[11], (DEPTH, D_MODEL), 0.05),
        'norm_pre_ffn': 1.0 + nrm(ks[12], (DEPTH, D_MODEL), 0.05),
        'norm_post_ffn': 1.0 + nrm(ks[13], (DEPTH, D_MODEL), 0.05),
        'w_in': nrm(ks[14], (DEPTH, D_MODEL, IN_COLS), D_MODEL ** -0.5),
        'ln_v_g': 1.0 + nrm(ks[15], (DEPTH, A_WIDTH), 0.05),
        'ln_v_b': nrm(ks[16], (DEPTH, A_WIDTH), 0.02),
        'sgu_w': nrm(ks[17], (DEPTH, A_GROUPS, CHUNK, CHUNK), CHUNK ** -0.5),
        'sgu_b': nrm(ks[18], (DEPTH, A_GROUPS, CHUNK), 0.02),
        'cmp_pe_k': nrm(ks[19], (DEPTH, CMP_BLOCK, HEAD_DIM), 0.1),
        'cmp_pe_v': nrm(ks[20], (DEPTH, CMP_BLOCK, HEAD_DIM), 0.1),
        'cmp_w1_k': nrm(ks[21], (DEPTH, CMP_BLOCK * HEAD_DIM, CMP_HIDDEN), (CMP_BLOCK * HEAD_DIM) ** -0.5),
        'cmp_w2_k': nrm(ks[22], (DEPTH, CMP_HIDDEN, HEAD_DIM), CMP_HIDDEN ** -0.5),
        'cmp_w1_v': nrm(ks[23], (DEPTH, CMP_BLOCK * HEAD_DIM, CMP_HIDDEN), (CMP_BLOCK * HEAD_DIM) ** -0.5),
        'cmp_w2_v': nrm(ks[24], (DEPTH, CMP_HIDDEN, HEAD_DIM), CMP_HIDDEN ** -0.5),
        'out_norm_a': 1.0 + nrm(ks[25], (DEPTH, A_WIDTH), 0.05),
        'out_norm_b': 1.0 + nrm(ks[26], (DEPTH, B_WIDTH), 0.05),
        'w_out': nrm(ks[27], (DEPTH, A_WIDTH + B_WIDTH, D_MODEL), (A_WIDTH + B_WIDTH) ** -0.5),
        'w_up': nrm(ks[28], (DEPTH, D_MODEL, D_FF), D_MODEL ** -0.5),
        'w_down': nrm(ks[29], (DEPTH, D_FF, D_MODEL), D_FF ** -0.5),
    }


def reference(x_prompt, x_sample, cache_cmp, cache_slc, cache_win, page_table, c_prompt, c_sample,
              w_ada, b_ada, norm_pre_mix, norm_post_mix, norm_pre_ffn, norm_post_ffn, w_in, ln_v_g, ln_v_b,
              sgu_w, sgu_b, cmp_pe_k, cmp_pe_v, cmp_w1_k, cmp_w2_k, cmp_w1_v, cmp_w2_v,
              out_norm_a, out_norm_b, w_out, w_up, w_down):
    pos_p = jnp.arange(x_prompt.shape[1])
    pos_s = PAST_LEN + jnp.arange(x_sample.shape[1])
    win_keep = min(WINDOW, x_prompt.shape[1])
    y_p, y_s = x_prompt, x_sample
    cmp_p, slc_p, win_p, cmp_s, slc_s, win_s, sgu_s = [], [], [], [], [], [], []
    for l in range(DEPTH):
        lp = {'w_ada': w_ada[l], 'b_ada': b_ada[l], 'pre_mix': norm_pre_mix[l], 'post_mix': norm_post_mix[l],
              'pre_ffn': norm_pre_ffn[l], 'post_ffn': norm_post_ffn[l], 'w_in': w_in[l],
              'ln_v_g': ln_v_g[l], 'ln_v_b': ln_v_b[l], 'sgu_w': sgu_w[l], 'sgu_b': sgu_b[l],
              'cmp_pe_k': cmp_pe_k[l], 'cmp_pe_v': cmp_pe_v[l], 'cmp_w1_k': cmp_w1_k[l], 'cmp_w2_k': cmp_w2_k[l],
              'cmp_w1_v': cmp_w1_v[l], 'cmp_w2_v': cmp_w2_v[l], 'out_norm_a': out_norm_a[l],
              'out_norm_b': out_norm_b[l], 'w_out': w_out[l], 'w_up': w_up[l], 'w_down': w_down[l]}
        y_p, st_p = layer(y_p, c_prompt, pos_p, lp, None)
        past = {'cmp': cache_cmp[l], 'slc': cache_slc[l], 'win': cache_win[l], 'page_table': page_table}
        y_s, st_s = layer(y_s, c_sample, pos_s, lp, past)
        cmp_p.append(st_p[0])
        slc_p.append(st_p[1])
        win_p.append(st_p[2][:, -win_keep:])
        cmp_s.append(st_s[0])
        slc_s.append(st_s[1])
        win_s.append(st_s[2])
        sgu_s.append(st_s[3])
    return (y_p, y_s, jnp.stack(cmp_p), jnp.stack(slc_p), jnp.stack(win_p),
            jnp.stack(cmp_s), jnp.stack(slc_s), jnp.stack(win_s), jnp.stack(sgu_s))
```

```python
import functools

import jax
import jax.numpy as jnp
from jax import lax
from jax.experimental import pallas as pl
from jax.experimental.pallas import tpu as pltpu

F32 = jnp.float32
BF16 = jnp.bfloat16
I32 = jnp.int32

LANES = 128
HEAD_DIM = 128
N_KV = 2
PAGE_SIZE = 128
CHUNK = 128
CMP_BLOCK = 32
CMP_STRIDE = 16
SLC_BLOCK = 64
N_SEL = 16
WINDOW = 512
ROPE_DIM = HEAD_DIM // 4
ROPE_HALF = ROPE_DIM // 2
ROPE_THETA = 500000.0
EPS = 1e-6
FORCE = 1e4
SCALE = HEAD_DIM ** -0.5
KV_COLS = N_KV * 2 * HEAD_DIM
CMP_STRIPS = CMP_BLOCK // CMP_STRIDE
STRIP_COLS = CMP_STRIDE * HEAD_DIM
CMP_PER_SLC = SLC_BLOCK // CMP_STRIDE
CMP_INSIDE = (SLC_BLOCK - CMP_BLOCK) // CMP_STRIDE + 1
VMEM_LIMIT = 56 * 2 ** 20


def _dot(a, b):
    return jnp.dot(a, b, preferred_element_type=F32)


def _dot_nt(a, b):
    return lax.dot_general(a, b, (((1,), (1,)), ((), ())), preferred_element_type=F32)


def _gelu(x):
    return 0.5 * x * (1.0 + jnp.tanh(0.7978845608028654 * (x + 0.044715 * (x * x * x))))


def _sigmoid(x):
    return 1.0 / (1.0 + jnp.exp(-x))


def _rms(x, g):
    return x * lax.rsqrt(jnp.mean(x * x, axis=-1, keepdims=True) + EPS) * g


def _softmax_neg_inf(s):
    m = jnp.max(s, axis=-1, keepdims=True)
    m = jnp.where(m > -jnp.inf, m, 0.0)
    e = jnp.exp(s - m)
    return e / jnp.maximum(jnp.sum(e, axis=-1, keepdims=True), 1e-30)


def _msoftmax(s, mask):
    return _softmax_neg_inf(jnp.where(mask, s, -jnp.inf))


def _rope(x, c, sa, sb):
    return x * c + pltpu.roll(x, LANES - ROPE_HALF, axis=1) * sa + pltpu.roll(x, ROPE_HALF, axis=1) * sb


def _split_dot(x, m):
    hi = x.astype(BF16)
    r1 = x - hi.astype(F32)
    mid = r1.astype(BF16)
    lo = (r1 - mid.astype(F32)).astype(BF16)
    return _dot(hi, m) + _dot(mid, m) + _dot(lo, m)


def _params(sem, vmem=None):
    return pltpu.CompilerParams(dimension_semantics=sem, vmem_limit_bytes=vmem)


def _ada_kernel(c_ref, w_ref, b_ref, o_ref):
    c = c_ref[...]
    o_ref[...] = _dot((c * _sigmoid(c)).astype(BF16), w_ref[...].astype(BF16)) + b_ref[...]


def _ada(c_all, w_ada, b_ada):
    depth, d, n = w_ada.shape
    rows = c_all.shape[0]
    tn = 1024
    return pl.pallas_call(
        _ada_kernel,
        grid=(depth, n // tn),
        in_specs=[pl.BlockSpec((rows, d), lambda l, j: (0, 0)),
                  pl.BlockSpec((None, d, tn), lambda l, j: (l, 0, j)),
                  pl.BlockSpec((None, 1, tn), lambda l, j: (l, 0, j))],
        out_specs=pl.BlockSpec((None, rows, tn), lambda l, j: (l, 0, j)),
        out_shape=jax.ShapeDtypeStruct((depth, rows, n), F32),
        compiler_params=_params(("arbitrary", "arbitrary"), VMEM_LIMIT),
        name="ada",
    )(c_all, w_ada, b_ada.reshape(depth, 1, n))


class _Rows:
    def __init__(self, mod, layer, d, m, tm, rows_per_batch, mod_row0):
        self.m, self.tm, self.d = m, tm, d
        depth, r, n = mod.shape
        if rows_per_batch is None:
            self.mod = mod
            self._spec = lambda k: pl.BlockSpec((None, tm, d), lambda i, *_: (layer, mod_row0 // tm, k))
        else:
            tiles = rows_per_batch // tm
            self.mod = mod.reshape(depth, r, 1, n)
            self._spec = lambda k: pl.BlockSpec((None, None, 1, d),
                                                lambda i, *_: (layer, mod_row0 + i // tiles, 0, k))

    def mod_spec(self, k):
        return self._spec(k)


def _vec_spec(layer, n):
    return pl.BlockSpec((None, 1, n), lambda *_: (layer, 0, 0))


def _in_kernel(x_ref, g_ref, sc_ref, sh_ref, w_ref, wg_ref, z_ref, gt_ref, h_ref):
    @pl.when(pl.program_id(1) == 0)
    def _():
        h = _rms(x_ref[...], g_ref[...]) * (1.0 + sc_ref[...]) + sh_ref[...]
        h_ref[...] = h.astype(h_ref.dtype)
        gt_ref[...] = _dot(h.astype(BF16), wg_ref[...])

    z_ref[...] = _dot(h_ref[...].astype(BF16), w_ref[...])


def _in_proj(x, rows, layer, norm_w, w_main, w_gate):
    m, tm, d = rows.m, rows.tm, rows.d
    n = w_main.shape[-1]
    tn = 768 if n % 768 == 0 else 512
    h_dtype = BF16 if tm % 16 == 0 else F32
    return pl.pallas_call(
        _in_kernel,
        grid=(m // tm, n // tn),
        in_specs=[pl.BlockSpec((tm, d), lambda i, j: (i, 0)),
                  _vec_spec(layer, d), rows.mod_spec(1), rows.mod_spec(0),
                  pl.BlockSpec((None, d, tn), lambda i, j: (layer, 0, j)),
                  pl.BlockSpec((None, d, LANES), lambda i, j: (layer, 0, 0))],
        out_specs=[pl.BlockSpec((tm, tn), lambda i, j: (i, j)),
                   pl.BlockSpec((tm, LANES), lambda i, j: (i, 0))],
        out_shape=[jax.ShapeDtypeStruct((m, n), F32), jax.ShapeDtypeStruct((m, LANES), F32)],
        scratch_shapes=[pltpu.VMEM((tm, d), h_dtype)],
        compiler_params=_params(("arbitrary", "arbitrary"), VMEM_LIMIT),
        name="in_proj",
    )(x, norm_w, rows.mod, rows.mod, w_main, w_gate)


def _layernorm(v, g, b):
    mu = jnp.mean(v, axis=-1, keepdims=True)
    vc = v - mu
    return vc * lax.rsqrt(jnp.mean(vc * vc, axis=-1, keepdims=True) + EPS) * g + b


def _sgu_kernel(u_ref, v_ref, lg_ref, lb_ref, w_ref, bt_ref, na_ref, o_ref, *, groups, chunks):
    row = lax.broadcasted_iota(I32, (CHUNK, CHUNK), 0)
    col = lax.broadcasted_iota(I32, (CHUNK, CHUNK), 1)
    ws = [jnp.where(row >= col, w_ref[g], 0.0).astype(BF16) for g in range(groups)]
    for c in range(chunks):
        rs = slice(c * CHUNK, (c + 1) * CHUNK)
        vn = _layernorm(_gelu(v_ref[rs, :]), lg_ref[...], lb_ref[...]).astype(BF16)
        u = _gelu(u_ref[rs, :])
        outs = []
        for g in range(groups):
            cs = slice(g * HEAD_DIM, (g + 1) * HEAD_DIM)
            outs.append(u[:, cs] * (_dot(ws[g], vn[:, cs]) + bt_ref[:, g:g + 1]))
        o_ref[rs, :] = _rms(jnp.concatenate(outs, axis=1), na_ref[...]).astype(BF16)


def _sgu(z, layer, a_width, ln_g, ln_b, sgu_w, sgu_bt, norm_a):
    m = z.shape[0]
    groups = a_width // HEAD_DIM
    chunks = 2
    tr = chunks * CHUNK
    return pl.pallas_call(
        functools.partial(_sgu_kernel, groups=groups, chunks=chunks),
        grid=(m // tr,),
        in_specs=[pl.BlockSpec((tr, a_width), lambda i: (i, 0)),
                  pl.BlockSpec((tr, a_width), lambda i: (i, 1)),
                  _vec_spec(layer, a_width), _vec_spec(layer, a_width),
                  pl.BlockSpec((None, groups, CHUNK, CHUNK), lambda i: (layer, 0, 0, 0)),
                  pl.BlockSpec((None, CHUNK, groups), lambda i: (layer, 0, 0)),
                  _vec_spec(layer, a_width)],
        out_specs=pl.BlockSpec((tr, a_width), lambda i: (i, 0)),
        out_shape=jax.ShapeDtypeStruct((m, a_width), BF16),
        compiler_params=_params(("arbitrary",), VMEM_LIMIT),
        name="sgu",
    )(z, z, ln_g, ln_b, sgu_w, sgu_bt, norm_a)


def _kv_kernel(kv_ref, rc_ref, rsa_ref, rsb_ref, cmp_ref, slc_ref, win_ref, slcb_ref, winb_ref):
    cmp_ref[...] = kv_ref[:, 0:KV_COLS]
    rc, rsa, rsb = rc_ref[...], rsa_ref[...], rsb_ref[...]
    for t, (o_ref, ob_ref) in enumerate(((slc_ref, slcb_ref), (win_ref, winb_ref))):
        for g in range(N_KV):
            k0 = (t + 1) * KV_COLS + g * 2 * HEAD_DIM
            k = _rope(kv_ref[:, k0:k0 + HEAD_DIM], rc, rsa, rsb)
            v = kv_ref[:, k0 + HEAD_DIM:k0 + 2 * HEAD_DIM]
            o0 = g * 2 * HEAD_DIM
            o_ref[:, o0:o0 + HEAD_DIM] = k
            o_ref[:, o0 + HEAD_DIM:o0 + 2 * HEAD_DIM] = v
            ob_ref[:, o0:o0 + HEAD_DIM] = k.astype(BF16)
            ob_ref[:, o0 + HEAD_DIM:o0 + 2 * HEAD_DIM] = v.astype(BF16)


def _kv_rope(z, kv_block, tables, tr):
    m = z.shape[0]
    t_rows = tables[0].shape[0]
    nt = t_rows // tr
    tspec = pl.BlockSpec((tr, LANES), lambda i: (i % nt, 0))
    ospec = pl.BlockSpec((tr, KV_COLS), lambda i: (i, 0))
    return pl.pallas_call(
        _kv_kernel,
        grid=(m // tr,),
        in_specs=[pl.BlockSpec((tr, 3 * KV_COLS), lambda i: (i, kv_block)), tspec, tspec, tspec],
        out_specs=[ospec] * 5,
        out_shape=[jax.ShapeDtypeStruct((m, KV_COLS), F32)] * 3 + [jax.ShapeDtypeStruct((m, KV_COLS), BF16)] * 2,
        compiler_params=_params(("arbitrary",), VMEM_LIMIT),
        name="kv_rope",
    )(z, *tables)


def _compress_strips(xf, pe_ref, w1_ref, sel):
    p = _dot((xf + pe_ref[sel, 0:1, :]).astype(BF16), w1_ref[sel, 0:STRIP_COLS, :])
    q = _dot((xf + pe_ref[sel, 1:2, :]).astype(BF16), w1_ref[sel, STRIP_COLS:2 * STRIP_COLS, :])
    return p, q


def _cmp_kernel(x_ref, pe_ref, w1_ref, w2_ref, o_ref, *, n):
    sel = pl.program_id(1) % 2
    xf = jnp.concatenate([x_ref[pl.ds(r, n, stride=CMP_STRIDE), :] for r in range(CMP_STRIDE)], axis=1)
    p, q = _compress_strips(xf, pe_ref, w1_ref, sel)
    hid = _gelu(p + pltpu.roll(q, n - 1, axis=0))
    o_ref[...] = _dot(hid.astype(BF16), w2_ref[sel]).astype(BF16)


def _compress_prompt(z, batch, seq, cmp_block, pe2, w1, w2):
    n = seq // CMP_STRIDE
    hidden = w1.shape[-1]
    heads = 2 * N_KV
    return pl.pallas_call(
        functools.partial(_cmp_kernel, n=n),
        grid=(batch, heads),
        in_specs=[pl.BlockSpec((seq, HEAD_DIM), lambda b, h: (b, cmp_block * heads + h)),
                  pl.BlockSpec((2, 2, STRIP_COLS), lambda b, h: (0, 0, 0)),
                  pl.BlockSpec((2, 2 * STRIP_COLS, hidden), lambda b, h: (0, 0, 0)),
                  pl.BlockSpec((2, hidden, HEAD_DIM), lambda b, h: (0, 0, 0))],
        out_specs=pl.BlockSpec((None, None, n, HEAD_DIM), lambda b, h: (b, h, 0, 0)),
        out_shape=jax.ShapeDtypeStruct((batch, heads, n, HEAD_DIM), BF16),
        compiler_params=_params(("arbitrary", "arbitrary"), VMEM_LIMIT),
        name="compress_prompt",
    )(z, pe2, w1, w2)


def _attn_kernel(q_ref, gt_ref, cmp_ref, slc_ref, win_ref, rc_ref, rsa_ref, rsb_ref, nb_ref, sel_ref, exp_ref,
                 o_ref, acc_ref, *, tq, seq, hpg, n_cmp):
    q0 = pl.program_id(1) * tq
    pos = q0 + lax.broadcasted_iota(I32, (tq, 1), 0)
    posh = jnp.concatenate([pos] * hpg, axis=0)
    rc = jnp.concatenate([rc_ref[...]] * hpg, axis=0)
    rsa = jnp.concatenate([rsa_ref[...]] * hpg, axis=0)
    rsb = jnp.concatenate([rsb_ref[...]] * hpg, axis=0)
    sig = _sigmoid(gt_ref[...])
    lane = lax.broadcasted_iota(I32, (1, LANES), 1)
    clane = lax.broadcasted_iota(I32, (1, n_cmp), 1)
    cur = pos // SLC_BLOCK
    kpos = lax.broadcasted_iota(I32, (1, seq), 1)
    n_blk = seq // SLC_BLOCK
    n_sel = min(N_SEL, n_blk)
    wk = WINDOW + tq
    wstart = pl.multiple_of(jnp.maximum(q0 - WINDOW, 0), LANES)
    wpos = wstart + lax.broadcasted_iota(I32, (1, wk), 1)
    wdiff = posh - wpos
    wmask = (wdiff >= 0) & (wdiff < WINDOW)
    cmask = (clane * CMP_STRIDE + (CMP_BLOCK - 1)) <= posh
    for g in range(N_KV):
        heads = [g * hpg + h for h in range(hpg)]
        qc = jnp.concatenate([q_ref[:, hd * HEAD_DIM:(hd + 1) * HEAD_DIM] for hd in heads], axis=0)
        qrb = _rope(qc, rc, rsa, rsb).astype(BF16)
        k0 = g * 2 * HEAD_DIM
        v0 = k0 + HEAD_DIM
        p = _msoftmax(_dot_nt(qc.astype(BF16), cmp_ref[2 * g]) * SCALE, cmask)
        o_c = _dot(p.astype(BF16), cmp_ref[2 * g + 1])
        imp = p[0:tq]
        for h in range(1, hpg):
            imp = imp + p[h * tq:(h + 1) * tq]
        impb = _split_dot(imp, sel_ref[...])
        forced = (lane == 0) | (lane == cur) | (lane == cur - 1)
        score = jnp.where(lane <= cur, jnp.where(forced, FORCE, impb), -FORCE)
        rank = jnp.zeros((tq, LANES), F32)
        for i in range(n_blk):
            ci = score[:, i:i + 1]
            ahead = (ci > score) | ((ci == score) & (lane > i))
            rank = rank + jnp.where(ahead, 1.0, 0.0)
        chosen = (rank < float(n_sel)) & (lane <= cur)
        selk = _dot(jnp.where(chosen, 1.0, 0.0).astype(BF16), exp_ref[...])
        bias = jnp.where((selk > 0.5) & (kpos <= pos), 0.0, -jnp.inf)
        s = _dot_nt(qrb, slc_ref[:, k0:k0 + HEAD_DIM]) * SCALE + jnp.concatenate([bias] * hpg, axis=0)
        o_s = _dot(_softmax_neg_inf(s).astype(BF16), slc_ref[:, v0:v0 + HEAD_DIM])
        s = _dot_nt(qrb, win_ref[pl.ds(wstart, wk), k0:k0 + HEAD_DIM]) * SCALE
        o_w = _dot(_msoftmax(s, wmask).astype(BF16), win_ref[pl.ds(wstart, wk), v0:v0 + HEAD_DIM])
        for h, hd in enumerate(heads):
            rs = slice(h * tq, (h + 1) * tq)
            acc_ref[:, hd * HEAD_DIM:(hd + 1) * HEAD_DIM] = (
                sig[:, 3 * hd:3 * hd + 1] * o_c[rs] + sig[:, 3 * hd + 1:3 * hd + 2] * o_s[rs]
                + sig[:, 3 * hd + 2:3 * hd + 3] * o_w[rs])
    o_ref[...] = _rms(acc_ref[...], nb_ref[...]).astype(BF16)


def _attn_prompt(z, gt, kvc, slc_b, win_b, tables, layer, norm_b, sel_m, exp_m, batch, seq, b_width, q_block):
    tq = 128
    nq = seq // tq
    hpg = b_width // HEAD_DIM // N_KV
    n_cmp = kvc.shape[2]
    tspec = pl.BlockSpec((tq, LANES), lambda b, i: (i, 0))
    kvspec = pl.BlockSpec((seq, KV_COLS), lambda b, i: (b, 0))
    return pl.pallas_call(
        functools.partial(_attn_kernel, tq=tq, seq=seq, hpg=hpg, n_cmp=n_cmp),
        grid=(batch, nq),
        in_specs=[pl.BlockSpec((tq, b_width), lambda b, i: (b * nq + i, q_block)),
                  pl.BlockSpec((tq, LANES), lambda b, i: (b * nq + i, 0)),
                  pl.BlockSpec((None, 2 * N_KV, n_cmp, HEAD_DIM), lambda b, i: (b, 0, 0, 0)),
                  kvspec, kvspec, tspec, tspec, tspec,
                  pl.BlockSpec((None, 1, b_width), lambda b, i: (layer, 0, 0)),
                  pl.BlockSpec(sel_m.shape, lambda b, i: (0, 0)),
                  pl.BlockSpec(exp_m.shape, lambda b, i: (0, 0))],
        out_specs=pl.BlockSpec((tq, b_width), lambda b, i: (b * nq + i, 0)),
        out_shape=jax.ShapeDtypeStruct((batch * seq, b_width), BF16),
        scratch_shapes=[pltpu.VMEM((tq, b_width), F32)],
        compiler_params=_params(("arbitrary", "arbitrary"), VMEM_LIMIT),
        name="attn_prompt",
    )(z, gt, kvc, slc_b, win_b, *tables, norm_b, sel_m, exp_m)


def _out_kernel(a_ref, b_ref, w_ref, x_ref, gate_ref, g_ref, o_ref, *, a_width):
    m = _dot(a_ref[...], w_ref[0:a_width, :]) + _dot(b_ref[...], w_ref[a_width:, :])
    o_ref[...] = x_ref[...] + gate_ref[...] * _rms(m, g_ref[...])


def _out_proj(a_n, b_n, x, rows, layer, w_out, norm_w):
    m, tm, d = rows.m, rows.tm, rows.d
    a_width = a_n.shape[1]
    b_width = b_n.shape[1]
    return pl.pallas_call(
        functools.partial(_out_kernel, a_width=a_width),
        grid=(m // tm,),
        in_specs=[pl.BlockSpec((tm, a_width), lambda i: (i, 0)),
                  pl.BlockSpec((tm, b_width), lambda i: (i, 0)),
                  pl.BlockSpec((None, a_width + b_width, d), lambda i: (layer, 0, 0)),
                  pl.BlockSpec((tm, d), lambda i: (i, 0)),
                  rows.mod_spec(2), _vec_spec(layer, d)],
        out_specs=pl.BlockSpec((tm, d), lambda i: (i, 0)),
        out_shape=jax.ShapeDtypeStruct((m, d), F32),
        compiler_params=_params(("arbitrary",), VMEM_LIMIT),
        name="out_proj",
    )(a_n, b_n, w_out, x, rows.mod, norm_w)


def _ffn_kernel(x_ref, g1_ref, sc_ref, sh_ref, wu_ref, wd_ref, gate_ref, g2_ref, o_ref, h_ref, acc_ref):
    k = pl.program_id(1)

    @pl.when(k == 0)
    def _():
        h = _rms(x_ref[...], g1_ref[...]) * (1.0 + sc_ref[...]) + sh_ref[...]
        h_ref[...] = h.astype(h_ref.dtype)
        acc_ref[...] = jnp.zeros_like(acc_ref)

    up = jnp.maximum(_dot(h_ref[...].astype(BF16), wu_ref[...]), 0.0)
    acc_ref[...] += _dot((up * up).astype(BF16), wd_ref[...])

    @pl.when(k == pl.num_programs(1) - 1)
    def _():
        o_ref[...] = x_ref[...] + gate_ref[...] * _rms(acc_ref[...], g2_ref[...])


def _ffn(x, rows, layer, norm_pre, norm_post, w_up, w_down):
    m, tm, d = rows.m, rows.tm, rows.d
    d_ff = w_up.shape[-1]
    tf = 512
    h_dtype = BF16 if tm % 16 == 0 else F32
    return pl.pallas_call(
        _ffn_kernel,
        grid=(m // tm, d_ff // tf),
        in_specs=[pl.BlockSpec((tm, d), lambda i, k: (i, 0)),
                  _vec_spec(layer, d), rows.mod_spec(4), rows.mod_spec(3),
                  pl.BlockSpec((None, d, tf), lambda i, k: (layer, 0, k)),
                  pl.BlockSpec((None, tf, d), lambda i, k: (layer, k, 0)),
                  rows.mod_spec(5), _vec_spec(layer, d)],
        out_specs=pl.BlockSpec((tm, d), lambda i, k: (i, 0)),
        out_shape=jax.ShapeDtypeStruct((m, d), F32),
        scratch_shapes=[pltpu.VMEM((tm, d), h_dtype), pltpu.VMEM((tm, d), F32)],
        compiler_params=_params(("arbitrary", "arbitrary"), VMEM_LIMIT),
        name="ffn",
    )(x, norm_pre, rows.mod, rows.mod, w_up, w_down, rows.mod, norm_post)


def _scmp_kernel(pt_ref, cache_ref, new_ref, pe_ref, w1_ref, w2_ref, o_ref, buf, sem, p_s, q_s,
                 *, layer, pages, n_chunk, n_batch):
    b = pl.program_id(0)
    c = pl.program_id(1)
    step = b * n_chunk + c
    slot = step % 2
    strips = pages * (PAGE_SIZE // CMP_STRIDE)
    n_past = n_chunk * strips

    def copies(bb, cc, sl):
        return [pltpu.make_async_copy(cache_ref.at[layer, pt_ref[bb, cc * pages + p], :, gs, :],
                                      buf.at[sl, gs, p], sem.at[sl])
                for p in range(pages) for gs in range(2 * N_KV)]

    @pl.when(step == 0)
    def _():
        for cp in copies(0, 0, 0):
            cp.start()

    @pl.when(step + 1 < n_batch * n_chunk)
    def _():
        nxt = step + 1
        for cp in copies(nxt // n_chunk, nxt % n_chunk, 1 - slot):
            cp.start()

    for cp in copies(b, c, slot):
        cp.wait()

    cur = buf.at[slot]
    row0 = pl.multiple_of(c * strips, strips)
    for gs in range(2 * N_KV):
        sel = gs % 2
        xf = jnp.concatenate(
            [cur[gs, :, pl.ds(r, PAGE_SIZE // CMP_STRIDE, stride=CMP_STRIDE), :].reshape(strips, HEAD_DIM)
             for r in range(CMP_STRIDE)], axis=1)
        p, q = _compress_strips(xf, pe_ref, w1_ref, sel)
        p_s[gs, pl.ds(row0, strips), :] = p
        q_s[gs, pl.ds(row0, strips), :] = q

    @pl.when(c == n_chunk - 1)
    def _():
        for gs in range(2 * N_KV):
            sel = gs % 2
            new = new_ref[:, gs * HEAD_DIM:(gs + 1) * HEAD_DIM]
            xn = jnp.concatenate([new, jnp.zeros((1, STRIP_COLS - HEAD_DIM), F32)], axis=1) + pe_ref[sel, 1:2, :]
            xn = jnp.broadcast_to(xn, (8, STRIP_COLS)).astype(BF16)
            q_s[gs, n_past:n_past + 8, :] = _dot(xn, w1_ref[sel, STRIP_COLS:2 * STRIP_COLS, :])
            hid = _gelu(p_s[gs] + q_s[gs, 1:n_past + 1, :])
            o_ref[gs] = _dot(hid.astype(BF16), w2_ref[sel]).astype(BF16)


def _compress_sample(page_table, cache, layer, new_cmp, pe2, w1, w2):
    n_batch, n_pages = page_table.shape
    pages = 32 if n_pages % 32 == 0 else n_pages
    n_chunk = n_pages // pages
    n_past = n_pages * (PAGE_SIZE // CMP_STRIDE)
    hidden = w1.shape[-1]
    grid_spec = pltpu.PrefetchScalarGridSpec(
        num_scalar_prefetch=1,
        grid=(n_batch, n_chunk),
        in_specs=[pl.BlockSpec(memory_space=pl.ANY),
                  pl.BlockSpec((None, 1, KV_COLS), lambda b, c, pt: (b, 0, 0)),
                  pl.BlockSpec((2, 2, STRIP_COLS), lambda b, c, pt: (0, 0, 0)),
                  pl.BlockSpec((2, 2 * STRIP_COLS, hidden), lambda b, c, pt: (0, 0, 0)),
                  pl.BlockSpec((2, hidden, HEAD_DIM), lambda b, c, pt: (0, 0, 0))],
        out_specs=pl.BlockSpec((None, 2 * N_KV, n_past, HEAD_DIM), lambda b, c, pt: (b, 0, 0, 0)),
        scratch_shapes=[pltpu.VMEM((2, 2 * N_KV, pages, PAGE_SIZE, HEAD_DIM), F32),
                        pltpu.SemaphoreType.DMA((2,)),
                        pltpu.VMEM((2 * N_KV, n_past, hidden), F32),
                        pltpu.VMEM((2 * N_KV, n_past + 8, hidden), F32)])
    return pl.pallas_call(
        functools.partial(_scmp_kernel, layer=layer, pages=pages, n_chunk=n_chunk, n_batch=n_batch),
        grid_spec=grid_spec,
        out_shape=jax.ShapeDtypeStruct((n_batch, 2 * N_KV, n_past, HEAD_DIM), BF16),
        compiler_params=_params(("arbitrary", "arbitrary"), VMEM_LIMIT),
        name="compress_sample",
    )(page_table, cache, new_cmp.reshape(n_batch, 1, KV_COLS), pe2, w1, w2)


def _ssel_kernel(q_ref, kvc_ref, sel_ref, oc_ref, idx_ref, *, pos, hpg, n_cmp, n_lane):
    qb = q_ref[...].astype(BF16)
    n_heads = qb.shape[0]
    head_row = lax.broadcasted_iota(I32, (n_heads, 1), 0)
    clane = lax.broadcasted_iota(I32, (1, n_cmp), 1)
    cmask = (clane * CMP_STRIDE + (CMP_BLOCK - 1)) <= pos
    lane = lax.broadcasted_iota(I32, (1, n_lane), 1)
    cur = pos // SLC_BLOCK
    forced = (lane == 0) | (lane == cur) | (lane == cur - 1)
    ri = lax.broadcasted_iota(I32, (n_lane, n_lane), 0)
    ci = lax.broadcasted_iota(I32, (n_lane, n_lane), 1)
    slot_id = lax.broadcasted_iota(I32, (N_SEL, 1), 0)
    o_c = jnp.zeros((n_heads, HEAD_DIM), F32)
    for g in range(N_KV):
        mine = (head_row // hpg) == g
        p = _msoftmax(_dot_nt(qb, kvc_ref[2 * g]) * SCALE, cmask)
        o_c = jnp.where(mine, _dot(p.astype(BF16), kvc_ref[2 * g + 1]), o_c)
        imp = jnp.sum(jnp.where(mine, p, 0.0), axis=0, keepdims=True)
        impb = _split_dot(jnp.broadcast_to(imp, (8, n_cmp)), sel_ref[...])[0:1]
        score = jnp.where(lane <= cur, jnp.where(forced, FORCE, impb), -FORCE)
        srow = jnp.broadcast_to(score, (n_lane, n_lane))
        scol = srow.T
        ahead = (scol > srow) | ((scol == srow) & (ri < ci))
        rank = jnp.sum(jnp.where(ahead, 1.0, 0.0), axis=0, keepdims=True)
        chosen = jnp.where((rank < float(N_SEL)) & (lane <= cur), 1.0, 0.0)
        ccol = jnp.broadcast_to(chosen, (n_lane, n_lane)).T
        before = jnp.sum(jnp.where(ri < ci, ccol, 0.0), axis=0, keepdims=True)
        hit = (chosen > 0.5) & (before == slot_id.astype(F32))
        idx = jnp.sum(jnp.where(hit, lane.astype(F32), 0.0), axis=1, keepdims=True)
        idx_ref[g] = jnp.broadcast_to(idx, (N_SEL, LANES)).astype(I32)
    oc_ref[...] = o_c


def _select_sample(q3, kvc, sel_m, pos, hpg):
    n_batch, n_heads, _ = q3.shape
    n_cmp = kvc.shape[2]
    n_lane = sel_m.shape[1]
    return pl.pallas_call(
        functools.partial(_ssel_kernel, pos=pos, hpg=hpg, n_cmp=n_cmp, n_lane=n_lane),
        grid=(n_batch,),
        in_specs=[pl.BlockSpec((None, n_heads, HEAD_DIM), lambda b: (b, 0, 0)),
                  pl.BlockSpec((None, 2 * N_KV, n_cmp, HEAD_DIM), lambda b: (b, 0, 0, 0)),
                  pl.BlockSpec(sel_m.shape, lambda b: (0, 0))],
        out_specs=[pl.BlockSpec((None, n_heads, HEAD_DIM), lambda b: (b, 0, 0)),
                   pl.BlockSpec((None, N_KV, N_SEL, LANES), lambda b: (b, 0, 0, 0))],
        out_shape=[jax.ShapeDtypeStruct((n_batch, n_heads, HEAD_DIM), F32),
                   jax.ShapeDtypeStruct((n_batch, N_KV, N_SEL, LANES), I32)],
        compiler_params=_params(("arbitrary",), VMEM_LIMIT),
        name="select_sample",
    )(q3, kvc, sel_m)


def _smix_kernel(pt_ref, idx_ref, cache_ref, q_ref, oc_ref, gt_ref, slc_ref, win_ref, wbuf_ref, rope_ref,
                 u_ref, v_ref, lg_ref, lb_ref, w0_ref, b0_ref, na_ref, nb_ref,
                 a_ref, b_ref, vn_ref, buf, sem, *, layer, pos, hpg, nb_past):
    b = pl.program_id(0)
    halves = PAGE_SIZE // SLC_BLOCK

    def copies():
        out = []
        for g in range(N_KV):
            for k in range(N_SEL):
                jp = jnp.minimum(idx_ref[b, g, k], nb_past - 1)
                phys = pt_ref[b, jp // halves]
                out.append(pltpu.make_async_copy(cache_ref.at[layer, phys, jp % halves, :, g, :], buf.at[g, k],
                                                 sem.at[0]))
        return out

    for cp in copies():
        cp.start()

    vn = _layernorm(_gelu(v_ref[...]), lg_ref[...], lb_ref[...])
    vn_ref[...] = vn
    a_out = _gelu(u_ref[...]) * (w0_ref[...] * vn + b0_ref[...])
    a_ref[...] = _rms(a_out, na_ref[...]).astype(BF16)

    q = q_ref[...]
    n_heads = q.shape[0]
    head_row = lax.broadcasted_iota(I32, (n_heads, 1), 0)
    qrb = _rope(q, rope_ref[0:1, :], rope_ref[1:2, :], rope_ref[2:3, :]).astype(BF16)
    sig = _sigmoid(gt_ref[...])
    n_keys = N_SEL * SLC_BLOCK
    klane = lax.broadcasted_iota(I32, (1, n_keys), 1)
    wb = wbuf_ref.shape[0]
    wdiff = pos - (pos - wb + lax.broadcasted_iota(I32, (1, wb), 1))
    wmask = (wdiff >= 0) & (wdiff < WINDOW)

    for cp in copies():
        cp.wait()

    o_s = jnp.zeros((n_heads, HEAD_DIM), F32)
    o_w = jnp.zeros((n_heads, HEAD_DIM), F32)
    for g in range(N_KV):
        mine = (head_row // hpg) == g
        k0 = g * 2 * HEAD_DIM
        v0 = k0 + HEAD_DIM
        kmask = jnp.zeros((1, n_keys), F32)
        has_new = jnp.zeros((1, 1), F32)
        for k in range(N_SEL):
            blk = idx_ref[b, g, k]
            in_k = (klane // SLC_BLOCK) == k
            kmask = jnp.where(in_k & (blk < nb_past), 1.0, kmask)
            has_new = jnp.where(blk >= nb_past, 1.0, has_new)
        kv = buf[g].reshape(n_keys, 2 * HEAD_DIM).astype(BF16)
        s = jnp.where(kmask > 0.5, _dot_nt(qrb, kv[:, 0:HEAD_DIM]) * SCALE, -jnp.inf)
        k_new = slc_ref[:, k0:k0 + HEAD_DIM].astype(BF16)
        v_new = slc_ref[:, v0:v0 + HEAD_DIM].astype(BF16)
        s_new = jnp.sum(qrb.astype(F32) * k_new.astype(F32), axis=-1, keepdims=True) * SCALE
        s_new = jnp.where(has_new > 0.5, s_new, -jnp.inf)
        m = jnp.maximum(jnp.max(s, axis=-1, keepdims=True), s_new)
        m = jnp.where(m > -jnp.inf, m, 0.0)
        e = jnp.exp(s - m)
        e_new = jnp.exp(s_new - m)
        den = jnp.maximum(jnp.sum(e, axis=-1, keepdims=True) + e_new, 1e-30)
        og = _dot((e / den).astype(BF16), kv[:, HEAD_DIM:]) + (e_new / den).astype(BF16).astype(F32) * v_new.astype(F32)
        o_s = jnp.where(mine, og, o_s)
        wkv = wbuf_ref[:, k0:k0 + 2 * HEAD_DIM].astype(BF16)
        s = jnp.where(wmask, _dot_nt(qrb, wkv[:, 0:HEAD_DIM]) * SCALE, -jnp.inf)
        k_new = win_ref[:, k0:k0 + HEAD_DIM].astype(BF16)
        v_new = win_ref[:, v0:v0 + HEAD_DIM].astype(BF16)
        s_new = jnp.sum(qrb.astype(F32) * k_new.astype(F32), axis=-1, keepdims=True) * SCALE
        m = jnp.maximum(jnp.max(s, axis=-1, keepdims=True), s_new)
        e = jnp.exp(s - m)
        e_new = jnp.exp(s_new - m)
        den = jnp.maximum(jnp.sum(e, axis=-1, keepdims=True) + e_new, 1e-30)
        og = _dot((e / den).astype(BF16), wkv[:, HEAD_DIM:]) + (e_new / den).astype(BF16).astype(F32) * v_new.astype(F32)
        o_w = jnp.where(mine, og, o_w)
    b_out = sig[:, 0:1] * oc_ref[...] + sig[:, 1:2] * o_s + sig[:, 2:3] * o_w
    ms = jnp.sum(jnp.sum(b_out * b_out, axis=-1, keepdims=True), axis=0, keepdims=True) / float(n_heads * HEAD_DIM)
    b_ref[...] = (b_out * lax.rsqrt(ms + EPS) * nb_ref[...]).astype(BF16)


def _mix_sample(page_table, idx, cache5, q3, o_c, gt3, slc_new, win_new, win_buf, rope_s, z3, layer,
                ln_g, ln_b, w0, b0, norm_a, norm_b3, pos, hpg, a_width):
    n_batch, n_heads, _ = q3.shape
    wb = win_buf.shape[2]
    nb_past = page_table.shape[1] * (PAGE_SIZE // SLC_BLOCK)
    head_spec = pl.BlockSpec((None, n_heads, HEAD_DIM), lambda b, *_: (b, 0, 0))
    new_spec = pl.BlockSpec((None, 1, KV_COLS), lambda b, *_: (b, 0, 0))
    row_spec = lambda blk: pl.BlockSpec((None, 1, a_width), lambda b, *_: (b, 0, blk))
    vec = lambda: pl.BlockSpec((None, 1, a_width), lambda b, *_: (layer, 0, 0))
    grid_spec = pltpu.PrefetchScalarGridSpec(
        num_scalar_prefetch=2,
        grid=(n_batch,),
        in_specs=[pl.BlockSpec(memory_space=pl.ANY), head_spec, head_spec,
                  pl.BlockSpec((None, n_heads, 3), lambda b, *_: (b, 0, 0)),
                  new_spec, new_spec,
                  pl.BlockSpec((None, None, wb, KV_COLS), lambda b, *_: (layer, b, 0, 0)),
                  pl.BlockSpec(rope_s.shape, lambda b, *_: (0, 0)),
                  row_spec(0), row_spec(1), vec(), vec(), vec(), vec(), vec(),
                  pl.BlockSpec((None, n_heads, HEAD_DIM), lambda b, *_: (layer, 0, 0))],
        out_specs=[pl.BlockSpec((None, 1, a_width), lambda b, *_: (b, 0, 0)),
                   head_spec,
                   pl.BlockSpec((None, 1, a_width), lambda b, *_: (b, 0, 0))],
        scratch_shapes=[pltpu.VMEM((N_KV, N_SEL, SLC_BLOCK, 2 * HEAD_DIM), F32),
                        pltpu.SemaphoreType.DMA((1,))])
    return pl.pallas_call(
        functools.partial(_smix_kernel, layer=layer, pos=pos, hpg=hpg, nb_past=nb_past),
        grid_spec=grid_spec,
        out_shape=[jax.ShapeDtypeStruct((n_batch, 1, a_width), BF16),
                   jax.ShapeDtypeStruct((n_batch, n_heads, HEAD_DIM), BF16),
                   jax.ShapeDtypeStruct((n_batch, 1, a_width), F32)],
        compiler_params=_params(("arbitrary",), VMEM_LIMIT),
        name="mix_sample",
    )(page_table, idx, cache5, q3, o_c, gt3, slc_new.reshape(n_batch, 1, KV_COLS),
      win_new.reshape(n_batch, 1, KV_COLS), win_buf, rope_s, z3, z3,
      ln_g, ln_b, w0, b0, norm_a, norm_b3)


def _rope_tables(pos):
    inv = ROPE_THETA ** (-jnp.arange(ROPE_HALF, dtype=F32) / ROPE_HALF)
    ang = pos.astype(F32)[:, None] * inv[None, :]
    cos, sin = jnp.cos(ang), jnp.sin(ang)
    n = pos.shape[0]
    zeros = lambda w: jnp.zeros((n, w), F32)
    c = jnp.concatenate([cos, cos, jnp.ones((n, HEAD_DIM - ROPE_DIM), F32)], axis=1)
    sa = jnp.concatenate([-sin, zeros(HEAD_DIM - ROPE_HALF)], axis=1)
    sb = jnp.concatenate([zeros(ROPE_HALF), sin, zeros(HEAD_DIM - ROPE_DIM)], axis=1)
    return c, sa, sb


def _block_sum_matrix(n_cmp, n_lane):
    i = jnp.arange(n_cmp)[:, None]
    j = jnp.arange(n_lane)[None, :]
    return ((i // CMP_PER_SLC == j) & (i % CMP_PER_SLC < CMP_INSIDE)).astype(BF16)


def kernel(x_prompt, x_sample, cache_cmp, cache_slc, cache_win, page_table, c_prompt, c_sample, w_ada, b_ada, norm_pre_mix, norm_post_mix, norm_pre_ffn, norm_post_ffn, w_in, ln_v_g, ln_v_b, sgu_w, sgu_b, cmp_pe_k, cmp_pe_v, cmp_w1_k, cmp_w2_k, cmp_w1_v, cmp_w2_v, out_norm_a, out_norm_b, w_out, w_up, w_down):
    batch, seq, d = x_prompt.shape
    n_dec, dec_seq, _ = x_sample.shape
    depth = w_ada.shape[0]
    a_width = d // 2
    b_width = d - a_width
    groups = a_width // HEAD_DIM
    n_heads = b_width // HEAD_DIM
    hpg = n_heads // N_KV
    n_main = 2 * a_width + b_width + 3 * KV_COLS
    n_gate = 3 * n_heads
    n_pages = page_table.shape[1]
    past = n_pages * PAGE_SIZE
    n_phys = cache_cmp.shape[1]
    assert dec_seq == 1 and n_dec % 8 == 0 and a_width == b_width and n_gate <= LANES
    assert (2 * a_width + b_width) % (3 * KV_COLS) == 0 and seq % 256 == 0 and seq >= WINDOW + 128
    assert past % SLC_BLOCK == 0 and past // SLC_BLOCK + 1 > N_SEL and cache_win.shape[2] <= past
    q_block = 2 * a_width // b_width
    kv_block = (2 * a_width + b_width) // (3 * KV_COLS)
    cmp_block = (2 * a_width + b_width) // KV_COLS
    m_p = batch * seq

    vec3 = lambda a: a.reshape(depth, 1, a.shape[-1])
    w_main = w_in[:, :, :n_main].astype(BF16)
    w_gate = jnp.pad(w_in[:, :, n_main:], ((0, 0), (0, 0), (0, LANES - n_gate))).astype(BF16)
    w_out_b, w_up_b, w_down_b = w_out.astype(BF16), w_up.astype(BF16), w_down.astype(BF16)
    cmp_w1 = jnp.stack([cmp_w1_k, cmp_w1_v], axis=1).astype(BF16)
    cmp_w2 = jnp.stack([cmp_w2_k, cmp_w2_v], axis=1).astype(BF16)
    cmp_pe = jnp.stack([cmp_pe_k, cmp_pe_v], axis=1).reshape(depth, 2, CMP_STRIPS, STRIP_COLS)
    pre_mix, post_mix, pre_ffn, post_ffn = map(vec3, (norm_pre_mix, norm_post_mix, norm_pre_ffn, norm_post_ffn))
    ln_g, ln_b, norm_a, norm_b = map(vec3, (ln_v_g, ln_v_b, out_norm_a, out_norm_b))
    norm_b3 = out_norm_b.reshape(depth, n_heads, HEAD_DIM)
    sgu_bt = jnp.swapaxes(sgu_b, 1, 2)
    sgu_w0 = jnp.repeat(sgu_w[:, :, 0, 0], HEAD_DIM, axis=1).reshape(depth, 1, a_width)
    sgu_b0 = jnp.repeat(sgu_b[:, :, 0], HEAD_DIM, axis=1).reshape(depth, 1, a_width)

    mod_rows = -(-(n_dec + batch) // 8) * 8
    c_all = jnp.concatenate([c_sample, c_prompt, jnp.zeros((mod_rows - n_dec - batch, d), F32)], axis=0)
    mod = _ada(c_all, w_ada, b_ada)

    tables_p = _rope_tables(jnp.arange(seq))
    tables_s = _rope_tables(jnp.full((n_dec,), past))
    rope_s = jnp.concatenate([t[0:1] for t in tables_s] + [jnp.zeros((5, LANES), F32)], axis=0)
    n_cmp_p = seq // CMP_STRIDE
    sel_p = _block_sum_matrix(n_cmp_p, LANES)
    exp_p = (jnp.arange(LANES)[:, None] == (jnp.arange(seq) // SLC_BLOCK)[None, :]).astype(BF16)
    n_cmp_s = past // CMP_STRIDE
    nb_lane = -(-(past // SLC_BLOCK + 1) // LANES) * LANES
    sel_s = _block_sum_matrix(n_cmp_s, nb_lane)

    cache_cmp_r = cache_cmp.reshape(depth, n_phys, PAGE_SIZE, 2 * N_KV, HEAD_DIM)
    cache_slc_r = cache_slc.reshape(depth, n_phys, PAGE_SIZE // SLC_BLOCK, SLC_BLOCK, N_KV, 2 * HEAD_DIM)
    cache_win_r = cache_win.reshape(depth, n_dec, cache_win.shape[2], KV_COLS)

    y_p = x_prompt.reshape(m_p, d)
    y_s = x_sample.reshape(n_dec, d)
    outs = [[] for _ in range(7)]
    win_keep = min(WINDOW, seq)
    for l in range(depth):
        rows = _Rows(mod, l, d, m_p, 1024 if seq % 1024 == 0 else 256, seq, n_dec)
        z, gt = _in_proj(y_p, rows, l, pre_mix, w_main, w_gate)
        a_n = _sgu(z, l, a_width, ln_g, ln_b, sgu_w, sgu_bt, norm_a)
        kv_cmp, kv_slc, kv_win, slc_b, win_b = _kv_rope(z, kv_block, tables_p, 256)
        kvc = _compress_prompt(z, batch, seq, cmp_block, cmp_pe[l], cmp_w1[l], cmp_w2[l])
        b_n = _attn_prompt(z, gt, kvc, slc_b, win_b, tables_p, l, norm_b, sel_p, exp_p, batch, seq, b_width, q_block)
        rows = _Rows(mod, l, d, m_p, 512 if seq % 512 == 0 else 256, seq, n_dec)
        y_p = _out_proj(a_n, b_n, y_p, rows, l, w_out_b, post_mix)
        y_p = _ffn(y_p, rows, l, pre_ffn, post_ffn, w_up_b, w_down_b)
        outs[0].append(kv_cmp.reshape(batch, seq, N_KV, 2, HEAD_DIM))
        outs[1].append(kv_slc.reshape(batch, seq, N_KV, 2, HEAD_DIM))
        outs[2].append(kv_win.reshape(batch, seq, N_KV, 2, HEAD_DIM)[:, seq - win_keep:])
        rows = _Rows(mod, l, d, n_dec, n_dec, None, 0)
        z, gt = _in_proj(y_s, rows, l, pre_mix, w_main, w_gate)
        kv_cmp, kv_slc, kv_win, _, _ = _kv_rope(z, kv_block, tables_s, n_dec)
        kvc = _compress_sample(page_table, cache_cmp_r, l, kv_cmp, cmp_pe[l], cmp_w1[l], cmp_w2[l])
        q3 = z[:, 2 * a_width:2 * a_width + b_width].reshape(n_dec, n_heads, HEAD_DIM)
        o_c, idx = _select_sample(q3, kvc, sel_s, past, hpg)
        a_n, b_n, v_n = _mix_sample(page_table, idx[:, :, :, 0], cache_slc_r, q3, o_c,
                                    gt[:, :n_gate].reshape(n_dec, n_heads, 3), kv_slc, kv_win, cache_win_r,
                                    rope_s, z.reshape(n_dec, 1, n_main), l, ln_g, ln_b, sgu_w0, sgu_b0,
                                    norm_a, norm_b3, past, hpg, a_width)
        y_s = _out_proj(a_n.reshape(n_dec, a_width), b_n.reshape(n_dec, b_width), y_s, rows, l, w_out_b, post_mix)
        y_s = _ffn(y_s, rows, l, pre_ffn, post_ffn, w_up_b, w_down_b)
        outs[3].append(kv_cmp.reshape(n_dec, 1, N_KV, 2, HEAD_DIM))
        outs[4].append(kv_slc.reshape(n_dec, 1, N_KV, 2, HEAD_DIM))
        outs[5].append(kv_win.reshape(n_dec, 1, N_KV, 2, HEAD_DIM))
        outs[6].append(v_n)
    return (y_p.reshape(batch, seq, d), y_s.reshape(n_dec, 1, d), *[jnp.stack(o) for o in outs])
```

```python
import functools

import jax
import jax.numpy as jnp
from jax import lax
from jax.experimental import pallas as pl
from jax.experimental.pallas import tpu as pltpu

F32 = jnp.float32
BF16 = jnp.bfloat16
I32 = jnp.int32

LANES = 128
HEAD_DIM = 128
N_KV = 2
PAGE_SIZE = 128
CHUNK = 128
CMP_BLOCK = 32
CMP_STRIDE = 16
SLC_BLOCK = 64
N_SEL = 16
WINDOW = 512
ROPE_DIM = HEAD_DIM // 4
ROPE_HALF = ROPE_DIM // 2
ROPE_THETA = 500000.0
EPS = 1e-6
FORCE = 1e4
SCALE = HEAD_DIM ** -0.5
EXP2_SCALE = SCALE * 1.4426950408889634
KV_COLS = N_KV * 2 * HEAD_DIM
CMP_STRIPS = CMP_BLOCK // CMP_STRIDE
STRIP_COLS = CMP_STRIDE * HEAD_DIM
CMP_PER_SLC = SLC_BLOCK // CMP_STRIDE
CMP_INSIDE = (SLC_BLOCK - CMP_BLOCK) // CMP_STRIDE + 1
VMEM_LIMIT = 56 * 2 ** 20


def _dot(a, b):
    return jnp.dot(a, b, preferred_element_type=F32)


def _dot_nt(a, b):
    return lax.dot_general(a, b, (((1,), (1,)), ((), ())), preferred_element_type=F32)


def _gelu(x):
    return 0.5 * x * (1.0 + jnp.tanh(0.7978845608028654 * (x + 0.044715 * (x * x * x))))


def _sigmoid(x):
    return 1.0 / (1.0 + jnp.exp(-x))


def _rms(x, g):
    return x * lax.rsqrt(jnp.mean(x * x, axis=-1, keepdims=True) + EPS) * g


def _softmax_neg_inf(s):
    m = jnp.max(s, axis=-1, keepdims=True)
    m = jnp.where(m > -jnp.inf, m, 0.0)
    e = jnp.exp(s - m)
    return e / jnp.maximum(jnp.sum(e, axis=-1, keepdims=True), 1e-30)


def _msoftmax(s, mask):
    return _softmax_neg_inf(jnp.where(mask, s, -jnp.inf))


def _attend(s, v):
    m = jnp.max(s, axis=-1, keepdims=True)
    m = jnp.where(m > -jnp.inf, m, 0.0)
    e = jnp.exp2((s - m) * EXP2_SCALE)
    return _dot(e.astype(BF16), v) / jnp.maximum(jnp.sum(e, axis=-1, keepdims=True), 1e-30)


def _rope(x, c, sa, sb):
    return x * c + pltpu.roll(x, LANES - ROPE_HALF, axis=1) * sa + pltpu.roll(x, ROPE_HALF, axis=1) * sb


def _split_dot(x, m):
    hi = x.astype(BF16)
    r1 = x - hi.astype(F32)
    mid = r1.astype(BF16)
    lo = (r1 - mid.astype(F32)).astype(BF16)
    return _dot(hi, m) + _dot(mid, m) + _dot(lo, m)


def _params(sem, vmem=None):
    return pltpu.CompilerParams(dimension_semantics=sem, vmem_limit_bytes=vmem)


def _ada_kernel(c_ref, w_ref, b_ref, o_ref):
    c = c_ref[...]
    o_ref[...] = _dot((c * _sigmoid(c)).astype(BF16), w_ref[...].astype(BF16)) + b_ref[...]


def _ada(c_all, w_ada, b_ada):
    depth, d, n = w_ada.shape
    rows = c_all.shape[0]
    tn = 1024
    return pl.pallas_call(
        _ada_kernel,
        grid=(depth, n // tn),
        in_specs=[pl.BlockSpec((rows, d), lambda l, j: (0, 0)),
                  pl.BlockSpec((None, d, tn), lambda l, j: (l, 0, j)),
                  pl.BlockSpec((None, 1, tn), lambda l, j: (l, 0, j))],
        out_specs=pl.BlockSpec((None, rows, tn), lambda l, j: (l, 0, j)),
        out_shape=jax.ShapeDtypeStruct((depth, rows, n), F32),
        compiler_params=_params(("arbitrary", "arbitrary"), VMEM_LIMIT),
        name="ada",
    )(c_all, w_ada, b_ada.reshape(depth, 1, n))


class _Rows:
    def __init__(self, mod, layer, d, m, tm, rows_per_batch, mod_row0):
        self.m, self.tm, self.d = m, tm, d
        depth, r, n = mod.shape
        if rows_per_batch is None:
            self.mod = mod
            self._spec = lambda k: pl.BlockSpec((None, tm, d), lambda i, *_: (layer, mod_row0 // tm, k))
        else:
            tiles = rows_per_batch // tm
            self.mod = mod.reshape(depth, r, 1, n)
            self._spec = lambda k: pl.BlockSpec((None, None, 1, d),
                                                lambda i, *_: (layer, mod_row0 + i // tiles, 0, k))

    def mod_spec(self, k):
        return self._spec(k)


def _vec_spec(layer, n):
    return pl.BlockSpec((None, 1, n), lambda *_: (layer, 0, 0))


def _in_kernel(x_ref, g_ref, sc_ref, sh_ref, w_ref, wg_ref, z_ref, gt_ref, h_ref):
    @pl.when(pl.program_id(1) == 0)
    def _():
        h = _rms(x_ref[...], g_ref[...]) * (1.0 + sc_ref[...]) + sh_ref[...]
        h_ref[...] = h.astype(h_ref.dtype)
        gt_ref[...] = _dot(h.astype(BF16), wg_ref[...])

    z_ref[...] = _dot(h_ref[...].astype(BF16), w_ref[...])


def _in_proj(x, rows, layer, norm_w, w_main, w_gate, n):
    m, tm, d = rows.m, rows.tm, rows.d
    tn = 768 if n % 768 == 0 else 512
    h_dtype = BF16 if tm % 16 == 0 else F32
    return pl.pallas_call(
        _in_kernel,
        grid=(m // tm, n // tn),
        in_specs=[pl.BlockSpec((tm, d), lambda i, j: (i, 0)),
                  _vec_spec(layer, d), rows.mod_spec(1), rows.mod_spec(0),
                  pl.BlockSpec((None, d, tn), lambda i, j: (layer, 0, j)),
                  pl.BlockSpec((None, d, LANES), lambda i, j: (layer, 0, 0))],
        out_specs=[pl.BlockSpec((tm, tn), lambda i, j: (i, j)),
                   pl.BlockSpec((tm, LANES), lambda i, j: (i, 0))],
        out_shape=[jax.ShapeDtypeStruct((m, n), F32), jax.ShapeDtypeStruct((m, LANES), F32)],
        scratch_shapes=[pltpu.VMEM((tm, d), h_dtype)],
        compiler_params=_params(("arbitrary", "arbitrary"), VMEM_LIMIT),
        name="in_proj",
    )(x, norm_w, rows.mod, rows.mod, w_main, w_gate)


def _layernorm(v, g, b):
    mu = jnp.mean(v, axis=-1, keepdims=True)
    vc = v - mu
    return vc * lax.rsqrt(jnp.mean(vc * vc, axis=-1, keepdims=True) + EPS) * g + b


def _sgu_kernel(u_ref, v_ref, lg_ref, lb_ref, w_ref, bt_ref, na_ref, o_ref, *, groups, chunks):
    row = lax.broadcasted_iota(I32, (CHUNK, CHUNK), 0)
    col = lax.broadcasted_iota(I32, (CHUNK, CHUNK), 1)
    ws = [jnp.where(row >= col, w_ref[g], 0.0).astype(BF16) for g in range(groups)]
    for c in range(chunks):
        rs = slice(c * CHUNK, (c + 1) * CHUNK)
        vn = _layernorm(_gelu(v_ref[rs, :]), lg_ref[...], lb_ref[...]).astype(BF16)
        u = _gelu(u_ref[rs, :])
        outs = []
        for g in range(groups):
            cs = slice(g * HEAD_DIM, (g + 1) * HEAD_DIM)
            outs.append(u[:, cs] * (_dot(ws[g], vn[:, cs]) + bt_ref[:, g:g + 1]))
        o_ref[rs, :] = _rms(jnp.concatenate(outs, axis=1), na_ref[...]).astype(BF16)


def _sgu(z, layer, a_width, ln_g, ln_b, sgu_w, sgu_bt, norm_a):
    m = z.shape[0]
    groups = a_width // HEAD_DIM
    chunks = 2
    tr = chunks * CHUNK
    return pl.pallas_call(
        functools.partial(_sgu_kernel, groups=groups, chunks=chunks),
        grid=(m // tr,),
        in_specs=[pl.BlockSpec((tr, a_width), lambda i: (i, 0)),
                  pl.BlockSpec((tr, a_width), lambda i: (i, 1)),
                  _vec_spec(layer, a_width), _vec_spec(layer, a_width),
                  pl.BlockSpec((None, groups, CHUNK, CHUNK), lambda i: (layer, 0, 0, 0)),
                  pl.BlockSpec((None, CHUNK, groups), lambda i: (layer, 0, 0)),
                  _vec_spec(layer, a_width)],
        out_specs=pl.BlockSpec((tr, a_width), lambda i: (i, 0)),
        out_shape=jax.ShapeDtypeStruct((m, a_width), BF16),
        compiler_params=_params(("arbitrary",), VMEM_LIMIT),
        name="sgu",
    )(z, z, ln_g, ln_b, sgu_w, sgu_bt, norm_a)


def _kv_kernel(kv_ref, rc_ref, rsa_ref, rsb_ref, cmp_ref, slc_ref, win_ref, slcb_ref, winb_ref):
    cmp_ref[...] = kv_ref[:, 0:KV_COLS]
    rc, rsa, rsb = rc_ref[...], rsa_ref[...], rsb_ref[...]
    for t, (o_ref, ob_ref) in enumerate(((slc_ref, slcb_ref), (win_ref, winb_ref))):
        for g in range(N_KV):
            k0 = (t + 1) * KV_COLS + g * 2 * HEAD_DIM
            k = _rope(kv_ref[:, k0:k0 + HEAD_DIM], rc, rsa, rsb)
            v = kv_ref[:, k0 + HEAD_DIM:k0 + 2 * HEAD_DIM]
            o0 = g * 2 * HEAD_DIM
            o_ref[:, o0:o0 + HEAD_DIM] = k
            o_ref[:, o0 + HEAD_DIM:o0 + 2 * HEAD_DIM] = v
            ob_ref[:, o0:o0 + HEAD_DIM] = k.astype(BF16)
            ob_ref[:, o0 + HEAD_DIM:o0 + 2 * HEAD_DIM] = v.astype(BF16)


def _kv_rope(z, kv_block, tables, tr):
    m = z.shape[0]
    t_rows = tables[0].shape[0]
    nt = t_rows // tr
    tspec = pl.BlockSpec((tr, LANES), lambda i: (i % nt, 0))
    ospec = pl.BlockSpec((tr, KV_COLS), lambda i: (i, 0))
    return pl.pallas_call(
        _kv_kernel,
        grid=(m // tr,),
        in_specs=[pl.BlockSpec((tr, 3 * KV_COLS), lambda i: (i, kv_block)), tspec, tspec, tspec],
        out_specs=[ospec] * 5,
        out_shape=[jax.ShapeDtypeStruct((m, KV_COLS), F32)] * 3 + [jax.ShapeDtypeStruct((m, KV_COLS), BF16)] * 2,
        compiler_params=_params(("arbitrary",), VMEM_LIMIT),
        name="kv_rope",
    )(z, *tables)


def _compress_strips(xf, pe_ref, w1_ref, sel):
    p = _dot((xf + pe_ref[sel, 0:1, :]).astype(BF16), w1_ref[sel, 0:STRIP_COLS, :])
    q = _dot((xf + pe_ref[sel, 1:2, :]).astype(BF16), w1_ref[sel, STRIP_COLS:2 * STRIP_COLS, :])
    return p, q


def _cmp_kernel(x_ref, pe_ref, w1_ref, w2_ref, o_ref, *, n):
    sel = pl.program_id(1) % 2
    xf = jnp.concatenate([x_ref[pl.ds(r, n, stride=CMP_STRIDE), :] for r in range(CMP_STRIDE)], axis=1)
    p, q = _compress_strips(xf, pe_ref, w1_ref, sel)
    hid = _gelu(p + pltpu.roll(q, n - 1, axis=0))
    o_ref[...] = _dot(hid.astype(BF16), w2_ref[sel]).astype(BF16)


def _compress_prompt(z, batch, seq, cmp_block, pe2, w1, w2):
    n = seq // CMP_STRIDE
    hidden = w1.shape[-1]
    heads = 2 * N_KV
    return pl.pallas_call(
        functools.partial(_cmp_kernel, n=n),
        grid=(batch, heads),
        in_specs=[pl.BlockSpec((seq, HEAD_DIM), lambda b, h: (b, cmp_block * heads + h)),
                  pl.BlockSpec((2, 2, STRIP_COLS), lambda b, h: (0, 0, 0)),
                  pl.BlockSpec((2, 2 * STRIP_COLS, hidden), lambda b, h: (0, 0, 0)),
                  pl.BlockSpec((2, hidden, HEAD_DIM), lambda b, h: (0, 0, 0))],
        out_specs=pl.BlockSpec((None, None, n, HEAD_DIM), lambda b, h: (b, h, 0, 0)),
        out_shape=jax.ShapeDtypeStruct((batch, heads, n, HEAD_DIM), BF16),
        compiler_params=_params(("arbitrary", "arbitrary"), VMEM_LIMIT),
        name="compress_prompt",
    )(z, pe2, w1, w2)


def _attn_kernel(q_ref, gt_ref, cmp_ref, slc_ref, win_ref, rc_ref, rsa_ref, rsb_ref, nb_ref, sel_ref, exp_ref,
                 o_ref, acc_ref, *, tq, seq, hpg, n_cmp, kc):
    q0 = pl.program_id(1) * tq
    n_chunks = (q0 + tq + kc - 1) // kc
    kiota = lax.broadcasted_iota(I32, (1, kc), 1)
    pos = q0 + lax.broadcasted_iota(I32, (tq, 1), 0)
    posh = jnp.concatenate([pos] * hpg, axis=0)
    rc = jnp.concatenate([rc_ref[...]] * hpg, axis=0)
    rsa = jnp.concatenate([rsa_ref[...]] * hpg, axis=0)
    rsb = jnp.concatenate([rsb_ref[...]] * hpg, axis=0)
    sig = _sigmoid(gt_ref[...])
    clane = lax.broadcasted_iota(I32, (1, n_cmp), 1)
    n_blk = seq // SLC_BLOCK
    n_sel = min(N_SEL, n_blk)
    nb8 = -(-n_blk // 8) * 8
    brow = lax.broadcasted_iota(I32, (nb8, 1), 0)
    cur_t = (q0 + lax.broadcasted_iota(I32, (1, tq), 1)) // SLC_BLOCK
    forced_t = (brow == 0) | (brow == cur_t) | (brow == cur_t - 1)
    brow_q = lax.broadcasted_iota(I32, (nb8, tq), 0)
    wk = WINDOW + tq
    wstart = pl.multiple_of(jnp.maximum(q0 - WINDOW, 0), LANES)
    wpos = wstart + lax.broadcasted_iota(I32, (1, wk), 1)
    wdiff = posh - wpos
    wmask = (wdiff >= 0) & (wdiff < WINDOW)
    cmask = (clane * CMP_STRIDE + (CMP_BLOCK - 1)) <= posh
    for g in range(N_KV):
        heads = [g * hpg + h for h in range(hpg)]
        qc = jnp.concatenate([q_ref[:, hd * HEAD_DIM:(hd + 1) * HEAD_DIM] for hd in heads], axis=0)
        qrb = _rope(qc, rc, rsa, rsb).astype(BF16)
        k0 = g * 2 * HEAD_DIM
        v0 = k0 + HEAD_DIM
        p = _msoftmax(_dot_nt(qc.astype(BF16), cmp_ref[2 * g]) * SCALE, cmask)
        o_c = _dot(p.astype(BF16), cmp_ref[2 * g + 1])
        imp = p[0:tq]
        for h in range(1, hpg):
            imp = imp + p[h * tq:(h + 1) * tq]
        impb = _split_dot(imp, sel_ref[...])
        score = jnp.where(brow <= cur_t, jnp.where(forced_t, FORCE, impb.T[0:nb8]), -FORCE)
        rank = jnp.zeros((nb8, tq), F32)
        for i in range(n_blk):
            ci = score[i:i + 1, :]
            tie = jnp.where(brow_q > i, 1.0, 0.0)
            rank = rank + jnp.where(ci > score, 1.0, jnp.where(ci == score, tie, 0.0))
        chosen_t = jnp.where((rank < float(n_sel)) & (brow <= cur_t), 1.0, 0.0)
        chosen_b = jnp.concatenate([chosen_t, jnp.zeros((LANES - nb8, tq), F32)], axis=0).T.astype(BF16)

        def chunk(c, carry):
            m, l, acc = carry
            r0 = pl.multiple_of(c * kc, kc)
            selk = _dot(chosen_b, exp_ref[c])
            open_key = (selk > 0.5) & ((r0 + kiota) <= pos)
            bias = jnp.where(open_key, 0.0, -jnp.inf)
            s = _dot_nt(qrb, slc_ref[pl.ds(r0, kc), k0:k0 + HEAD_DIM]) + jnp.concatenate([bias] * hpg, axis=0)
            m_new = jnp.maximum(m, jnp.max(s, axis=-1, keepdims=True))
            m_ref = jnp.where(m_new > -jnp.inf, m_new, 0.0)
            alpha = jnp.exp2((m - m_ref) * EXP2_SCALE)
            e = jnp.exp2((s - m_ref) * EXP2_SCALE)
            l = alpha * l + jnp.sum(e, axis=-1, keepdims=True)
            acc = alpha * acc + _dot(e.astype(BF16), slc_ref[pl.ds(r0, kc), v0:v0 + HEAD_DIM])
            return m_new, l, acc

        rows = hpg * tq
        init = (jnp.full((rows, 1), -jnp.inf, F32), jnp.zeros((rows, 1), F32), jnp.zeros((rows, HEAD_DIM), F32))
        _, l, acc = lax.fori_loop(0, n_chunks, chunk, init)
        o_s = acc / jnp.maximum(l, 1e-30)
        s = jnp.where(wmask, _dot_nt(qrb, win_ref[pl.ds(wstart, wk), k0:k0 + HEAD_DIM]), -jnp.inf)
        o_w = _attend(s, win_ref[pl.ds(wstart, wk), v0:v0 + HEAD_DIM])
        for h, hd in enumerate(heads):
            rs = slice(h * tq, (h + 1) * tq)
            acc_ref[:, hd * HEAD_DIM:(hd + 1) * HEAD_DIM] = (
                sig[:, 3 * hd:3 * hd + 1] * o_c[rs] + sig[:, 3 * hd + 1:3 * hd + 2] * o_s[rs]
                + sig[:, 3 * hd + 2:3 * hd + 3] * o_w[rs])
    o_ref[...] = _rms(acc_ref[...], nb_ref[...]).astype(BF16)


def _attn_prompt(z, gt, kvc, slc_b, win_b, tables, layer, norm_b, sel_m, exp_m, batch, seq, b_width, q_block):
    tq = 128
    nq = seq // tq
    hpg = b_width // HEAD_DIM // N_KV
    n_cmp = kvc.shape[2]
    kc = exp_m.shape[2]
    tspec = pl.BlockSpec((tq, LANES), lambda b, i: (i, 0))
    kvspec = pl.BlockSpec((seq, KV_COLS), lambda b, i: (b, 0))
    return pl.pallas_call(
        functools.partial(_attn_kernel, tq=tq, seq=seq, hpg=hpg, n_cmp=n_cmp, kc=kc),
        grid=(batch, nq),
        in_specs=[pl.BlockSpec((tq, b_width), lambda b, i: (b * nq + i, q_block)),
                  pl.BlockSpec((tq, LANES), lambda b, i: (b * nq + i, 0)),
                  pl.BlockSpec((None, 2 * N_KV, n_cmp, HEAD_DIM), lambda b, i: (b, 0, 0, 0)),
                  kvspec, kvspec, tspec, tspec, tspec,
                  pl.BlockSpec((None, 1, b_width), lambda b, i: (layer, 0, 0)),
                  pl.BlockSpec(sel_m.shape, lambda b, i: (0, 0)),
                  pl.BlockSpec(exp_m.shape, lambda b, i: (0, 0, 0))],
        out_specs=pl.BlockSpec((tq, b_width), lambda b, i: (b * nq + i, 0)),
        out_shape=jax.ShapeDtypeStruct((batch * seq, b_width), BF16),
        scratch_shapes=[pltpu.VMEM((tq, b_width), F32)],
        compiler_params=_params(("arbitrary", "arbitrary"), VMEM_LIMIT),
        name="attn_prompt",
    )(z, gt, kvc, slc_b, win_b, *tables, norm_b, sel_m, exp_m)


def _out_kernel(a_ref, b_ref, w_ref, x_ref, gate_ref, g_ref, o_ref, *, a_width):
    m = _dot(a_ref[...], w_ref[0:a_width, :]) + _dot(b_ref[...], w_ref[a_width:, :])
    o_ref[...] = x_ref[...] + gate_ref[...] * _rms(m, g_ref[...])


def _out_proj(a_n, b_n, x, rows, layer, w_out, norm_w):
    m, tm, d = rows.m, rows.tm, rows.d
    a_width = a_n.shape[1]
    b_width = b_n.shape[1]
    return pl.pallas_call(
        functools.partial(_out_kernel, a_width=a_width),
        grid=(m // tm,),
        in_specs=[pl.BlockSpec((tm, a_width), lambda i: (i, 0)),
                  pl.BlockSpec((tm, b_width), lambda i: (i, 0)),
                  pl.BlockSpec((None, a_width + b_width, d), lambda i: (layer, 0, 0)),
                  pl.BlockSpec((tm, d), lambda i: (i, 0)),
                  rows.mod_spec(2), _vec_spec(layer, d)],
        out_specs=pl.BlockSpec((tm, d), lambda i: (i, 0)),
        out_shape=jax.ShapeDtypeStruct((m, d), F32),
        compiler_params=_params(("arbitrary",), VMEM_LIMIT),
        name="out_proj",
    )(a_n, b_n, w_out, x, rows.mod, norm_w)


def _ffn_kernel(x_ref, g1_ref, sc_ref, sh_ref, wu_ref, wd_ref, gate_ref, g2_ref, o_ref, h_ref):
    k = pl.program_id(1)

    @pl.when(k == 0)
    def _():
        h = _rms(x_ref[...], g1_ref[...]) * (1.0 + sc_ref[...]) + sh_ref[...]
        h_ref[...] = h.astype(h_ref.dtype)
        o_ref[...] = jnp.zeros_like(o_ref)

    up = jnp.maximum(_dot(h_ref[...].astype(BF16), wu_ref[...]), 0.0)
    o_ref[...] += _dot((up * up).astype(BF16), wd_ref[...])

    @pl.when(k == pl.num_programs(1) - 1)
    def _():
        o_ref[...] = x_ref[...] + gate_ref[...] * _rms(o_ref[...], g2_ref[...])


def _ffn(x, rows, layer, norm_pre, norm_post, w_up, w_down):
    m, tm, d = rows.m, rows.tm, rows.d
    d_ff = w_up.shape[-1]
    tf = 512
    h_dtype = BF16 if tm % 16 == 0 else F32
    x_mode = dict(pipeline_mode=pl.Buffered(1)) if tm * d * 4 >= 8 * 2 ** 20 else {}
    return pl.pallas_call(
        _ffn_kernel,
        grid=(m // tm, d_ff // tf),
        in_specs=[pl.BlockSpec((tm, d), lambda i, k: (i, 0), **x_mode),
                  _vec_spec(layer, d), rows.mod_spec(4), rows.mod_spec(3),
                  pl.BlockSpec((None, d, tf), lambda i, k: (layer, 0, k)),
                  pl.BlockSpec((None, tf, d), lambda i, k: (layer, k, 0)),
                  rows.mod_spec(5), _vec_spec(layer, d)],
        out_specs=pl.BlockSpec((tm, d), lambda i, k: (i, 0)),
        out_shape=jax.ShapeDtypeStruct((m, d), F32),
        scratch_shapes=[pltpu.VMEM((tm, d), h_dtype)],
        compiler_params=_params(("arbitrary", "arbitrary"), VMEM_LIMIT),
        name="ffn",
    )(x, norm_pre, rows.mod, rows.mod, w_up, w_down, rows.mod, norm_post)


def _scmp_kernel(pt_ref, cache_ref, new_ref, pe_ref, w1_ref, w2_ref, o_ref, buf, sem, p_s, q_s,
                 *, layer, pages, n_chunk, n_batch):
    b = pl.program_id(0)
    c = pl.program_id(1)
    step = b * n_chunk + c
    slot = step % 2
    strips = pages * (PAGE_SIZE // CMP_STRIDE)
    n_past = n_chunk * strips

    def copies(bb, cc, sl):
        return [pltpu.make_async_copy(cache_ref.at[layer, pt_ref[bb, cc * pages + p], :, gs // 2, gs % 2, :],
                                      buf.at[sl, gs, p], sem.at[sl])
                for p in range(pages) for gs in range(2 * N_KV)]

    @pl.when(step == 0)
    def _():
        for cp in copies(0, 0, 0):
            cp.start()

    @pl.when(step + 1 < n_batch * n_chunk)
    def _():
        nxt = step + 1
        for cp in copies(nxt // n_chunk, nxt % n_chunk, 1 - slot):
            cp.start()

    for cp in copies(b, c, slot):
        cp.wait()

    cur = buf.at[slot]
    row0 = pl.multiple_of(c * strips, strips)
    for gs in range(2 * N_KV):
        sel = gs % 2
        xf = jnp.concatenate(
            [cur[gs, :, pl.ds(r, PAGE_SIZE // CMP_STRIDE, stride=CMP_STRIDE), :].reshape(strips, HEAD_DIM)
             for r in range(CMP_STRIDE)], axis=1)
        p, q = _compress_strips(xf, pe_ref, w1_ref, sel)
        p_s[gs, pl.ds(row0, strips), :] = p
        q_s[gs, pl.ds(row0, strips), :] = q

    @pl.when(c == n_chunk - 1)
    def _():
        for gs in range(2 * N_KV):
            sel = gs % 2
            new = new_ref[:, gs * HEAD_DIM:(gs + 1) * HEAD_DIM]
            xn = jnp.concatenate([new, jnp.zeros((1, STRIP_COLS - HEAD_DIM), F32)], axis=1) + pe_ref[sel, 1:2, :]
            xn = jnp.broadcast_to(xn, (8, STRIP_COLS)).astype(BF16)
            q_s[gs, n_past:n_past + 8, :] = _dot(xn, w1_ref[sel, STRIP_COLS:2 * STRIP_COLS, :])
            hid = _gelu(p_s[gs] + q_s[gs, 1:n_past + 1, :])
            o_ref[gs] = _dot(hid.astype(BF16), w2_ref[sel]).astype(BF16)


def _compress_sample(page_table, cache, layer, new_cmp, pe2, w1, w2):
    n_batch, n_pages = page_table.shape
    pages = 32 if n_pages % 32 == 0 else n_pages
    n_chunk = n_pages // pages
    n_past = n_pages * (PAGE_SIZE // CMP_STRIDE)
    hidden = w1.shape[-1]
    grid_spec = pltpu.PrefetchScalarGridSpec(
        num_scalar_prefetch=1,
        grid=(n_batch, n_chunk),
        in_specs=[pl.BlockSpec(memory_space=pl.ANY),
                  pl.BlockSpec((None, 1, KV_COLS), lambda b, c, pt: (b, 0, 0)),
                  pl.BlockSpec((2, 2, STRIP_COLS), lambda b, c, pt: (0, 0, 0)),
                  pl.BlockSpec((2, 2 * STRIP_COLS, hidden), lambda b, c, pt: (0, 0, 0)),
                  pl.BlockSpec((2, hidden, HEAD_DIM), lambda b, c, pt: (0, 0, 0))],
        out_specs=pl.BlockSpec((None, 2 * N_KV, n_past, HEAD_DIM), lambda b, c, pt: (b, 0, 0, 0)),
        scratch_shapes=[pltpu.VMEM((2, 2 * N_KV, pages, PAGE_SIZE, HEAD_DIM), F32),
                        pltpu.SemaphoreType.DMA((2,)),
                        pltpu.VMEM((2 * N_KV, n_past, hidden), F32),
                        pltpu.VMEM((2 * N_KV, n_past + 8, hidden), F32)])
    return pl.pallas_call(
        functools.partial(_scmp_kernel, layer=layer, pages=pages, n_chunk=n_chunk, n_batch=n_batch),
        grid_spec=grid_spec,
        out_shape=jax.ShapeDtypeStruct((n_batch, 2 * N_KV, n_past, HEAD_DIM), BF16),
        compiler_params=_params(("arbitrary", "arbitrary"), VMEM_LIMIT),
        name="compress_sample",
    )(page_table, cache, new_cmp.reshape(n_batch, 1, KV_COLS), pe2, w1, w2)


def _ssel_kernel(q_ref, kvc_ref, sel_ref, oc_ref, idx_ref, *, pos, hpg, n_cmp, n_lane):
    qb = q_ref[...].astype(BF16)
    n_heads = qb.shape[0]
    head_row = lax.broadcasted_iota(I32, (n_heads, 1), 0)
    clane = lax.broadcasted_iota(I32, (1, n_cmp), 1)
    cmask = (clane * CMP_STRIDE + (CMP_BLOCK - 1)) <= pos
    lane = lax.broadcasted_iota(I32, (1, n_lane), 1)
    cur = pos // SLC_BLOCK
    forced = (lane == 0) | (lane == cur) | (lane == cur - 1)
    ri = lax.broadcasted_iota(I32, (n_lane, n_lane), 0)
    ci = lax.broadcasted_iota(I32, (n_lane, n_lane), 1)
    slot_id = lax.broadcasted_iota(I32, (N_SEL, 1), 0)
    o_c = jnp.zeros((n_heads, HEAD_DIM), F32)
    for g in range(N_KV):
        mine = (head_row // hpg) == g
        p = _msoftmax(_dot_nt(qb, kvc_ref[2 * g]) * SCALE, cmask)
        o_c = jnp.where(mine, _dot(p.astype(BF16), kvc_ref[2 * g + 1]), o_c)
        imp = jnp.sum(jnp.where(mine, p, 0.0), axis=0, keepdims=True)
        impb = _split_dot(jnp.broadcast_to(imp, (8, n_cmp)), sel_ref[...])[0:1]
        score = jnp.where(lane <= cur, jnp.where(forced, FORCE, impb), -FORCE)
        srow = jnp.broadcast_to(score, (n_lane, n_lane))
        scol = srow.T
        ahead = (scol > srow) | ((scol == srow) & (ri < ci))
        rank = jnp.sum(jnp.where(ahead, 1.0, 0.0), axis=0, keepdims=True)
        chosen = jnp.where((rank < float(N_SEL)) & (lane <= cur), 1.0, 0.0)
        ccol = jnp.broadcast_to(chosen, (n_lane, n_lane)).T
        before = jnp.sum(jnp.where(ri < ci, ccol, 0.0), axis=0, keepdims=True)
        hit = (chosen > 0.5) & (before == slot_id.astype(F32))
        idx = jnp.sum(jnp.where(hit, lane.astype(F32), 0.0), axis=1, keepdims=True)
        idx_ref[g] = jnp.broadcast_to(idx, (N_SEL, LANES)).astype(I32)
    oc_ref[...] = o_c


def _select_sample(q3, kvc, sel_m, pos, hpg):
    n_batch, n_heads, _ = q3.shape
    n_cmp = kvc.shape[2]
    n_lane = sel_m.shape[1]
    return pl.pallas_call(
        functools.partial(_ssel_kernel, pos=pos, hpg=hpg, n_cmp=n_cmp, n_lane=n_lane),
        grid=(n_batch,),
        in_specs=[pl.BlockSpec((None, n_heads, HEAD_DIM), lambda b: (b, 0, 0)),
                  pl.BlockSpec((None, 2 * N_KV, n_cmp, HEAD_DIM), lambda b: (b, 0, 0, 0)),
                  pl.BlockSpec(sel_m.shape, lambda b: (0, 0))],
        out_specs=[pl.BlockSpec((None, n_heads, HEAD_DIM), lambda b: (b, 0, 0)),
                   pl.BlockSpec((None, N_KV, N_SEL, LANES), lambda b: (b, 0, 0, 0))],
        out_shape=[jax.ShapeDtypeStruct((n_batch, n_heads, HEAD_DIM), F32),
                   jax.ShapeDtypeStruct((n_batch, N_KV, N_SEL, LANES), I32)],
        compiler_params=_params(("arbitrary",), VMEM_LIMIT),
        name="select_sample",
    )(q3, kvc, sel_m)


def _smix_kernel(pt_ref, idx_ref, cache_ref, wcache_ref, q_ref, oc_ref, gt_ref, slc_ref, win_ref, rope_ref,
                 u_ref, v_ref, lg_ref, lb_ref, w0_ref, b0_ref, na_ref, nb_ref,
                 a_ref, b_ref, vn_ref, buf, wbuf, sem, *, layer, pos, hpg, nb_past):
    b = pl.program_id(0)
    halves = PAGE_SIZE // SLC_BLOCK

    def copies():
        out = []
        for g in range(N_KV):
            for k in range(N_SEL):
                jp = jnp.minimum(idx_ref[b, g, k], nb_past - 1)
                phys = pt_ref[b, jp // halves]
                rows = pl.ds(pl.multiple_of((jp % halves) * SLC_BLOCK, SLC_BLOCK), SLC_BLOCK)
                for t in range(2):
                    out.append(pltpu.make_async_copy(cache_ref.at[layer, phys, rows, g, t, :], buf.at[t, g, k],
                                                     sem.at[0]))
            for t in range(2):
                out.append(pltpu.make_async_copy(wcache_ref.at[layer, b, :, g, t, :], wbuf.at[g, t], sem.at[1]))
        return out

    for cp in copies():
        cp.start()

    vn = _layernorm(_gelu(v_ref[...]), lg_ref[...], lb_ref[...])
    vn_ref[...] = vn
    a_out = _gelu(u_ref[...]) * (w0_ref[...] * vn + b0_ref[...])
    a_ref[...] = _rms(a_out, na_ref[...]).astype(BF16)

    q = q_ref[...]
    n_heads = q.shape[0]
    head_row = lax.broadcasted_iota(I32, (n_heads, 1), 0)
    qrb = _rope(q, rope_ref[0:1, :], rope_ref[1:2, :], rope_ref[2:3, :]).astype(BF16)
    sig = _sigmoid(gt_ref[...])
    n_keys = N_SEL * SLC_BLOCK
    klane = lax.broadcasted_iota(I32, (1, n_keys), 1)
    wb = wbuf.shape[2]
    wdiff = pos - (pos - wb + lax.broadcasted_iota(I32, (1, wb), 1))
    wmask = (wdiff >= 0) & (wdiff < WINDOW)

    for cp in copies():
        cp.wait()

    o_s = jnp.zeros((n_heads, HEAD_DIM), F32)
    o_w = jnp.zeros((n_heads, HEAD_DIM), F32)
    for g in range(N_KV):
        mine = (head_row // hpg) == g
        k0 = g * 2 * HEAD_DIM
        v0 = k0 + HEAD_DIM
        kmask = jnp.zeros((1, n_keys), F32)
        has_new = jnp.zeros((1, 1), F32)
        for k in range(N_SEL):
            blk = idx_ref[b, g, k]
            in_k = (klane // SLC_BLOCK) == k
            kmask = jnp.where(in_k & (blk < nb_past), 1.0, kmask)
            has_new = jnp.where(blk >= nb_past, 1.0, has_new)
        ks = buf[0, g].reshape(n_keys, HEAD_DIM).astype(BF16)
        vs = buf[1, g].reshape(n_keys, HEAD_DIM).astype(BF16)
        s = jnp.where(kmask > 0.5, _dot_nt(qrb, ks) * SCALE, -jnp.inf)
        k_new = slc_ref[:, k0:k0 + HEAD_DIM].astype(BF16)
        v_new = slc_ref[:, v0:v0 + HEAD_DIM].astype(BF16)
        s_new = jnp.sum(qrb.astype(F32) * k_new.astype(F32), axis=-1, keepdims=True) * SCALE
        s_new = jnp.where(has_new > 0.5, s_new, -jnp.inf)
        m = jnp.maximum(jnp.max(s, axis=-1, keepdims=True), s_new)
        m = jnp.where(m > -jnp.inf, m, 0.0)
        e = jnp.exp(s - m)
        e_new = jnp.exp(s_new - m)
        den = jnp.maximum(jnp.sum(e, axis=-1, keepdims=True) + e_new, 1e-30)
        og = _dot((e / den).astype(BF16), vs) + (e_new / den).astype(BF16).astype(F32) * v_new.astype(F32)
        o_s = jnp.where(mine, og, o_s)
        s = jnp.where(wmask, _dot_nt(qrb, wbuf[g, 0].astype(BF16)) * SCALE, -jnp.inf)
        k_new = win_ref[:, k0:k0 + HEAD_DIM].astype(BF16)
        v_new = win_ref[:, v0:v0 + HEAD_DIM].astype(BF16)
        s_new = jnp.sum(qrb.astype(F32) * k_new.astype(F32), axis=-1, keepdims=True) * SCALE
        m = jnp.maximum(jnp.max(s, axis=-1, keepdims=True), s_new)
        e = jnp.exp(s - m)
        e_new = jnp.exp(s_new - m)
        den = jnp.maximum(jnp.sum(e, axis=-1, keepdims=True) + e_new, 1e-30)
        og = (_dot((e / den).astype(BF16), wbuf[g, 1].astype(BF16))
              + (e_new / den).astype(BF16).astype(F32) * v_new.astype(F32))
        o_w = jnp.where(mine, og, o_w)
    b_out = sig[:, 0:1] * oc_ref[...] + sig[:, 1:2] * o_s + sig[:, 2:3] * o_w
    ms = jnp.sum(jnp.sum(b_out * b_out, axis=-1, keepdims=True), axis=0, keepdims=True) / float(n_heads * HEAD_DIM)
    b_ref[...] = (b_out * lax.rsqrt(ms + EPS) * nb_ref[...]).astype(BF16)


def _mix_sample(page_table, idx, cache_slc, cache_win, q3, o_c, gt3, slc_new, win_new, rope_s, z3, layer,
                ln_g, ln_b, w0, b0, norm_a, norm_b3, pos, hpg, a_width):
    n_batch, n_heads, _ = q3.shape
    wb = cache_win.shape[2]
    nb_past = page_table.shape[1] * (PAGE_SIZE // SLC_BLOCK)
    head_spec = pl.BlockSpec((None, n_heads, HEAD_DIM), lambda b, *_: (b, 0, 0))
    new_spec = pl.BlockSpec((None, 1, KV_COLS), lambda b, *_: (b, 0, 0))
    row_spec = lambda blk: pl.BlockSpec((None, 1, a_width), lambda b, *_: (b, 0, blk))
    vec = lambda: pl.BlockSpec((None, 1, a_width), lambda b, *_: (layer, 0, 0))
    grid_spec = pltpu.PrefetchScalarGridSpec(
        num_scalar_prefetch=2,
        grid=(n_batch,),
        in_specs=[pl.BlockSpec(memory_space=pl.ANY), pl.BlockSpec(memory_space=pl.ANY), head_spec, head_spec,
                  pl.BlockSpec((None, n_heads, 3), lambda b, *_: (b, 0, 0)),
                  new_spec, new_spec,
                  pl.BlockSpec(rope_s.shape, lambda b, *_: (0, 0)),
                  row_spec(0), row_spec(1), vec(), vec(), vec(), vec(), vec(),
                  pl.BlockSpec((None, n_heads, HEAD_DIM), lambda b, *_: (layer, 0, 0))],
        out_specs=[pl.BlockSpec((None, 1, a_width), lambda b, *_: (b, 0, 0)),
                   head_spec,
                   pl.BlockSpec((None, 1, a_width), lambda b, *_: (b, 0, 0))],
        scratch_shapes=[pltpu.VMEM((2, N_KV, N_SEL, SLC_BLOCK, HEAD_DIM), F32),
                        pltpu.VMEM((N_KV, 2, wb, HEAD_DIM), F32),
                        pltpu.SemaphoreType.DMA((2,))])
    return pl.pallas_call(
        functools.partial(_smix_kernel, layer=layer, pos=pos, hpg=hpg, nb_past=nb_past),
        grid_spec=grid_spec,
        out_shape=[jax.ShapeDtypeStruct((n_batch, 1, a_width), BF16),
                   jax.ShapeDtypeStruct((n_batch, n_heads, HEAD_DIM), BF16),
                   jax.ShapeDtypeStruct((n_batch, 1, a_width), F32)],
        compiler_params=_params(("arbitrary",), VMEM_LIMIT),
        name="mix_sample",
    )(page_table, idx, cache_slc, cache_win, q3, o_c, gt3, slc_new.reshape(n_batch, 1, KV_COLS),
      win_new.reshape(n_batch, 1, KV_COLS), rope_s, z3, z3,
      ln_g, ln_b, w0, b0, norm_a, norm_b3)


def _rope_tables(pos):
    inv = ROPE_THETA ** (-jnp.arange(ROPE_HALF, dtype=F32) / ROPE_HALF)
    ang = pos.astype(F32)[:, None] * inv[None, :]
    cos, sin = jnp.cos(ang), jnp.sin(ang)
    n = pos.shape[0]
    zeros = lambda w: jnp.zeros((n, w), F32)
    c = jnp.concatenate([cos, cos, jnp.ones((n, HEAD_DIM - ROPE_DIM), F32)], axis=1)
    sa = jnp.concatenate([-sin, zeros(HEAD_DIM - ROPE_HALF)], axis=1)
    sb = jnp.concatenate([zeros(ROPE_HALF), sin, zeros(HEAD_DIM - ROPE_DIM)], axis=1)
    return c, sa, sb


def _block_sum_matrix(n_cmp, n_lane):
    i = jnp.arange(n_cmp)[:, None]
    j = jnp.arange(n_lane)[None, :]
    return ((i // CMP_PER_SLC == j) & (i % CMP_PER_SLC < CMP_INSIDE)).astype(BF16)


def kernel(x_prompt, x_sample, cache_cmp, cache_slc, cache_win, page_table, c_prompt, c_sample, w_ada, b_ada, norm_pre_mix, norm_post_mix, norm_pre_ffn, norm_post_ffn, w_in, ln_v_g, ln_v_b, sgu_w, sgu_b, cmp_pe_k, cmp_pe_v, cmp_w1_k, cmp_w2_k, cmp_w1_v, cmp_w2_v, out_norm_a, out_norm_b, w_out, w_up, w_down):
    batch, seq, d = x_prompt.shape
    n_dec, dec_seq, _ = x_sample.shape
    depth = w_ada.shape[0]
    a_width = d // 2
    b_width = d - a_width
    groups = a_width // HEAD_DIM
    n_heads = b_width // HEAD_DIM
    hpg = n_heads // N_KV
    n_main = 2 * a_width + b_width + 3 * KV_COLS
    n_gate = 3 * n_heads
    n_pages = page_table.shape[1]
    past = n_pages * PAGE_SIZE
    assert dec_seq == 1 and n_dec % 8 == 0 and a_width == b_width and n_gate <= LANES
    assert (2 * a_width + b_width) % (3 * KV_COLS) == 0 and seq % 1024 == 0
    assert past % SLC_BLOCK == 0 and past // SLC_BLOCK + 1 > N_SEL and cache_win.shape[2] <= past
    q_block = 2 * a_width // b_width
    kv_block = (2 * a_width + b_width) // (3 * KV_COLS)
    cmp_block = (2 * a_width + b_width) // KV_COLS
    m_p = batch * seq

    vec3 = lambda a: a.reshape(depth, 1, a.shape[-1])
    w_main = w_in.astype(BF16)
    w_gate = jnp.pad(w_in[:, :, n_main:], ((0, 0), (0, 0), (0, LANES - n_gate))).astype(BF16)
    w_out_b, w_up_b, w_down_b = w_out.astype(BF16), w_up.astype(BF16), w_down.astype(BF16)
    cmp_w1 = jnp.stack([cmp_w1_k, cmp_w1_v], axis=1).astype(BF16)
    cmp_w2 = jnp.stack([cmp_w2_k, cmp_w2_v], axis=1).astype(BF16)
    cmp_pe = jnp.stack([cmp_pe_k, cmp_pe_v], axis=1).reshape(depth, 2, CMP_STRIPS, STRIP_COLS)
    pre_mix, post_mix, pre_ffn, post_ffn = map(vec3, (norm_pre_mix, norm_post_mix, norm_pre_ffn, norm_post_ffn))
    ln_g, ln_b, norm_a, norm_b = map(vec3, (ln_v_g, ln_v_b, out_norm_a, out_norm_b))
    norm_b3 = out_norm_b.reshape(depth, n_heads, HEAD_DIM)
    sgu_bt = jnp.swapaxes(sgu_b, 1, 2)
    sgu_w0 = jnp.repeat(sgu_w[:, :, 0, 0], HEAD_DIM, axis=1).reshape(depth, 1, a_width)
    sgu_b0 = jnp.repeat(sgu_b[:, :, 0], HEAD_DIM, axis=1).reshape(depth, 1, a_width)

    mod_rows = -(-(n_dec + batch) // 8) * 8
    c_all = jnp.concatenate([c_sample, c_prompt, jnp.zeros((mod_rows - n_dec - batch, d), F32)], axis=0)
    mod = _ada(c_all, w_ada, b_ada)

    tables_p = _rope_tables(jnp.arange(seq))
    tables_s = _rope_tables(jnp.full((n_dec,), past))
    rope_s = jnp.concatenate([t[0:1] for t in tables_s] + [jnp.zeros((5, LANES), F32)], axis=0)
    n_cmp_p = seq // CMP_STRIDE
    sel_p = _block_sum_matrix(n_cmp_p, LANES)
    key_chunk = 512
    exp_p = (jnp.arange(LANES)[None, :, None]
             == (jnp.arange(seq) // SLC_BLOCK).reshape(seq // key_chunk, 1, key_chunk)).astype(BF16)
    n_cmp_s = past // CMP_STRIDE
    nb_lane = -(-(past // SLC_BLOCK + 1) // LANES) * LANES
    sel_s = _block_sum_matrix(n_cmp_s, nb_lane)

    y_p = x_prompt.reshape(m_p, d)
    y_s = x_sample.reshape(n_dec, d)
    outs = [[] for _ in range(7)]
    win_keep = min(WINDOW, seq)
    for l in range(depth):
        rows = _Rows(mod, l, d, m_p, 1024, seq, n_dec)
        z, gt = _in_proj(y_p, rows, l, pre_mix, w_main, w_gate, n_main)
        a_n = _sgu(z, l, a_width, ln_g, ln_b, sgu_w, sgu_bt, norm_a)
        kv_cmp, kv_slc, kv_win, slc_b, win_b = _kv_rope(z, kv_block, tables_p, 256)
        kvc = _compress_prompt(z, batch, seq, cmp_block, cmp_pe[l], cmp_w1[l], cmp_w2[l])
        b_n = _attn_prompt(z, gt, kvc, slc_b, win_b, tables_p, l, norm_b, sel_p, exp_p, batch, seq, b_width, q_block)
        y_p = _out_proj(a_n, b_n, y_p, _Rows(mod, l, d, m_p, 512, seq, n_dec), l, w_out_b, post_mix)
        y_p = _ffn(y_p, rows, l, pre_ffn, post_ffn, w_up_b, w_down_b)
        outs[0].append(kv_cmp.reshape(batch, seq, N_KV, 2, HEAD_DIM))
        outs[1].append(kv_slc.reshape(batch, seq, N_KV, 2, HEAD_DIM))
        outs[2].append(kv_win.reshape(batch, seq, N_KV, 2, HEAD_DIM)[:, seq - win_keep:])
        rows = _Rows(mod, l, d, n_dec, n_dec, None, 0)
        z, gt = _in_proj(y_s, rows, l, pre_mix, w_main, w_gate, n_main)
        kv_cmp, kv_slc, kv_win, _, _ = _kv_rope(z, kv_block, tables_s, n_dec)
        kvc = _compress_sample(page_table, cache_cmp, l, kv_cmp, cmp_pe[l], cmp_w1[l], cmp_w2[l])
        q3 = z[:, 2 * a_width:2 * a_width + b_width].reshape(n_dec, n_heads, HEAD_DIM)
        o_c, idx = _select_sample(q3, kvc, sel_s, past, hpg)
        a_n, b_n, v_n = _mix_sample(page_table, idx[:, :, :, 0], cache_slc, cache_win, q3, o_c,
                                    gt[:, :n_gate].reshape(n_dec, n_heads, 3), kv_slc, kv_win,
                                    rope_s, z.reshape(n_dec, 1, n_main), l, ln_g, ln_b, sgu_w0, sgu_b0,
                                    norm_a, norm_b3, past, hpg, a_width)
        y_s = _out_proj(a_n.reshape(n_dec, a_width), b_n.reshape(n_dec, b_width), y_s, rows, l, w_out_b, post_mix)
        y_s = _ffn(y_s, rows, l, pre_ffn, post_ffn, w_up_b, w_down_b)
        outs[3].append(kv_cmp.reshape(n_dec, 1, N_KV, 2, HEAD_DIM))
        outs[4].append(kv_slc.reshape(n_dec, 1, N_KV, 2, HEAD_DIM))
        outs[5].append(kv_win.reshape(n_dec, 1, N_KV, 2, HEAD_DIM))
        outs[6].append(v_n)
    return (y_p.reshape(batch, seq, d), y_s.reshape(n_dec, 1, d), *[jnp.stack(o) for o in outs])
```

```python
import functools

import jax
import jax.numpy as jnp
from jax import lax
from jax.experimental import pallas as pl
from jax.experimental.pallas import tpu as pltpu

F32 = jnp.float32
BF16 = jnp.bfloat16
I32 = jnp.int32

LANES = 128
HEAD_DIM = 128
N_KV = 2
PAGE_SIZE = 128
CHUNK = 128
CMP_BLOCK = 32
CMP_STRIDE = 16
SLC_BLOCK = 64
N_SEL = 16
WINDOW = 512
ROPE_DIM = HEAD_DIM // 4
ROPE_HALF = ROPE_DIM // 2
ROPE_THETA = 500000.0
EPS = 1e-6
FORCE = 1e4
SCALE = HEAD_DIM ** -0.5
EXP2_SCALE = SCALE * 1.4426950408889634
KV_COLS = N_KV * 2 * HEAD_DIM
CMP_STRIPS = CMP_BLOCK // CMP_STRIDE
STRIP_COLS = CMP_STRIDE * HEAD_DIM
CMP_PER_SLC = SLC_BLOCK // CMP_STRIDE
CMP_INSIDE = (SLC_BLOCK - CMP_BLOCK) // CMP_STRIDE + 1
VMEM_LIMIT = 56 * 2 ** 20


def _dot(a, b):
    return jnp.dot(a, b, preferred_element_type=F32)


def _dot_nt(a, b):
    return lax.dot_general(a, b, (((1,), (1,)), ((), ())), preferred_element_type=F32)


def _gelu(x):
    return 0.5 * x * (1.0 + jnp.tanh(0.7978845608028654 * (x + 0.044715 * (x * x * x))))


def _sigmoid(x):
    return 1.0 / (1.0 + jnp.exp(-x))


def _rms(x, g):
    return x * lax.rsqrt(jnp.mean(x * x, axis=-1, keepdims=True) + EPS) * g


def _softmax_neg_inf(s):
    m = jnp.max(s, axis=-1, keepdims=True)
    m = jnp.where(m > -jnp.inf, m, 0.0)
    e = jnp.exp(s - m)
    return e / jnp.maximum(jnp.sum(e, axis=-1, keepdims=True), 1e-30)


def _msoftmax(s, mask):
    return _softmax_neg_inf(jnp.where(mask, s, -jnp.inf))


def _attend(s, v):
    m = jnp.max(s, axis=-1, keepdims=True)
    m = jnp.where(m > -jnp.inf, m, 0.0)
    e = jnp.exp2(s - m)
    return _dot(e.astype(BF16), v) / jnp.maximum(jnp.sum(e, axis=-1, keepdims=True), 1e-30)


def _rope(x, c, sa, sb):
    return x * c + pltpu.roll(x, LANES - ROPE_HALF, axis=1) * sa + pltpu.roll(x, ROPE_HALF, axis=1) * sb


def _split_dot(x, m):
    hi = x.astype(BF16)
    r1 = x - hi.astype(F32)
    mid = r1.astype(BF16)
    lo = (r1 - mid.astype(F32)).astype(BF16)
    return _dot(hi, m) + _dot(mid, m) + _dot(lo, m)


def _modulated_norm(x_ref, g_ref, sc_ref, sh_ref, h_ref):
    h = _rms(x_ref[...], g_ref[...]) * (1.0 + sc_ref[...]) + sh_ref[...]
    h_ref[...] = h.astype(h_ref.dtype)


def _gated_residual(x_ref, gate_ref, g_ref, o_ref):
    o_ref[...] = x_ref[...] + gate_ref[...] * _rms(o_ref[...], g_ref[...])


def _params(sem, vmem=None):
    return pltpu.CompilerParams(dimension_semantics=sem, vmem_limit_bytes=vmem)


def _ada_kernel(c_ref, w_ref, b_ref, o_ref):
    c = c_ref[...]
    o_ref[...] = _dot((c * _sigmoid(c)).astype(BF16), w_ref[...].astype(BF16)) + b_ref[...]


def _ada(c_all, w_ada, b_ada):
    depth, d, n = w_ada.shape
    rows = c_all.shape[0]
    tn = 1024
    return pl.pallas_call(
        _ada_kernel,
        grid=(depth, n // tn),
        in_specs=[pl.BlockSpec((rows, d), lambda l, j: (0, 0)),
                  pl.BlockSpec((None, d, tn), lambda l, j: (l, 0, j)),
                  pl.BlockSpec((None, 1, tn), lambda l, j: (l, 0, j))],
        out_specs=pl.BlockSpec((None, rows, tn), lambda l, j: (l, 0, j)),
        out_shape=jax.ShapeDtypeStruct((depth, rows, n), F32),
        compiler_params=_params(("arbitrary", "arbitrary"), VMEM_LIMIT),
        name="ada",
    )(c_all, w_ada, b_ada.reshape(depth, 1, n))


class _Rows:
    def __init__(self, mod, layer, d, m, tm, rows_per_batch, mod_row0):
        self.m, self.tm, self.d = m, tm, d
        depth, r, n = mod.shape
        if rows_per_batch is None:
            self.mod = mod
            self._spec = lambda k: pl.BlockSpec((None, tm, d), lambda i, *_: (layer, mod_row0 // tm, k))
        else:
            tiles = rows_per_batch // tm
            self.mod = mod.reshape(depth, r, 1, n)
            self._spec = lambda k: pl.BlockSpec((None, None, 1, d),
                                                lambda i, *_: (layer, mod_row0 + i // tiles, 0, k))

    def mod_spec(self, k):
        return self._spec(k)


def _vec_spec(layer, n):
    return pl.BlockSpec((None, 1, n), lambda *_: (layer, 0, 0))


def _in_kernel(x_ref, g_ref, sc_ref, sh_ref, w_ref, wg_ref, z_ref, gt_ref, h_ref):
    @pl.when(pl.program_id(1) == 0)
    def _():
        _modulated_norm(x_ref, g_ref, sc_ref, sh_ref, h_ref)
        gt_ref[...] = _dot(h_ref[...].astype(BF16), wg_ref[...])

    z_ref[...] = _dot(h_ref[...].astype(BF16), w_ref[...].astype(BF16))


def _in_proj(x, rows, layer, norm_w, w_main, w_gate, n):
    m, tm, d = rows.m, rows.tm, rows.d
    tn = 768 if n % 768 == 0 else 512
    h_dtype = BF16 if tm % 16 == 0 else F32
    return pl.pallas_call(
        _in_kernel,
        grid=(m // tm, n // tn),
        in_specs=[pl.BlockSpec((tm, d), lambda i, j: (i, 0)),
                  _vec_spec(layer, d), rows.mod_spec(1), rows.mod_spec(0),
                  pl.BlockSpec((None, d, tn), lambda i, j: (layer, 0, j)),
                  pl.BlockSpec((None, d, LANES), lambda i, j: (layer, 0, 0))],
        out_specs=[pl.BlockSpec((tm, tn), lambda i, j: (i, j)),
                   pl.BlockSpec((tm, LANES), lambda i, j: (i, 0))],
        out_shape=[jax.ShapeDtypeStruct((m, n), F32), jax.ShapeDtypeStruct((m, LANES), F32)],
        scratch_shapes=[pltpu.VMEM((tm, d), h_dtype)],
        compiler_params=_params(("arbitrary", "arbitrary"), VMEM_LIMIT),
        name="in_proj",
    )(x, norm_w, rows.mod, rows.mod, w_main, w_gate)


def _layernorm(v, g, b):
    mu = jnp.mean(v, axis=-1, keepdims=True)
    vc = v - mu
    return vc * lax.rsqrt(jnp.mean(vc * vc, axis=-1, keepdims=True) + EPS) * g + b


def _sgu_kernel(u_ref, v_ref, lg_ref, lb_ref, w_ref, bt_ref, na_ref, o_ref, *, groups, chunks):
    row = lax.broadcasted_iota(I32, (CHUNK, CHUNK), 0)
    col = lax.broadcasted_iota(I32, (CHUNK, CHUNK), 1)
    ws = [jnp.where(row >= col, w_ref[g], 0.0).astype(BF16) for g in range(groups)]
    for c in range(chunks):
        rs = slice(c * CHUNK, (c + 1) * CHUNK)
        vn = _layernorm(_gelu(v_ref[rs, :]), lg_ref[...], lb_ref[...]).astype(BF16)
        u = _gelu(u_ref[rs, :])
        outs = []
        for g in range(groups):
            cs = slice(g * HEAD_DIM, (g + 1) * HEAD_DIM)
            outs.append(u[:, cs] * (_dot(ws[g], vn[:, cs]) + bt_ref[:, g:g + 1]))
        o_ref[rs, :] = _rms(jnp.concatenate(outs, axis=1), na_ref[...]).astype(BF16)


def _sgu(z, layer, a_width, ln_g, ln_b, sgu_w, sgu_bt, norm_a):
    m = z.shape[0]
    groups = a_width // HEAD_DIM
    chunks = 2
    tr = chunks * CHUNK
    return pl.pallas_call(
        functools.partial(_sgu_kernel, groups=groups, chunks=chunks),
        grid=(m // tr,),
        in_specs=[pl.BlockSpec((tr, a_width), lambda i: (i, 0)),
                  pl.BlockSpec((tr, a_width), lambda i: (i, 1)),
                  _vec_spec(layer, a_width), _vec_spec(layer, a_width),
                  pl.BlockSpec((None, groups, CHUNK, CHUNK), lambda i: (layer, 0, 0, 0)),
                  pl.BlockSpec((None, CHUNK, groups), lambda i: (layer, 0, 0)),
                  _vec_spec(layer, a_width)],
        out_specs=pl.BlockSpec((tr, a_width), lambda i: (i, 0)),
        out_shape=jax.ShapeDtypeStruct((m, a_width), BF16),
        compiler_params=_params(("arbitrary",), VMEM_LIMIT),
        name="sgu",
    )(z, z, ln_g, ln_b, sgu_w, sgu_bt, norm_a)


def _kv_kernel(kv_ref, rc_ref, rsa_ref, rsb_ref, cmp_ref, slc_ref, win_ref, slcb_ref, winb_ref):
    cmp_ref[...] = kv_ref[:, 0:KV_COLS]
    rc, rsa, rsb = rc_ref[...], rsa_ref[...], rsb_ref[...]
    for t, (o_ref, ob_ref) in enumerate(((slc_ref, slcb_ref), (win_ref, winb_ref))):
        for g in range(N_KV):
            k0 = (t + 1) * KV_COLS + g * 2 * HEAD_DIM
            k = _rope(kv_ref[:, k0:k0 + HEAD_DIM], rc, rsa, rsb)
            v = kv_ref[:, k0 + HEAD_DIM:k0 + 2 * HEAD_DIM]
            o0 = g * 2 * HEAD_DIM
            o_ref[:, o0:o0 + HEAD_DIM] = k
            o_ref[:, o0 + HEAD_DIM:o0 + 2 * HEAD_DIM] = v
            ob_ref[:, o0:o0 + HEAD_DIM] = k.astype(BF16)
            ob_ref[:, o0 + HEAD_DIM:o0 + 2 * HEAD_DIM] = v.astype(BF16)


def _kv_cache_kernel(kv_ref, rc_ref, rsa_ref, rsb_ref, *refs):
    cmp_ref, slc_ref, win_ref, slcb_ref, winb_ref = refs[-5:]
    rc, rsa, rsb = rc_ref[...], rsa_ref[...], rsb_ref[...]
    for g in range(N_KV):
        for t in range(2):
            c0 = (g * 2 + t) * HEAD_DIM
            cmp_ref[:, g, t, :] = kv_ref[:, c0:c0 + HEAD_DIM]
    for j, (o_ref, ob_ref) in enumerate(((slc_ref, slcb_ref), (win_ref, winb_ref))):
        for g in range(N_KV):
            k0 = (j + 1) * KV_COLS + g * 2 * HEAD_DIM
            k = _rope(kv_ref[:, k0:k0 + HEAD_DIM], rc, rsa, rsb)
            v = kv_ref[:, k0 + HEAD_DIM:k0 + 2 * HEAD_DIM]
            o0 = g * 2 * HEAD_DIM
            o_ref[:, g, 0, :] = k
            o_ref[:, g, 1, :] = v
            ob_ref[:, o0:o0 + HEAD_DIM] = k.astype(BF16)
            ob_ref[:, o0 + HEAD_DIM:o0 + 2 * HEAD_DIM] = v.astype(BF16)


def _kv_rope_prompt(z, kv_block, tables, tr, layer, depth, batch, seq, win_keep, prev):
    m = z.shape[0]
    nt = seq // tr
    first = (seq - win_keep) // tr
    tspec = pl.BlockSpec((tr, LANES), lambda i: (i % nt, 0))
    bspec = pl.BlockSpec((tr, KV_COLS), lambda i: (i, 0))
    cache_block = (None, None, tr, N_KV, 2, HEAD_DIM)
    cspec = pl.BlockSpec(cache_block, lambda i: (layer, i // nt, i % nt, 0, 0, 0))
    wspec = pl.BlockSpec(cache_block, lambda i: (layer, i // nt, jnp.maximum(i % nt - first, 0), 0, 0, 0))
    cache_shape = lambda rows: jax.ShapeDtypeStruct((depth, batch, rows, N_KV, 2, HEAD_DIM), F32)
    return pl.pallas_call(
        _kv_cache_kernel,
        grid=(m // tr,),
        in_specs=[pl.BlockSpec((tr, 3 * KV_COLS), lambda i: (i, kv_block)), tspec, tspec, tspec]
        + [pl.BlockSpec(memory_space=pl.ANY)] * len(prev),
        out_specs=[cspec, cspec, wspec, bspec, bspec],
        out_shape=[cache_shape(seq), cache_shape(seq), cache_shape(win_keep)]
        + [jax.ShapeDtypeStruct((m, KV_COLS), BF16)] * 2,
        input_output_aliases={4 + i: i for i in range(len(prev))},
        compiler_params=_params(("arbitrary",), VMEM_LIMIT),
        name="kv_rope_prompt",
    )(z, *tables, *prev)


def _kv_rope(z, kv_block, tables, tr):
    m = z.shape[0]
    t_rows = tables[0].shape[0]
    nt = t_rows // tr
    tspec = pl.BlockSpec((tr, LANES), lambda i: (i % nt, 0))
    ospec = pl.BlockSpec((tr, KV_COLS), lambda i: (i, 0))
    return pl.pallas_call(
        _kv_kernel,
        grid=(m // tr,),
        in_specs=[pl.BlockSpec((tr, 3 * KV_COLS), lambda i: (i, kv_block)), tspec, tspec, tspec],
        out_specs=[ospec] * 5,
        out_shape=[jax.ShapeDtypeStruct((m, KV_COLS), F32)] * 3 + [jax.ShapeDtypeStruct((m, KV_COLS), BF16)] * 2,
        compiler_params=_params(("arbitrary",), VMEM_LIMIT),
        name="kv_rope",
    )(z, *tables)


def _compress_strips(xf, pe_ref, w1_ref, sel):
    p = _dot((xf + pe_ref[sel, 0:1, :]).astype(BF16), w1_ref[sel, 0:STRIP_COLS, :])
    q = _dot((xf + pe_ref[sel, 1:2, :]).astype(BF16), w1_ref[sel, STRIP_COLS:2 * STRIP_COLS, :])
    return p, q


def _cmp_kernel(x_ref, pe_ref, w1_ref, w2_ref, o_ref, *, n):
    sel = pl.program_id(1) % 2
    xf = jnp.concatenate([x_ref[pl.ds(r, n, stride=CMP_STRIDE), :] for r in range(CMP_STRIDE)], axis=1)
    p, q = _compress_strips(xf, pe_ref, w1_ref, sel)
    hid = _gelu(p + pltpu.roll(q, n - 1, axis=0))
    o_ref[...] = _dot(hid.astype(BF16), w2_ref[sel]).astype(BF16)


def _compress_prompt(z, batch, seq, cmp_block, pe2, w1, w2):
    n = seq // CMP_STRIDE
    hidden = w1.shape[-1]
    heads = 2 * N_KV
    return pl.pallas_call(
        functools.partial(_cmp_kernel, n=n),
        grid=(batch, heads),
        in_specs=[pl.BlockSpec((seq, HEAD_DIM), lambda b, h: (b, cmp_block * heads + h)),
                  pl.BlockSpec((2, 2, STRIP_COLS), lambda b, h: (0, 0, 0)),
                  pl.BlockSpec((2, 2 * STRIP_COLS, hidden), lambda b, h: (0, 0, 0)),
                  pl.BlockSpec((2, hidden, HEAD_DIM), lambda b, h: (0, 0, 0))],
        out_specs=pl.BlockSpec((None, None, n, HEAD_DIM), lambda b, h: (b, h, 0, 0)),
        out_shape=jax.ShapeDtypeStruct((batch, heads, n, HEAD_DIM), BF16),
        compiler_params=_params(("arbitrary", "arbitrary"), VMEM_LIMIT),
        name="compress_prompt",
    )(z, pe2, w1, w2)


def _attn_kernel(q_ref, gt_ref, cmp_ref, slc_ref, win_ref, rc_ref, rsa_ref, rsb_ref, nb_ref, sel_ref, exp_ref,
                 o_ref, acc_ref, *, tq, seq, hpg, n_cmp, kc):
    q0 = pl.program_id(1) * tq
    n_chunks = (q0 + tq + kc - 1) // kc
    kiota = lax.broadcasted_iota(I32, (1, kc), 1)
    pos = q0 + lax.broadcasted_iota(I32, (tq, 1), 0)
    posh = jnp.concatenate([pos] * hpg, axis=0)
    rc = jnp.concatenate([rc_ref[...]] * hpg, axis=0)
    rsa = jnp.concatenate([rsa_ref[...]] * hpg, axis=0)
    rsb = jnp.concatenate([rsb_ref[...]] * hpg, axis=0)
    sig = _sigmoid(gt_ref[...])
    clane = lax.broadcasted_iota(I32, (1, n_cmp), 1)
    n_blk = seq // SLC_BLOCK
    n_sel = min(N_SEL, n_blk)
    nb8 = -(-n_blk // 8) * 8
    brow = lax.broadcasted_iota(I32, (nb8, 1), 0)
    cur_t = (q0 + lax.broadcasted_iota(I32, (1, tq), 1)) // SLC_BLOCK
    forced_t = (brow == 0) | (brow == cur_t) | (brow == cur_t - 1)
    brow_q = lax.broadcasted_iota(I32, (nb8, tq), 0)
    wk = WINDOW + tq
    wstart = pl.multiple_of(jnp.maximum(q0 - WINDOW, 0), LANES)
    wpos = wstart + lax.broadcasted_iota(I32, (1, wk), 1)
    wdiff = posh - wpos
    wmask = (wdiff >= 0) & (wdiff < WINDOW)
    cmask = (clane * CMP_STRIDE + (CMP_BLOCK - 1)) <= posh
    for g in range(N_KV):
        heads = [g * hpg + h for h in range(hpg)]
        qc = jnp.concatenate([q_ref[:, hd * HEAD_DIM:(hd + 1) * HEAD_DIM] for hd in heads], axis=0)
        qrb = (_rope(qc, rc, rsa, rsb) * EXP2_SCALE).astype(BF16)
        k0 = g * 2 * HEAD_DIM
        v0 = k0 + HEAD_DIM
        p = _msoftmax(_dot_nt(qc.astype(BF16), cmp_ref[2 * g]) * SCALE, cmask)
        o_c = _dot(p.astype(BF16), cmp_ref[2 * g + 1])
        imp = p[0:tq]
        for h in range(1, hpg):
            imp = imp + p[h * tq:(h + 1) * tq]
        impb = _split_dot(imp, sel_ref[...])
        score = jnp.where(brow <= cur_t, jnp.where(forced_t, FORCE, impb.T[0:nb8]), -FORCE)
        rank = jnp.zeros((nb8, tq), F32)
        for i in range(n_blk):
            ci = score[i:i + 1, :]
            tie = jnp.where(brow_q > i, 1.0, 0.0)
            rank = rank + jnp.where(ci > score, 1.0, jnp.where(ci == score, tie, 0.0))
        chosen_t = jnp.where((rank < float(n_sel)) & (brow <= cur_t), 1.0, 0.0)
        chosen_b = jnp.concatenate([chosen_t, jnp.zeros((LANES - nb8, tq), F32)], axis=0).T.astype(BF16)
        rows = hpg * tq

        def chunk(c, carry, diagonal):
            m, l, acc = carry
            r0 = pl.multiple_of(c * kc, kc)
            open_key = _dot(chosen_b, exp_ref[c]) > 0.5
            if diagonal:
                open_key = open_key & ((r0 + kiota) <= pos)
            bias = jnp.where(open_key, 0.0, -jnp.inf)
            s = _dot_nt(qrb, slc_ref[pl.ds(r0, kc), k0:k0 + HEAD_DIM]).reshape(hpg, tq, kc) + bias[None]
            m_new = jnp.maximum(m, jnp.max(s, axis=-1, keepdims=True))
            m_ref = jnp.where(m_new > -jnp.inf, m_new, 0.0)
            alpha = jnp.exp2(m - m_ref)
            e = jnp.exp2(s - m_ref)
            l = alpha * l + jnp.sum(e, axis=-1, keepdims=True)
            pv = _dot(e.reshape(rows, kc).astype(BF16), slc_ref[pl.ds(r0, kc), v0:v0 + HEAD_DIM])
            return m_new, l, alpha.reshape(rows, 1) * acc + pv

        init = (jnp.full((hpg, tq, 1), -jnp.inf, F32), jnp.zeros((hpg, tq, 1), F32),
                jnp.zeros((rows, HEAD_DIM), F32))
        carry = lax.fori_loop(0, n_chunks - 1, functools.partial(chunk, diagonal=False), init)
        _, l, acc = chunk(n_chunks - 1, carry, diagonal=True)
        o_s = acc / jnp.maximum(l.reshape(rows, 1), 1e-30)
        s = jnp.where(wmask, _dot_nt(qrb, win_ref[pl.ds(wstart, wk), k0:k0 + HEAD_DIM]), -jnp.inf)
        o_w = _attend(s, win_ref[pl.ds(wstart, wk), v0:v0 + HEAD_DIM])
        for h, hd in enumerate(heads):
            rs = slice(h * tq, (h + 1) * tq)
            acc_ref[:, hd * HEAD_DIM:(hd + 1) * HEAD_DIM] = (
                sig[:, 3 * hd:3 * hd + 1] * o_c[rs] + sig[:, 3 * hd + 1:3 * hd + 2] * o_s[rs]
                + sig[:, 3 * hd + 2:3 * hd + 3] * o_w[rs])
    o_ref[...] = _rms(acc_ref[...], nb_ref[...]).astype(BF16)


def _attn_prompt(z, gt, kvc, slc_b, win_b, tables, layer, norm_b, sel_m, exp_m, batch, seq, b_width, q_block):
    tq = 128
    nq = seq // tq
    hpg = b_width // HEAD_DIM // N_KV
    n_cmp = kvc.shape[2]
    kc = exp_m.shape[2]
    tspec = pl.BlockSpec((tq, LANES), lambda b, i: (i, 0))
    kvspec = pl.BlockSpec((seq, KV_COLS), lambda b, i: (b, 0))
    return pl.pallas_call(
        functools.partial(_attn_kernel, tq=tq, seq=seq, hpg=hpg, n_cmp=n_cmp, kc=kc),
        grid=(batch, nq),
        in_specs=[pl.BlockSpec((tq, b_width), lambda b, i: (b * nq + i, q_block)),
                  pl.BlockSpec((tq, LANES), lambda b, i: (b * nq + i, 0)),
                  pl.BlockSpec((None, 2 * N_KV, n_cmp, HEAD_DIM), lambda b, i: (b, 0, 0, 0)),
                  kvspec, kvspec, tspec, tspec, tspec,
                  pl.BlockSpec((None, 1, b_width), lambda b, i: (layer, 0, 0)),
                  pl.BlockSpec(sel_m.shape, lambda b, i: (0, 0)),
                  pl.BlockSpec(exp_m.shape, lambda b, i: (0, 0, 0))],
        out_specs=pl.BlockSpec((tq, b_width), lambda b, i: (b * nq + i, 0)),
        out_shape=jax.ShapeDtypeStruct((batch * seq, b_width), BF16),
        scratch_shapes=[pltpu.VMEM((tq, b_width), F32)],
        compiler_params=_params(("arbitrary", "arbitrary"), VMEM_LIMIT),
        name="attn_prompt",
    )(z, gt, kvc, slc_b, win_b, *tables, norm_b, sel_m, exp_m)


def _out_kernel(a_ref, b_ref, w_ref, x_ref, gate_ref, g_ref, o_ref, *, a_width):
    o_ref[...] = _dot(a_ref[...], w_ref[0:a_width, :]) + _dot(b_ref[...], w_ref[a_width:, :])
    _gated_residual(x_ref, gate_ref, g_ref, o_ref)


def _out_proj(a_n, b_n, x, rows, layer, w_out, norm_w):
    m, tm, d = rows.m, rows.tm, rows.d
    a_width = a_n.shape[1]
    b_width = b_n.shape[1]
    return pl.pallas_call(
        functools.partial(_out_kernel, a_width=a_width),
        grid=(m // tm,),
        in_specs=[pl.BlockSpec((tm, a_width), lambda i: (i, 0)),
                  pl.BlockSpec((tm, b_width), lambda i: (i, 0)),
                  pl.BlockSpec((None, a_width + b_width, d), lambda i: (layer, 0, 0)),
                  pl.BlockSpec((tm, d), lambda i: (i, 0)),
                  rows.mod_spec(2), _vec_spec(layer, d)],
        out_specs=pl.BlockSpec((tm, d), lambda i: (i, 0)),
        out_shape=jax.ShapeDtypeStruct((m, d), F32),
        compiler_params=_params(("arbitrary",), VMEM_LIMIT),
        name="out_proj",
    )(a_n, b_n, w_out, x, rows.mod, norm_w)


def _ffn_kernel(x_ref, g1_ref, sc_ref, sh_ref, wu_ref, wd_ref, gate_ref, g2_ref, o_ref, h_ref):
    k = pl.program_id(1)

    @pl.when(k == 0)
    def _():
        _modulated_norm(x_ref, g1_ref, sc_ref, sh_ref, h_ref)
        o_ref[...] = jnp.zeros_like(o_ref)

    up = jnp.maximum(_dot(h_ref[...].astype(BF16), wu_ref[...]), 0.0)
    o_ref[...] += _dot((up * up).astype(BF16), wd_ref[...])

    @pl.when(k == pl.num_programs(1) - 1)
    def _():
        _gated_residual(x_ref, gate_ref, g2_ref, o_ref)


def _ffn(x, rows, layer, norm_pre, norm_post, w_up, w_down):
    m, tm, d = rows.m, rows.tm, rows.d
    d_ff = w_up.shape[-1]
    tf = 512
    h_dtype = BF16 if tm % 16 == 0 else F32
    x_mode = dict(pipeline_mode=pl.Buffered(1)) if tm * d * 4 >= 8 * 2 ** 20 else {}
    return pl.pallas_call(
        _ffn_kernel,
        grid=(m // tm, d_ff // tf),
        in_specs=[pl.BlockSpec((tm, d), lambda i, k: (i, 0), **x_mode),
                  _vec_spec(layer, d), rows.mod_spec(4), rows.mod_spec(3),
                  pl.BlockSpec((None, d, tf), lambda i, k: (layer, 0, k)),
                  pl.BlockSpec((None, tf, d), lambda i, k: (layer, k, 0)),
                  rows.mod_spec(5), _vec_spec(layer, d)],
        out_specs=pl.BlockSpec((tm, d), lambda i, k: (i, 0)),
        out_shape=jax.ShapeDtypeStruct((m, d), F32),
        scratch_shapes=[pltpu.VMEM((tm, d), h_dtype)],
        compiler_params=_params(("arbitrary", "arbitrary"), VMEM_LIMIT),
        name="ffn",
    )(x, norm_pre, rows.mod, rows.mod, w_up, w_down, rows.mod, norm_post)


def _scmp_kernel(pt_ref, cache_ref, new_ref, pe_ref, w1_ref, w2_ref, o_ref, buf, sem, p_s, q_s,
                 *, layer, pages, n_chunk, n_batch):
    b = pl.program_id(0)
    c = pl.program_id(1)
    step = b * n_chunk + c
    slot = step % 2
    strips = pages * (PAGE_SIZE // CMP_STRIDE)
    n_past = n_chunk * strips

    def copies(bb, cc, sl):
        return [pltpu.make_async_copy(cache_ref.at[layer, pt_ref[bb, cc * pages + p], :, gs // 2, gs % 2, :],
                                      buf.at[sl, gs, p], sem.at[sl])
                for p in range(pages) for gs in range(2 * N_KV)]

    @pl.when(step == 0)
    def _():
        for cp in copies(0, 0, 0):
            cp.start()

    @pl.when(step + 1 < n_batch * n_chunk)
    def _():
        nxt = step + 1
        for cp in copies(nxt // n_chunk, nxt % n_chunk, 1 - slot):
            cp.start()

    for cp in copies(b, c, slot):
        cp.wait()

    cur = buf.at[slot]
    row0 = pl.multiple_of(c * strips, strips)
    for gs in range(2 * N_KV):
        sel = gs % 2
        xf = jnp.concatenate(
            [cur[gs, :, pl.ds(r, PAGE_SIZE // CMP_STRIDE, stride=CMP_STRIDE), :].reshape(strips, HEAD_DIM)
             for r in range(CMP_STRIDE)], axis=1)
        p, q = _compress_strips(xf, pe_ref, w1_ref, sel)
        p_s[gs, pl.ds(row0, strips), :] = p
        q_s[gs, pl.ds(row0, strips), :] = q

    @pl.when(c == n_chunk - 1)
    def _():
        for gs in range(2 * N_KV):
            sel = gs % 2
            new = new_ref[:, gs * HEAD_DIM:(gs + 1) * HEAD_DIM]
            xn = jnp.concatenate([new, jnp.zeros((1, STRIP_COLS - HEAD_DIM), F32)], axis=1) + pe_ref[sel, 1:2, :]
            xn = jnp.broadcast_to(xn, (8, STRIP_COLS)).astype(BF16)
            q_s[gs, n_past:n_past + 8, :] = _dot(xn, w1_ref[sel, STRIP_COLS:2 * STRIP_COLS, :])
            hid = _gelu(p_s[gs] + q_s[gs, 1:n_past + 1, :])
            o_ref[gs] = _dot(hid.astype(BF16), w2_ref[sel]).astype(BF16)


def _compress_sample(page_table, cache, layer, new_cmp, pe2, w1, w2):
    n_batch, n_pages = page_table.shape
    pages = 32 if n_pages % 32 == 0 else n_pages
    n_chunk = n_pages // pages
    n_past = n_pages * (PAGE_SIZE // CMP_STRIDE)
    hidden = w1.shape[-1]
    grid_spec = pltpu.PrefetchScalarGridSpec(
        num_scalar_prefetch=1,
        grid=(n_batch, n_chunk),
        in_specs=[pl.BlockSpec(memory_space=pl.ANY),
                  pl.BlockSpec((None, 1, KV_COLS), lambda b, c, pt: (b, 0, 0)),
                  pl.BlockSpec((2, 2, STRIP_COLS), lambda b, c, pt: (0, 0, 0)),
                  pl.BlockSpec((2, 2 * STRIP_COLS, hidden), lambda b, c, pt: (0, 0, 0)),
                  pl.BlockSpec((2, hidden, HEAD_DIM), lambda b, c, pt: (0, 0, 0))],
        out_specs=pl.BlockSpec((None, 2 * N_KV, n_past, HEAD_DIM), lambda b, c, pt: (b, 0, 0, 0)),
        scratch_shapes=[pltpu.VMEM((2, 2 * N_KV, pages, PAGE_SIZE, HEAD_DIM), F32),
                        pltpu.SemaphoreType.DMA((2,)),
                        pltpu.VMEM((2 * N_KV, n_past, hidden), F32),
                        pltpu.VMEM((2 * N_KV, n_past + 8, hidden), F32)])
    return pl.pallas_call(
        functools.partial(_scmp_kernel, layer=layer, pages=pages, n_chunk=n_chunk, n_batch=n_batch),
        grid_spec=grid_spec,
        out_shape=jax.ShapeDtypeStruct((n_batch, 2 * N_KV, n_past, HEAD_DIM), BF16),
        compiler_params=_params(("arbitrary", "arbitrary"), VMEM_LIMIT),
        name="compress_sample",
    )(page_table, cache, new_cmp.reshape(n_batch, 1, KV_COLS), pe2, w1, w2)


def _ssel_kernel(q_ref, kvc_ref, sel_ref, oc_ref, idx_ref, *, pos, hpg, n_cmp, n_lane):
    qb = q_ref[...].astype(BF16)
    n_heads = qb.shape[0]
    head_row = lax.broadcasted_iota(I32, (n_heads, 1), 0)
    clane = lax.broadcasted_iota(I32, (1, n_cmp), 1)
    cmask = (clane * CMP_STRIDE + (CMP_BLOCK - 1)) <= pos
    lane = lax.broadcasted_iota(I32, (1, n_lane), 1)
    cur = pos // SLC_BLOCK
    forced = (lane == 0) | (lane == cur) | (lane == cur - 1)
    ri = lax.broadcasted_iota(I32, (n_lane, n_lane), 0)
    ci = lax.broadcasted_iota(I32, (n_lane, n_lane), 1)
    slot_id = lax.broadcasted_iota(I32, (N_SEL, 1), 0)
    o_c = jnp.zeros((n_heads, HEAD_DIM), F32)
    for g in range(N_KV):
        mine = (head_row // hpg) == g
        p = _msoftmax(_dot_nt(qb, kvc_ref[2 * g]) * SCALE, cmask)
        o_c = jnp.where(mine, _dot(p.astype(BF16), kvc_ref[2 * g + 1]), o_c)
        imp = jnp.sum(jnp.where(mine, p, 0.0), axis=0, keepdims=True)
        impb = _split_dot(jnp.broadcast_to(imp, (8, n_cmp)), sel_ref[...])[0:1]
        score = jnp.where(lane <= cur, jnp.where(forced, FORCE, impb), -FORCE)
        srow = jnp.broadcast_to(score, (n_lane, n_lane))
        scol = srow.T
        ahead = (scol > srow) | ((scol == srow) & (ri < ci))
        rank = jnp.sum(jnp.where(ahead, 1.0, 0.0), axis=0, keepdims=True)
        chosen = jnp.where((rank < float(N_SEL)) & (lane <= cur), 1.0, 0.0)
        ccol = jnp.broadcast_to(chosen, (n_lane, n_lane)).T
        before = jnp.sum(jnp.where(ri < ci, ccol, 0.0), axis=0, keepdims=True)
        hit = (chosen > 0.5) & (before == slot_id.astype(F32))
        idx = jnp.sum(jnp.where(hit, lane.astype(F32), 0.0), axis=1, keepdims=True)
        idx_ref[g] = jnp.broadcast_to(idx, (N_SEL, LANES)).astype(I32)
    oc_ref[...] = o_c


def _select_sample(q3, kvc, sel_m, pos, hpg):
    n_batch, n_heads, _ = q3.shape
    n_cmp = kvc.shape[2]
    n_lane = sel_m.shape[1]
    return pl.pallas_call(
        functools.partial(_ssel_kernel, pos=pos, hpg=hpg, n_cmp=n_cmp, n_lane=n_lane),
        grid=(n_batch,),
        in_specs=[pl.BlockSpec((None, n_heads, HEAD_DIM), lambda b: (b, 0, 0)),
                  pl.BlockSpec((None, 2 * N_KV, n_cmp, HEAD_DIM), lambda b: (b, 0, 0, 0)),
                  pl.BlockSpec(sel_m.shape, lambda b: (0, 0))],
        out_specs=[pl.BlockSpec((None, n_heads, HEAD_DIM), lambda b: (b, 0, 0)),
                   pl.BlockSpec((None, N_KV, N_SEL, LANES), lambda b: (b, 0, 0, 0))],
        out_shape=[jax.ShapeDtypeStruct((n_batch, n_heads, HEAD_DIM), F32),
                   jax.ShapeDtypeStruct((n_batch, N_KV, N_SEL, LANES), I32)],
        compiler_params=_params(("arbitrary",), VMEM_LIMIT),
        name="select_sample",
    )(q3, kvc, sel_m)


def _smix_kernel(pt_ref, idx_ref, cache_ref, wcache_ref, q_ref, oc_ref, gt_ref, slc_ref, win_ref, rope_ref,
                 u_ref, v_ref, lg_ref, lb_ref, w0_ref, b0_ref, na_ref, nb_ref,
                 a_ref, b_ref, vn_ref, buf, wbuf, sem, *, layer, pos, hpg, nb_past):
    b = pl.program_id(0)
    halves = PAGE_SIZE // SLC_BLOCK

    def copies():
        out = []
        for g in range(N_KV):
            for k in range(N_SEL):
                jp = jnp.minimum(idx_ref[b, g, k], nb_past - 1)
                phys = pt_ref[b, jp // halves]
                rows = pl.ds(pl.multiple_of((jp % halves) * SLC_BLOCK, SLC_BLOCK), SLC_BLOCK)
                for t in range(2):
                    out.append(pltpu.make_async_copy(cache_ref.at[layer, phys, rows, g, t, :], buf.at[t, g, k],
                                                     sem.at[0]))
            for t in range(2):
                out.append(pltpu.make_async_copy(wcache_ref.at[layer, b, :, g, t, :], wbuf.at[g, t], sem.at[1]))
        return out

    for cp in copies():
        cp.start()

    vn = _layernorm(_gelu(v_ref[...]), lg_ref[...], lb_ref[...])
    vn_ref[...] = vn
    a_out = _gelu(u_ref[...]) * (w0_ref[...] * vn + b0_ref[...])
    a_ref[...] = _rms(a_out, na_ref[...]).astype(BF16)

    q = q_ref[...]
    n_heads = q.shape[0]
    head_row = lax.broadcasted_iota(I32, (n_heads, 1), 0)
    qrb = _rope(q, rope_ref[0:1, :], rope_ref[1:2, :], rope_ref[2:3, :]).astype(BF16)
    sig = _sigmoid(gt_ref[...])
    n_keys = N_SEL * SLC_BLOCK
    klane = lax.broadcasted_iota(I32, (1, n_keys), 1)
    wb = wbuf.shape[2]
    wdiff = pos - (pos - wb + lax.broadcasted_iota(I32, (1, wb), 1))
    wmask = (wdiff >= 0) & (wdiff < WINDOW)

    for cp in copies():
        cp.wait()

    o_s = jnp.zeros((n_heads, HEAD_DIM), F32)
    o_w = jnp.zeros((n_heads, HEAD_DIM), F32)
    for g in range(N_KV):
        mine = (head_row // hpg) == g
        k0 = g * 2 * HEAD_DIM
        v0 = k0 + HEAD_DIM
        kmask = jnp.zeros((1, n_keys), F32)
        has_new = jnp.zeros((1, 1), F32)
        for k in range(N_SEL):
            blk = idx_ref[b, g, k]
            in_k = (klane // SLC_BLOCK) == k
            kmask = jnp.where(in_k & (blk < nb_past), 1.0, kmask)
            has_new = jnp.where(blk >= nb_past, 1.0, has_new)
        ks = buf[0, g].reshape(n_keys, HEAD_DIM).astype(BF16)
        vs = buf[1, g].reshape(n_keys, HEAD_DIM).astype(BF16)
        s = jnp.where(kmask > 0.5, _dot_nt(qrb, ks) * SCALE, -jnp.inf)
        k_new = slc_ref[:, k0:k0 + HEAD_DIM].astype(BF16)
        v_new = slc_ref[:, v0:v0 + HEAD_DIM].astype(BF16)
        s_new = jnp.sum(qrb.astype(F32) * k_new.astype(F32), axis=-1, keepdims=True) * SCALE
        s_new = jnp.where(has_new > 0.5, s_new, -jnp.inf)
        m = jnp.maximum(jnp.max(s, axis=-1, keepdims=True), s_new)
        m = jnp.where(m > -jnp.inf, m, 0.0)
        e = jnp.exp(s - m)
        e_new = jnp.exp(s_new - m)
        den = jnp.maximum(jnp.sum(e, axis=-1, keepdims=True) + e_new, 1e-30)
        og = _dot((e / den).astype(BF16), vs) + (e_new / den).astype(BF16).astype(F32) * v_new.astype(F32)
        o_s = jnp.where(mine, og, o_s)
        s = jnp.where(wmask, _dot_nt(qrb, wbuf[g, 0].astype(BF16)) * SCALE, -jnp.inf)
        k_new = win_ref[:, k0:k0 + HEAD_DIM].astype(BF16)
        v_new = win_ref[:, v0:v0 + HEAD_DIM].astype(BF16)
        s_new = jnp.sum(qrb.astype(F32) * k_new.astype(F32), axis=-1, keepdims=True) * SCALE
        m = jnp.maximum(jnp.max(s, axis=-1, keepdims=True), s_new)
        e = jnp.exp(s - m)
        e_new = jnp.exp(s_new - m)
        den = jnp.maximum(jnp.sum(e, axis=-1, keepdims=True) + e_new, 1e-30)
        og = (_dot((e / den).astype(BF16), wbuf[g, 1].astype(BF16))
              + (e_new / den).astype(BF16).astype(F32) * v_new.astype(F32))
        o_w = jnp.where(mine, og, o_w)
    b_out = sig[:, 0:1] * oc_ref[...] + sig[:, 1:2] * o_s + sig[:, 2:3] * o_w
    ms = jnp.sum(jnp.sum(b_out * b_out, axis=-1, keepdims=True), axis=0, keepdims=True) / float(n_heads * HEAD_DIM)
    b_ref[...] = (b_out * lax.rsqrt(ms + EPS) * nb_ref[...]).astype(BF16)


def _mix_sample(page_table, idx, cache_slc, cache_win, q3, o_c, gt3, slc_new, win_new, rope_s, z3, layer,
                ln_g, ln_b, w0, b0, norm_a, norm_b3, pos, hpg, a_width):
    n_batch, n_heads, _ = q3.shape
    wb = cache_win.shape[2]
    nb_past = page_table.shape[1] * (PAGE_SIZE // SLC_BLOCK)
    head_spec = pl.BlockSpec((None, n_heads, HEAD_DIM), lambda b, *_: (b, 0, 0))
    new_spec = pl.BlockSpec((None, 1, KV_COLS), lambda b, *_: (b, 0, 0))
    row_spec = lambda blk: pl.BlockSpec((None, 1, a_width), lambda b, *_: (b, 0, blk))
    vec = lambda: pl.BlockSpec((None, 1, a_width), lambda b, *_: (layer, 0, 0))
    grid_spec = pltpu.PrefetchScalarGridSpec(
        num_scalar_prefetch=2,
        grid=(n_batch,),
        in_specs=[pl.BlockSpec(memory_space=pl.ANY), pl.BlockSpec(memory_space=pl.ANY), head_spec, head_spec,
                  pl.BlockSpec((None, n_heads, 3), lambda b, *_: (b, 0, 0)),
                  new_spec, new_spec,
                  pl.BlockSpec(rope_s.shape, lambda b, *_: (0, 0)),
                  row_spec(0), row_spec(1), vec(), vec(), vec(), vec(), vec(),
                  pl.BlockSpec((None, n_heads, HEAD_DIM), lambda b, *_: (layer, 0, 0))],
        out_specs=[pl.BlockSpec((None, 1, a_width), lambda b, *_: (b, 0, 0)),
                   head_spec,
                   pl.BlockSpec((None, 1, a_width), lambda b, *_: (b, 0, 0))],
        scratch_shapes=[pltpu.VMEM((2, N_KV, N_SEL, SLC_BLOCK, HEAD_DIM), F32),
                        pltpu.VMEM((N_KV, 2, wb, HEAD_DIM), F32),
                        pltpu.SemaphoreType.DMA((2,))])
    return pl.pallas_call(
        functools.partial(_smix_kernel, layer=layer, pos=pos, hpg=hpg, nb_past=nb_past),
        grid_spec=grid_spec,
        out_shape=[jax.ShapeDtypeStruct((n_batch, 1, a_width), BF16),
                   jax.ShapeDtypeStruct((n_batch, n_heads, HEAD_DIM), BF16),
                   jax.ShapeDtypeStruct((n_batch, 1, a_width), F32)],
        compiler_params=_params(("arbitrary",), VMEM_LIMIT),
        name="mix_sample",
    )(page_table, idx, cache_slc, cache_win, q3, o_c, gt3, slc_new.reshape(n_batch, 1, KV_COLS),
      win_new.reshape(n_batch, 1, KV_COLS), rope_s, z3, z3,
      ln_g, ln_b, w0, b0, norm_a, norm_b3)


def _rope_tables(pos):
    inv = ROPE_THETA ** (-jnp.arange(ROPE_HALF, dtype=F32) / ROPE_HALF)
    ang = pos.astype(F32)[:, None] * inv[None, :]
    cos, sin = jnp.cos(ang), jnp.sin(ang)
    n = pos.shape[0]
    zeros = lambda w: jnp.zeros((n, w), F32)
    c = jnp.concatenate([cos, cos, jnp.ones((n, HEAD_DIM - ROPE_DIM), F32)], axis=1)
    sa = jnp.concatenate([-sin, zeros(HEAD_DIM - ROPE_HALF)], axis=1)
    sb = jnp.concatenate([zeros(ROPE_HALF), sin, zeros(HEAD_DIM - ROPE_DIM)], axis=1)
    return c, sa, sb


def _block_sum_matrix(n_cmp, n_lane):
    i = jnp.arange(n_cmp)[:, None]
    j = jnp.arange(n_lane)[None, :]
    return ((i // CMP_PER_SLC == j) & (i % CMP_PER_SLC < CMP_INSIDE)).astype(BF16)


def kernel(x_prompt, x_sample, cache_cmp, cache_slc, cache_win, page_table, c_prompt, c_sample, w_ada, b_ada, norm_pre_mix, norm_post_mix, norm_pre_ffn, norm_post_ffn, w_in, ln_v_g, ln_v_b, sgu_w, sgu_b, cmp_pe_k, cmp_pe_v, cmp_w1_k, cmp_w2_k, cmp_w1_v, cmp_w2_v, out_norm_a, out_norm_b, w_out, w_up, w_down):
    batch, seq, d = x_prompt.shape
    n_dec, dec_seq, _ = x_sample.shape
    depth = w_ada.shape[0]
    a_width = d // 2
    b_width = d - a_width
    groups = a_width // HEAD_DIM
    n_heads = b_width // HEAD_DIM
    hpg = n_heads // N_KV
    n_main = 2 * a_width + b_width + 3 * KV_COLS
    n_gate = 3 * n_heads
    n_pages = page_table.shape[1]
    past = n_pages * PAGE_SIZE
    assert dec_seq == 1 and n_dec % 8 == 0 and a_width == b_width and n_gate <= LANES
    assert (2 * a_width + b_width) % (3 * KV_COLS) == 0 and seq % 1024 == 0
    assert past % SLC_BLOCK == 0 and past // SLC_BLOCK + 1 > N_SEL and cache_win.shape[2] <= past
    q_block = 2 * a_width // b_width
    kv_block = (2 * a_width + b_width) // (3 * KV_COLS)
    cmp_block = (2 * a_width + b_width) // KV_COLS
    m_p = batch * seq

    vec3 = lambda a: a.reshape(depth, 1, a.shape[-1])
    w_main = w_in
    w_gate = jnp.pad(w_in[:, :, n_main:], ((0, 0), (0, 0), (0, LANES - n_gate))).astype(BF16)
    w_out_b, w_up_b, w_down_b = w_out.astype(BF16), w_up.astype(BF16), w_down.astype(BF16)
    cmp_w1 = jnp.stack([cmp_w1_k, cmp_w1_v], axis=1).astype(BF16)
    cmp_w2 = jnp.stack([cmp_w2_k, cmp_w2_v], axis=1).astype(BF16)
    cmp_pe = jnp.stack([cmp_pe_k, cmp_pe_v], axis=1).reshape(depth, 2, CMP_STRIPS, STRIP_COLS)
    pre_mix, post_mix, pre_ffn, post_ffn = map(vec3, (norm_pre_mix, norm_post_mix, norm_pre_ffn, norm_post_ffn))
    ln_g, ln_b, norm_a, norm_b = map(vec3, (ln_v_g, ln_v_b, out_norm_a, out_norm_b))
    norm_b3 = out_norm_b.reshape(depth, n_heads, HEAD_DIM)
    sgu_bt = jnp.swapaxes(sgu_b, 1, 2)
    sgu_w0 = jnp.repeat(sgu_w[:, :, 0, 0], HEAD_DIM, axis=1).reshape(depth, 1, a_width)
    sgu_b0 = jnp.repeat(sgu_b[:, :, 0], HEAD_DIM, axis=1).reshape(depth, 1, a_width)

    mod_rows = -(-(n_dec + batch) // 8) * 8
    c_all = jnp.concatenate([c_sample, c_prompt, jnp.zeros((mod_rows - n_dec - batch, d), F32)], axis=0)
    mod = _ada(c_all, w_ada, b_ada)

    tables_p = _rope_tables(jnp.arange(seq))
    tables_s = _rope_tables(jnp.full((n_dec,), past))
    rope_s = jnp.concatenate([t[0:1] for t in tables_s] + [jnp.zeros((5, LANES), F32)], axis=0)
    n_cmp_p = seq // CMP_STRIDE
    sel_p = _block_sum_matrix(n_cmp_p, LANES)
    key_chunk = 512
    exp_p = (jnp.arange(LANES)[None, :, None]
             == (jnp.arange(seq) // SLC_BLOCK).reshape(seq // key_chunk, 1, key_chunk)).astype(BF16)
    n_cmp_s = past // CMP_STRIDE
    nb_lane = -(-(past // SLC_BLOCK + 1) // LANES) * LANES
    sel_s = _block_sum_matrix(n_cmp_s, nb_lane)

    y_p = x_prompt.reshape(m_p, d)
    y_s = x_sample.reshape(n_dec, d)
    caches_p = [jnp.zeros((depth, batch, rows, N_KV, 2, HEAD_DIM), F32) for rows in (seq, seq, min(WINDOW, seq))]
    outs = [[] for _ in range(4)]
    win_keep = min(WINDOW, seq)
    for l in range(depth):
        rows = _Rows(mod, l, d, m_p, 1024, seq, n_dec)
        z, gt = _in_proj(y_p, rows, l, pre_mix, w_main, w_gate, n_main)
        a_n = _sgu(z, l, a_width, ln_g, ln_b, sgu_w, sgu_bt, norm_a)
        *caches_p, slc_b, win_b = _kv_rope_prompt(z, kv_block, tables_p, 256, l, depth, batch, seq, win_keep,
                                                  caches_p)
        kvc = _compress_prompt(z, batch, seq, cmp_block, cmp_pe[l], cmp_w1[l], cmp_w2[l])
        b_n = _attn_prompt(z, gt, kvc, slc_b, win_b, tables_p, l, norm_b, sel_p, exp_p, batch, seq, b_width, q_block)
        y_p = _out_proj(a_n, b_n, y_p, _Rows(mod, l, d, m_p, 512, seq, n_dec), l, w_out_b, post_mix)
        y_p = _ffn(y_p, rows, l, pre_ffn, post_ffn, w_up_b, w_down_b)
        rows = _Rows(mod, l, d, n_dec, n_dec, None, 0)
        z, gt = _in_proj(y_s, rows, l, pre_mix, w_main, w_gate, n_main)
        kv_cmp, kv_slc, kv_win, _, _ = _kv_rope(z, kv_block, tables_s, n_dec)
        kvc = _compress_sample(page_table, cache_cmp, l, kv_cmp, cmp_pe[l], cmp_w1[l], cmp_w2[l])
        q3 = z[:, 2 * a_width:2 * a_width + b_width].reshape(n_dec, n_heads, HEAD_DIM)
        o_c, idx = _select_sample(q3, kvc, sel_s, past, hpg)
        a_n, b_n, v_n = _mix_sample(page_table, idx[:, :, :, 0], cache_slc, cache_win, q3, o_c,
                                    gt[:, :n_gate].reshape(n_dec, n_heads, 3), kv_slc, kv_win,
                                    rope_s, z.reshape(n_dec, 1, n_main), l, ln_g, ln_b, sgu_w0, sgu_b0,
                                    norm_a, norm_b3, past, hpg, a_width)
        y_s = _out_proj(a_n.reshape(n_dec, a_width), b_n.reshape(n_dec, b_width), y_s, rows, l, w_out_b, post_mix)
        y_s = _ffn(y_s, rows, l, pre_ffn, post_ffn, w_up_b, w_down_b)
        outs[0].append(kv_cmp.reshape(n_dec, 1, N_KV, 2, HEAD_DIM))
        outs[1].append(kv_slc.reshape(n_dec, 1, N_KV, 2, HEAD_DIM))
        outs[2].append(kv_win.reshape(n_dec, 1, N_KV, 2, HEAD_DIM))
        outs[3].append(v_n)
    return (y_p.reshape(batch, seq, d), y_s.reshape(n_dec, 1, d), *caches_p, *[jnp.stack(o) for o in outs])
```

```python
import functools

import jax
import jax.numpy as jnp
from jax import lax
from jax.experimental import pallas as pl
from jax.experimental.pallas import tpu as pltpu

F32 = jnp.float32
BF16 = jnp.bfloat16
I32 = jnp.int32

LANES = 128
HEAD_DIM = 128
N_KV = 2
PAGE_SIZE = 128
CHUNK = 128
CMP_BLOCK = 32
CMP_STRIDE = 16
SLC_BLOCK = 64
N_SEL = 16
WINDOW = 512
ROPE_DIM = HEAD_DIM // 4
ROPE_HALF = ROPE_DIM // 2
ROPE_THETA = 500000.0
EPS = 1e-6
FORCE = 1e4
SCALE = HEAD_DIM ** -0.5
EXP2_SCALE = SCALE * 1.4426950408889634
KV_COLS = N_KV * 2 * HEAD_DIM
CMP_STRIPS = CMP_BLOCK // CMP_STRIDE
STRIP_COLS = CMP_STRIDE * HEAD_DIM
CMP_PER_SLC = SLC_BLOCK // CMP_STRIDE
CMP_INSIDE = (SLC_BLOCK - CMP_BLOCK) // CMP_STRIDE + 1
VMEM_LIMIT = 56 * 2 ** 20


def _dot(a, b):
    return jnp.dot(a, b, preferred_element_type=F32)


def _dot_nt(a, b):
    return lax.dot_general(a, b, (((1,), (1,)), ((), ())), preferred_element_type=F32)


def _gelu(x):
    return 0.5 * x * (1.0 + jnp.tanh(0.7978845608028654 * (x + 0.044715 * (x * x * x))))


def _sigmoid(x):
    return 1.0 / (1.0 + jnp.exp(-x))


def _rms(x, g):
    return x * lax.rsqrt(jnp.mean(x * x, axis=-1, keepdims=True) + EPS) * g


def _softmax_neg_inf(s):
    m = jnp.max(s, axis=-1, keepdims=True)
    m = jnp.where(m > -jnp.inf, m, 0.0)
    e = jnp.exp(s - m)
    return e / jnp.maximum(jnp.sum(e, axis=-1, keepdims=True), 1e-30)


def _msoftmax(s, mask):
    return _softmax_neg_inf(jnp.where(mask, s, -jnp.inf))


def _attend(s, v):
    m = jnp.max(s, axis=-1, keepdims=True)
    m = jnp.where(m > -jnp.inf, m, 0.0)
    e = jnp.exp2(s - m)
    return _dot(e.astype(BF16), v) / jnp.maximum(jnp.sum(e, axis=-1, keepdims=True), 1e-30)


def _rope(x, c, sa, sb):
    return x * c + pltpu.roll(x, LANES - ROPE_HALF, axis=1) * sa + pltpu.roll(x, ROPE_HALF, axis=1) * sb


def _split_dot(x, m):
    hi = x.astype(BF16)
    r1 = x - hi.astype(F32)
    mid = r1.astype(BF16)
    lo = (r1 - mid.astype(F32)).astype(BF16)
    return _dot(hi, m) + _dot(mid, m) + _dot(lo, m)


def _modulated_norm(x_ref, g_ref, sc_ref, sh_ref, h_ref):
    h = _rms(x_ref[...], g_ref[...]) * (1.0 + sc_ref[...]) + sh_ref[...]
    h_ref[...] = h.astype(h_ref.dtype)


def _gated_residual(x_ref, gate_ref, g_ref, o_ref):
    o_ref[...] = x_ref[...] + gate_ref[...] * _rms(o_ref[...], g_ref[...])


def _params(sem, vmem=None):
    return pltpu.CompilerParams(dimension_semantics=sem, vmem_limit_bytes=vmem)


def _ada_kernel(c_ref, w_ref, b_ref, o_ref):
    c = c_ref[...]
    o_ref[...] = _dot((c * _sigmoid(c)).astype(BF16), w_ref[...].astype(BF16)) + b_ref[...]


def _ada(c_all, w_ada, b_ada):
    depth, d, n = w_ada.shape
    rows = c_all.shape[0]
    tn = 1024
    return pl.pallas_call(
        _ada_kernel,
        grid=(depth, n // tn),
        in_specs=[pl.BlockSpec((rows, d), lambda l, j: (0, 0)),
                  pl.BlockSpec((None, d, tn), lambda l, j: (l, 0, j)),
                  pl.BlockSpec((None, 1, tn), lambda l, j: (l, 0, j))],
        out_specs=pl.BlockSpec((None, rows, tn), lambda l, j: (l, 0, j)),
        out_shape=jax.ShapeDtypeStruct((depth, rows, n), F32),
        compiler_params=_params(("arbitrary", "arbitrary"), VMEM_LIMIT),
        name="ada",
    )(c_all, w_ada, b_ada.reshape(depth, 1, n))


class _Rows:
    def __init__(self, mod, layer, d, m, tm, rows_per_batch, mod_row0):
        self.m, self.tm, self.d = m, tm, d
        depth, r, n = mod.shape
        if rows_per_batch is None:
            self.mod = mod
            self._spec = lambda k: pl.BlockSpec((None, tm, d), lambda i, *_: (layer, mod_row0 // tm, k))
        else:
            tiles = rows_per_batch // tm
            self.mod = mod.reshape(depth, r, 1, n)
            self._spec = lambda k: pl.BlockSpec((None, None, 1, d),
                                                lambda i, *_: (layer, mod_row0 + i // tiles, 0, k))

    def mod_spec(self, k):
        return self._spec(k)


def _vec_spec(layer, n):
    return pl.BlockSpec((None, 1, n), lambda *_: (layer, 0, 0))


def _in_kernel(x_ref, g_ref, sc_ref, sh_ref, w_ref, wg_ref, z_ref, gt_ref, h_ref):
    @pl.when(pl.program_id(1) == 0)
    def _():
        _modulated_norm(x_ref, g_ref, sc_ref, sh_ref, h_ref)
        gt_ref[...] = _dot(h_ref[...].astype(BF16), wg_ref[...])

    z_ref[...] = _dot(h_ref[...].astype(BF16), w_ref[...])


def _in_proj(x, rows, layer, norm_w, w_main, w_gate, n):
    m, tm, d = rows.m, rows.tm, rows.d
    tn = 768 if n % 768 == 0 else 512
    h_dtype = BF16 if tm % 16 == 0 else F32
    return pl.pallas_call(
        _in_kernel,
        grid=(m // tm, n // tn),
        in_specs=[pl.BlockSpec((tm, d), lambda i, j: (i, 0)),
                  _vec_spec(layer, d), rows.mod_spec(1), rows.mod_spec(0),
                  pl.BlockSpec((None, d, tn), lambda i, j: (0, 0, j)),
                  pl.BlockSpec((None, d, LANES), lambda i, j: (layer, 0, 0))],
        out_specs=[pl.BlockSpec((tm, tn), lambda i, j: (i, j)),
                   pl.BlockSpec((tm, LANES), lambda i, j: (i, 0))],
        out_shape=[jax.ShapeDtypeStruct((m, n), F32), jax.ShapeDtypeStruct((m, LANES), F32)],
        scratch_shapes=[pltpu.VMEM((tm, d), h_dtype)],
        compiler_params=_params(("arbitrary", "arbitrary"), VMEM_LIMIT),
        name="in_proj",
    )(x, norm_w, rows.mod, rows.mod, w_main, w_gate)


def _layernorm(v, g, b):
    mu = jnp.mean(v, axis=-1, keepdims=True)
    vc = v - mu
    return vc * lax.rsqrt(jnp.mean(vc * vc, axis=-1, keepdims=True) + EPS) * g + b


def _sgu_kernel(u_ref, v_ref, lg_ref, lb_ref, w_ref, bt_ref, na_ref, o_ref, *, groups, chunks):
    row = lax.broadcasted_iota(I32, (CHUNK, CHUNK), 0)
    col = lax.broadcasted_iota(I32, (CHUNK, CHUNK), 1)
    ws = [jnp.where(row >= col, w_ref[g], 0.0).astype(BF16) for g in range(groups)]
    for c in range(chunks):
        rs = slice(c * CHUNK, (c + 1) * CHUNK)
        vn = _layernorm(_gelu(v_ref[rs, :]), lg_ref[...], lb_ref[...]).astype(BF16)
        u = _gelu(u_ref[rs, :])
        outs = []
        for g in range(groups):
            cs = slice(g * HEAD_DIM, (g + 1) * HEAD_DIM)
            outs.append(u[:, cs] * (_dot(ws[g], vn[:, cs]) + bt_ref[:, g:g + 1]))
        o_ref[rs, :] = _rms(jnp.concatenate(outs, axis=1), na_ref[...]).astype(BF16)


def _sgu(z, layer, a_width, ln_g, ln_b, sgu_w, sgu_bt, norm_a):
    m = z.shape[0]
    groups = a_width // HEAD_DIM
    chunks = 2
    tr = chunks * CHUNK
    return pl.pallas_call(
        functools.partial(_sgu_kernel, groups=groups, chunks=chunks),
        grid=(m // tr,),
        in_specs=[pl.BlockSpec((tr, a_width), lambda i: (i, 0)),
                  pl.BlockSpec((tr, a_width), lambda i: (i, 1)),
                  _vec_spec(layer, a_width), _vec_spec(layer, a_width),
                  pl.BlockSpec((None, groups, CHUNK, CHUNK), lambda i: (layer, 0, 0, 0)),
                  pl.BlockSpec((None, CHUNK, groups), lambda i: (layer, 0, 0)),
                  _vec_spec(layer, a_width)],
        out_specs=pl.BlockSpec((tr, a_width), lambda i: (i, 0)),
        out_shape=jax.ShapeDtypeStruct((m, a_width), BF16),
        compiler_params=_params(("arbitrary",), VMEM_LIMIT),
        name="sgu",
    )(z, z, ln_g, ln_b, sgu_w, sgu_bt, norm_a)


def _kv_kernel(kv_ref, rc_ref, rsa_ref, rsb_ref, cmp_ref, slc_ref, win_ref, slcb_ref, winb_ref):
    cmp_ref[...] = kv_ref[:, 0:KV_COLS]
    rc, rsa, rsb = rc_ref[...], rsa_ref[...], rsb_ref[...]
    for t, (o_ref, ob_ref) in enumerate(((slc_ref, slcb_ref), (win_ref, winb_ref))):
        for g in range(N_KV):
            k0 = (t + 1) * KV_COLS + g * 2 * HEAD_DIM
            k = _rope(kv_ref[:, k0:k0 + HEAD_DIM], rc, rsa, rsb)
            v = kv_ref[:, k0 + HEAD_DIM:k0 + 2 * HEAD_DIM]
            o0 = g * 2 * HEAD_DIM
            o_ref[:, o0:o0 + HEAD_DIM] = k
            o_ref[:, o0 + HEAD_DIM:o0 + 2 * HEAD_DIM] = v
            ob_ref[:, o0:o0 + HEAD_DIM] = k.astype(BF16)
            ob_ref[:, o0 + HEAD_DIM:o0 + 2 * HEAD_DIM] = v.astype(BF16)


def _kv_cache_kernel(kv_ref, rc_ref, rsa_ref, rsb_ref, *refs):
    cmp_ref, slc_ref, win_ref, slcb_ref, winb_ref = refs[-5:]
    rc, rsa, rsb = rc_ref[...], rsa_ref[...], rsb_ref[...]
    for g in range(N_KV):
        for t in range(2):
            c0 = (g * 2 + t) * HEAD_DIM
            cmp_ref[:, g, t, :] = kv_ref[:, c0:c0 + HEAD_DIM]
    for j, (o_ref, ob_ref) in enumerate(((slc_ref, slcb_ref), (win_ref, winb_ref))):
        for g in range(N_KV):
            k0 = (j + 1) * KV_COLS + g * 2 * HEAD_DIM
            k = _rope(kv_ref[:, k0:k0 + HEAD_DIM], rc, rsa, rsb)
            v = kv_ref[:, k0 + HEAD_DIM:k0 + 2 * HEAD_DIM]
            o0 = g * 2 * HEAD_DIM
            o_ref[:, g, 0, :] = k
            o_ref[:, g, 1, :] = v
            ob_ref[:, o0:o0 + HEAD_DIM] = k.astype(BF16)
            ob_ref[:, o0 + HEAD_DIM:o0 + 2 * HEAD_DIM] = v.astype(BF16)


def _kv_rope_prompt(z, kv_block, tables, tr, layer, depth, batch, seq, win_keep, prev):
    m = z.shape[0]
    nt = seq // tr
    first = (seq - win_keep) // tr
    tspec = pl.BlockSpec((tr, LANES), lambda i: (i % nt, 0))
    bspec = pl.BlockSpec((tr, KV_COLS), lambda i: (i, 0))
    cache_block = (None, None, tr, N_KV, 2, HEAD_DIM)
    cspec = pl.BlockSpec(cache_block, lambda i: (layer, i // nt, i % nt, 0, 0, 0))
    wspec = pl.BlockSpec(cache_block, lambda i: (layer, i // nt, jnp.maximum(i % nt - first, 0), 0, 0, 0))
    cache_shape = lambda rows: jax.ShapeDtypeStruct((depth, batch, rows, N_KV, 2, HEAD_DIM), F32)
    return pl.pallas_call(
        _kv_cache_kernel,
        grid=(m // tr,),
        in_specs=[pl.BlockSpec((tr, 3 * KV_COLS), lambda i: (i, kv_block)), tspec, tspec, tspec]
        + [pl.BlockSpec(memory_space=pl.ANY)] * len(prev),
        out_specs=[cspec, cspec, wspec, bspec, bspec],
        out_shape=[cache_shape(seq), cache_shape(seq), cache_shape(win_keep)]
        + [jax.ShapeDtypeStruct((m, KV_COLS), BF16)] * 2,
        input_output_aliases={4 + i: i for i in range(len(prev))},
        compiler_params=_params(("arbitrary",), VMEM_LIMIT),
        name="kv_rope_prompt",
    )(z, *tables, *prev)


def _kv_rope(z, kv_block, tables, tr):
    m = z.shape[0]
    t_rows = tables[0].shape[0]
    nt = t_rows // tr
    tspec = pl.BlockSpec((tr, LANES), lambda i: (i % nt, 0))
    ospec = pl.BlockSpec((tr, KV_COLS), lambda i: (i, 0))
    return pl.pallas_call(
        _kv_kernel,
        grid=(m // tr,),
        in_specs=[pl.BlockSpec((tr, 3 * KV_COLS), lambda i: (i, kv_block)), tspec, tspec, tspec],
        out_specs=[ospec] * 5,
        out_shape=[jax.ShapeDtypeStruct((m, KV_COLS), F32)] * 3 + [jax.ShapeDtypeStruct((m, KV_COLS), BF16)] * 2,
        compiler_params=_params(("arbitrary",), VMEM_LIMIT),
        name="kv_rope",
    )(z, *tables)


def _compress_strips(xf, pe_ref, w1_ref, sel):
    p = _dot((xf + pe_ref[sel, 0:1, :]).astype(BF16), w1_ref[sel, 0:STRIP_COLS, :])
    q = _dot((xf + pe_ref[sel, 1:2, :]).astype(BF16), w1_ref[sel, STRIP_COLS:2 * STRIP_COLS, :])
    return p, q


def _cmp_kernel(x_ref, pe_ref, w1_ref, w2_ref, o_ref, *, n):
    sel = pl.program_id(1) % 2
    xf = jnp.concatenate([x_ref[pl.ds(r, n, stride=CMP_STRIDE), :] for r in range(CMP_STRIDE)], axis=1)
    p, q = _compress_strips(xf, pe_ref, w1_ref, sel)
    hid = _gelu(p + pltpu.roll(q, n - 1, axis=0))
    o_ref[...] = _dot(hid.astype(BF16), w2_ref[sel]).astype(BF16)


def _compress_prompt(z, batch, seq, cmp_block, pe2, w1, w2):
    n = seq // CMP_STRIDE
    hidden = w1.shape[-1]
    heads = 2 * N_KV
    return pl.pallas_call(
        functools.partial(_cmp_kernel, n=n),
        grid=(batch, heads),
        in_specs=[pl.BlockSpec((seq, HEAD_DIM), lambda b, h: (b, cmp_block * heads + h)),
                  pl.BlockSpec((2, 2, STRIP_COLS), lambda b, h: (0, 0, 0)),
                  pl.BlockSpec((2, 2 * STRIP_COLS, hidden), lambda b, h: (0, 0, 0)),
                  pl.BlockSpec((2, hidden, HEAD_DIM), lambda b, h: (0, 0, 0))],
        out_specs=pl.BlockSpec((None, None, n, HEAD_DIM), lambda b, h: (b, h, 0, 0)),
        out_shape=jax.ShapeDtypeStruct((batch, heads, n, HEAD_DIM), BF16),
        compiler_params=_params(("arbitrary", "arbitrary"), VMEM_LIMIT),
        name="compress_prompt",
    )(z, pe2, w1, w2)


def _attn_kernel(q_ref, gt_ref, cmp_ref, slc_ref, win_ref, rc_ref, rsa_ref, rsb_ref, nb_ref, sel_ref, exp_ref,
                 o_ref, acc_ref, *, tq, seq, hpg, n_cmp, kc):
    q0 = pl.program_id(1) * tq
    n_chunks = (q0 + tq + kc - 1) // kc
    kiota = lax.broadcasted_iota(I32, (1, kc), 1)
    pos = q0 + lax.broadcasted_iota(I32, (tq, 1), 0)
    posh = jnp.concatenate([pos] * hpg, axis=0)
    rc = jnp.concatenate([rc_ref[...]] * hpg, axis=0)
    rsa = jnp.concatenate([rsa_ref[...]] * hpg, axis=0)
    rsb = jnp.concatenate([rsb_ref[...]] * hpg, axis=0)
    sig = _sigmoid(gt_ref[...])
    clane = lax.broadcasted_iota(I32, (1, n_cmp), 1)
    n_blk = seq // SLC_BLOCK
    n_sel = min(N_SEL, n_blk)
    nb8 = -(-n_blk // 8) * 8
    brow = lax.broadcasted_iota(I32, (nb8, 1), 0)
    cur_t = (q0 + lax.broadcasted_iota(I32, (1, tq), 1)) // SLC_BLOCK
    forced_t = (brow == 0) | (brow == cur_t) | (brow == cur_t - 1)
    brow_q = lax.broadcasted_iota(I32, (nb8, tq), 0)
    wk = WINDOW + tq
    wstart = pl.multiple_of(jnp.maximum(q0 - WINDOW, 0), LANES)
    wpos = wstart + lax.broadcasted_iota(I32, (1, wk), 1)
    wdiff = posh - wpos
    wmask = (wdiff >= 0) & (wdiff < WINDOW)
    cmask = (clane * CMP_STRIDE + (CMP_BLOCK - 1)) <= posh
    for g in range(N_KV):
        heads = [g * hpg + h for h in range(hpg)]
        qc = jnp.concatenate([q_ref[:, hd * HEAD_DIM:(hd + 1) * HEAD_DIM] for hd in heads], axis=0)
        qrb = (_rope(qc, rc, rsa, rsb) * EXP2_SCALE).astype(BF16)
        k0 = g * 2 * HEAD_DIM
        v0 = k0 + HEAD_DIM
        p = _msoftmax(_dot_nt(qc.astype(BF16), cmp_ref[2 * g]) * SCALE, cmask)
        o_c = _dot(p.astype(BF16), cmp_ref[2 * g + 1])
        imp = p[0:tq]
        for h in range(1, hpg):
            imp = imp + p[h * tq:(h + 1) * tq]
        impb = _split_dot(imp, sel_ref[...])
        score = jnp.where(brow <= cur_t, jnp.where(forced_t, FORCE, impb.T[0:nb8]), -FORCE)
        rank = jnp.zeros((nb8, tq), F32)
        for i in range(n_blk):
            ci = score[i:i + 1, :]
            tie = jnp.where(brow_q > i, 1.0, 0.0)
            rank = rank + jnp.where(ci > score, 1.0, jnp.where(ci == score, tie, 0.0))
        chosen_t = jnp.where((rank < float(n_sel)) & (brow <= cur_t), 1.0, 0.0)
        chosen_b = jnp.concatenate([chosen_t, jnp.zeros((LANES - nb8, tq), F32)], axis=0).T.astype(BF16)
        rows = hpg * tq

        def chunk(c, carry, diagonal):
            m, l, acc = carry
            r0 = pl.multiple_of(c * kc, kc)
            open_key = _dot(chosen_b, exp_ref[c]) > 0.5
            if diagonal:
                open_key = open_key & ((r0 + kiota) <= pos)
            bias = jnp.where(open_key, 0.0, -jnp.inf)
            s = _dot_nt(qrb, slc_ref[pl.ds(r0, kc), k0:k0 + HEAD_DIM]).reshape(hpg, tq, kc) + bias[None]
            m_new = jnp.maximum(m, jnp.max(s, axis=-1, keepdims=True))
            m_ref = jnp.where(m_new > -jnp.inf, m_new, 0.0)
            alpha = jnp.exp2(m - m_ref)
            e = jnp.exp2(s - m_ref)
            l = alpha * l + jnp.sum(e, axis=-1, keepdims=True)
            pv = _dot(e.reshape(rows, kc).astype(BF16), slc_ref[pl.ds(r0, kc), v0:v0 + HEAD_DIM])
            return m_new, l, alpha.reshape(rows, 1) * acc + pv

        init = (jnp.full((hpg, tq, 1), -jnp.inf, F32), jnp.zeros((hpg, tq, 1), F32),
                jnp.zeros((rows, HEAD_DIM), F32))
        carry = lax.fori_loop(0, n_chunks - 1, functools.partial(chunk, diagonal=False), init)
        _, l, acc = chunk(n_chunks - 1, carry, diagonal=True)
        o_s = acc / jnp.maximum(l.reshape(rows, 1), 1e-30)
        s = jnp.where(wmask, _dot_nt(qrb, win_ref[pl.ds(wstart, wk), k0:k0 + HEAD_DIM]), -jnp.inf)
        o_w = _attend(s, win_ref[pl.ds(wstart, wk), v0:v0 + HEAD_DIM])
        for h, hd in enumerate(heads):
            rs = slice(h * tq, (h + 1) * tq)
            acc_ref[:, hd * HEAD_DIM:(hd + 1) * HEAD_DIM] = (
                sig[:, 3 * hd:3 * hd + 1] * o_c[rs] + sig[:, 3 * hd + 1:3 * hd + 2] * o_s[rs]
                + sig[:, 3 * hd + 2:3 * hd + 3] * o_w[rs])
    o_ref[...] = _rms(acc_ref[...], nb_ref[...]).astype(BF16)


def _attn_prompt(z, gt, kvc, slc_b, win_b, tables, layer, norm_b, sel_m, exp_m, batch, seq, b_width, q_block):
    tq = 128
    nq = seq // tq
    hpg = b_width // HEAD_DIM // N_KV
    n_cmp = kvc.shape[2]
    kc = exp_m.shape[2]
    tspec = pl.BlockSpec((tq, LANES), lambda b, i: (i, 0))
    kvspec = pl.BlockSpec((seq, KV_COLS), lambda b, i: (b, 0))
    return pl.pallas_call(
        functools.partial(_attn_kernel, tq=tq, seq=seq, hpg=hpg, n_cmp=n_cmp, kc=kc),
        grid=(batch, nq),
        in_specs=[pl.BlockSpec((tq, b_width), lambda b, i: (b * nq + i, q_block)),
                  pl.BlockSpec((tq, LANES), lambda b, i: (b * nq + i, 0)),
                  pl.BlockSpec((None, 2 * N_KV, n_cmp, HEAD_DIM), lambda b, i: (b, 0, 0, 0)),
                  kvspec, kvspec, tspec, tspec, tspec,
                  pl.BlockSpec((None, 1, b_width), lambda b, i: (layer, 0, 0)),
                  pl.BlockSpec(sel_m.shape, lambda b, i: (0, 0)),
                  pl.BlockSpec(exp_m.shape, lambda b, i: (0, 0, 0))],
        out_specs=pl.BlockSpec((tq, b_width), lambda b, i: (b * nq + i, 0)),
        out_shape=jax.ShapeDtypeStruct((batch * seq, b_width), BF16),
        scratch_shapes=[pltpu.VMEM((tq, b_width), F32)],
        compiler_params=_params(("arbitrary", "arbitrary"), VMEM_LIMIT),
        name="attn_prompt",
    )(z, gt, kvc, slc_b, win_b, *tables, norm_b, sel_m, exp_m)


def _out_kernel(a_ref, b_ref, w_ref, x_ref, gate_ref, g_ref, o_ref, *, a_width):
    o_ref[...] = _dot(a_ref[...], w_ref[0:a_width, :]) + _dot(b_ref[...], w_ref[a_width:, :])
    _gated_residual(x_ref, gate_ref, g_ref, o_ref)


def _out_proj(a_n, b_n, x, rows, layer, w_out, norm_w):
    m, tm, d = rows.m, rows.tm, rows.d
    a_width = a_n.shape[1]
    b_width = b_n.shape[1]
    return pl.pallas_call(
        functools.partial(_out_kernel, a_width=a_width),
        grid=(m // tm,),
        in_specs=[pl.BlockSpec((tm, a_width), lambda i: (i, 0)),
                  pl.BlockSpec((tm, b_width), lambda i: (i, 0)),
                  pl.BlockSpec((None, a_width + b_width, d), lambda i: (0, 0, 0)),
                  pl.BlockSpec((tm, d), lambda i: (i, 0)),
                  rows.mod_spec(2), _vec_spec(layer, d)],
        out_specs=pl.BlockSpec((tm, d), lambda i: (i, 0)),
        out_shape=jax.ShapeDtypeStruct((m, d), F32),
        compiler_params=_params(("arbitrary",), VMEM_LIMIT),
        name="out_proj",
    )(a_n, b_n, w_out, x, rows.mod, norm_w)


def _ffn_kernel(x_ref, g1_ref, sc_ref, sh_ref, wu_ref, wd_ref, gate_ref, g2_ref, *refs, n_cast):
    cast_in, o_ref, cast_out, h_ref = refs[:n_cast], refs[n_cast], refs[n_cast + 1:2 * n_cast + 1], refs[-1]
    k = pl.program_id(1)
    for src, dst in zip(cast_in, cast_out):
        dst[...] = src[...].astype(BF16)

    @pl.when(k == 0)
    def _():
        _modulated_norm(x_ref, g1_ref, sc_ref, sh_ref, h_ref)
        o_ref[...] = jnp.zeros_like(o_ref)

    up = jnp.maximum(_dot(h_ref[...].astype(BF16), wu_ref[...]), 0.0)
    o_ref[...] += _dot((up * up).astype(BF16), wd_ref[...])

    @pl.when(k == pl.num_programs(1) - 1)
    def _():
        _gated_residual(x_ref, gate_ref, g2_ref, o_ref)


def _ffn(x, rows, layer, norm_pre, norm_post, w_up, w_down, cast_next=(), next_layer=0):
    m, tm, d = rows.m, rows.tm, rows.d
    d_ff = w_up.shape[-1]
    tf = 512
    ni, nk = m // tm, d_ff // tf
    h_dtype = BF16 if tm % 16 == 0 else F32
    x_mode = dict(pipeline_mode=pl.Buffered(1)) if tm * d * 4 >= 8 * 2 ** 20 else {}
    cast_in, cast_out, cast_shape = [], [], []
    for w in cast_next:
        r, c = w.shape[1:]
        band = r // (ni * nk)
        assert band * ni * nk == r and band % 16 == 0
        cast_in.append(pl.BlockSpec((None, band, c), lambda i, k: (next_layer, i * nk + k, 0)))
        cast_out.append(pl.BlockSpec((band, c), lambda i, k: (i * nk + k, 0)))
        cast_shape.append(jax.ShapeDtypeStruct((r, c), BF16))
    y, *casts = pl.pallas_call(
        functools.partial(_ffn_kernel, n_cast=len(cast_next)),
        grid=(ni, nk),
        in_specs=[pl.BlockSpec((tm, d), lambda i, k: (i, 0), **x_mode),
                  _vec_spec(layer, d), rows.mod_spec(4), rows.mod_spec(3),
                  pl.BlockSpec((None, d, tf), lambda i, k: (0, 0, k)),
                  pl.BlockSpec((None, tf, d), lambda i, k: (0, k, 0)),
                  rows.mod_spec(5), _vec_spec(layer, d)] + cast_in,
        out_specs=[pl.BlockSpec((tm, d), lambda i, k: (i, 0))] + cast_out,
        out_shape=[jax.ShapeDtypeStruct((m, d), F32)] + cast_shape,
        scratch_shapes=[pltpu.VMEM((tm, d), h_dtype)],
        compiler_params=_params(("arbitrary", "arbitrary"), VMEM_LIMIT),
        name="ffn",
    )(x, norm_pre, rows.mod, rows.mod, w_up, w_down, rows.mod, norm_post, *cast_next)
    return y, casts


def _scmp_kernel(pt_ref, cache_ref, new_ref, pe_ref, w1_ref, w2_ref, o_ref, buf, sem, p_s, q_s,
                 *, layer, pages, n_chunk, n_batch):
    b = pl.program_id(0)
    c = pl.program_id(1)
    step = b * n_chunk + c
    slot = step % 2
    strips = pages * (PAGE_SIZE // CMP_STRIDE)
    n_past = n_chunk * strips

    def copies(bb, cc, sl):
        return [pltpu.make_async_copy(cache_ref.at[layer, pt_ref[bb, cc * pages + p], :, gs // 2, gs % 2, :],
                                      buf.at[sl, gs, p], sem.at[sl])
                for p in range(pages) for gs in range(2 * N_KV)]

    @pl.when(step == 0)
    def _():
        for cp in copies(0, 0, 0):
            cp.start()

    @pl.when(step + 1 < n_batch * n_chunk)
    def _():
        nxt = step + 1
        for cp in copies(nxt // n_chunk, nxt % n_chunk, 1 - slot):
            cp.start()

    for cp in copies(b, c, slot):
        cp.wait()

    cur = buf.at[slot]
    row0 = pl.multiple_of(c * strips, strips)
    for gs in range(2 * N_KV):
        sel = gs % 2
        xf = jnp.concatenate(
            [cur[gs, :, pl.ds(r, PAGE_SIZE // CMP_STRIDE, stride=CMP_STRIDE), :].reshape(strips, HEAD_DIM)
             for r in range(CMP_STRIDE)], axis=1)
        p, q = _compress_strips(xf, pe_ref, w1_ref, sel)
        p_s[gs, pl.ds(row0, strips), :] = p
        q_s[gs, pl.ds(row0, strips), :] = q

    @pl.when(c == n_chunk - 1)
    def _():
        for gs in range(2 * N_KV):
            sel = gs % 2
            new = new_ref[:, gs * HEAD_DIM:(gs + 1) * HEAD_DIM]
            xn = jnp.concatenate([new, jnp.zeros((1, STRIP_COLS - HEAD_DIM), F32)], axis=1) + pe_ref[sel, 1:2, :]
            xn = jnp.broadcast_to(xn, (8, STRIP_COLS)).astype(BF16)
            q_s[gs, n_past:n_past + 8, :] = _dot(xn, w1_ref[sel, STRIP_COLS:2 * STRIP_COLS, :])
            hid = _gelu(p_s[gs] + q_s[gs, 1:n_past + 1, :])
            o_ref[gs] = _dot(hid.astype(BF16), w2_ref[sel]).astype(BF16)


def _compress_sample(page_table, cache, layer, new_cmp, pe2, w1, w2):
    n_batch, n_pages = page_table.shape
    pages = 32 if n_pages % 32 == 0 else n_pages
    n_chunk = n_pages // pages
    n_past = n_pages * (PAGE_SIZE // CMP_STRIDE)
    hidden = w1.shape[-1]
    grid_spec = pltpu.PrefetchScalarGridSpec(
        num_scalar_prefetch=1,
        grid=(n_batch, n_chunk),
        in_specs=[pl.BlockSpec(memory_space=pl.ANY),
                  pl.BlockSpec((None, 1, KV_COLS), lambda b, c, pt: (b, 0, 0)),
                  pl.BlockSpec((2, 2, STRIP_COLS), lambda b, c, pt: (0, 0, 0)),
                  pl.BlockSpec((2, 2 * STRIP_COLS, hidden), lambda b, c, pt: (0, 0, 0)),
                  pl.BlockSpec((2, hidden, HEAD_DIM), lambda b, c, pt: (0, 0, 0))],
        out_specs=pl.BlockSpec((None, 2 * N_KV, n_past, HEAD_DIM), lambda b, c, pt: (b, 0, 0, 0)),
        scratch_shapes=[pltpu.VMEM((2, 2 * N_KV, pages, PAGE_SIZE, HEAD_DIM), F32),
                        pltpu.SemaphoreType.DMA((2,)),
                        pltpu.VMEM((2 * N_KV, n_past, hidden), F32),
                        pltpu.VMEM((2 * N_KV, n_past + 8, hidden), F32)])
    return pl.pallas_call(
        functools.partial(_scmp_kernel, layer=layer, pages=pages, n_chunk=n_chunk, n_batch=n_batch),
        grid_spec=grid_spec,
        out_shape=jax.ShapeDtypeStruct((n_batch, 2 * N_KV, n_past, HEAD_DIM), BF16),
        compiler_params=_params(("arbitrary", "arbitrary"), VMEM_LIMIT),
        name="compress_sample",
    )(page_table, cache, new_cmp.reshape(n_batch, 1, KV_COLS), pe2, w1, w2)


def _ssel_kernel(q_ref, kvc_ref, sel_ref, oc_ref, idx_ref, *, pos, hpg, n_cmp, n_lane):
    qb = q_ref[...].astype(BF16)
    n_heads = qb.shape[0]
    head_row = lax.broadcasted_iota(I32, (n_heads, 1), 0)
    clane = lax.broadcasted_iota(I32, (1, n_cmp), 1)
    cmask = (clane * CMP_STRIDE + (CMP_BLOCK - 1)) <= pos
    lane = lax.broadcasted_iota(I32, (1, n_lane), 1)
    cur = pos // SLC_BLOCK
    forced = (lane == 0) | (lane == cur) | (lane == cur - 1)
    ri = lax.broadcasted_iota(I32, (n_lane, n_lane), 0)
    ci = lax.broadcasted_iota(I32, (n_lane, n_lane), 1)
    slot_id = lax.broadcasted_iota(I32, (N_SEL, 1), 0)
    o_c = jnp.zeros((n_heads, HEAD_DIM), F32)
    for g in range(N_KV):
        mine = (head_row // hpg) == g
        p = _msoftmax(_dot_nt(qb, kvc_ref[2 * g]) * SCALE, cmask)
        o_c = jnp.where(mine, _dot(p.astype(BF16), kvc_ref[2 * g + 1]), o_c)
        imp = jnp.sum(jnp.where(mine, p, 0.0), axis=0, keepdims=True)
        impb = _split_dot(jnp.broadcast_to(imp, (8, n_cmp)), sel_ref[...])[0:1]
        score = jnp.where(lane <= cur, jnp.where(forced, FORCE, impb), -FORCE)
        srow = jnp.broadcast_to(score, (n_lane, n_lane))
        scol = srow.T
        ahead = (scol > srow) | ((scol == srow) & (ri < ci))
        rank = jnp.sum(jnp.where(ahead, 1.0, 0.0), axis=0, keepdims=True)
        chosen = jnp.where((rank < float(N_SEL)) & (lane <= cur), 1.0, 0.0)
        ccol = jnp.broadcast_to(chosen, (n_lane, n_lane)).T
        before = jnp.sum(jnp.where(ri < ci, ccol, 0.0), axis=0, keepdims=True)
        hit = (chosen > 0.5) & (before == slot_id.astype(F32))
        idx = jnp.sum(jnp.where(hit, lane.astype(F32), 0.0), axis=1, keepdims=True)
        idx_ref[g] = jnp.broadcast_to(idx, (N_SEL, LANES)).astype(I32)
    oc_ref[...] = o_c


def _select_sample(q3, kvc, sel_m, pos, hpg):
    n_batch, n_heads, _ = q3.shape
    n_cmp = kvc.shape[2]
    n_lane = sel_m.shape[1]
    return pl.pallas_call(
        functools.partial(_ssel_kernel, pos=pos, hpg=hpg, n_cmp=n_cmp, n_lane=n_lane),
        grid=(n_batch,),
        in_specs=[pl.BlockSpec((None, n_heads, HEAD_DIM), lambda b: (b, 0, 0)),
                  pl.BlockSpec((None, 2 * N_KV, n_cmp, HEAD_DIM), lambda b: (b, 0, 0, 0)),
                  pl.BlockSpec(sel_m.shape, lambda b: (0, 0))],
        out_specs=[pl.BlockSpec((None, n_heads, HEAD_DIM), lambda b: (b, 0, 0)),
                   pl.BlockSpec((None, N_KV, N_SEL, LANES), lambda b: (b, 0, 0, 0))],
        out_shape=[jax.ShapeDtypeStruct((n_batch, n_heads, HEAD_DIM), F32),
                   jax.ShapeDtypeStruct((n_batch, N_KV, N_SEL, LANES), I32)],
        compiler_params=_params(("arbitrary",), VMEM_LIMIT),
        name="select_sample",
    )(q3, kvc, sel_m)


def _smix_kernel(pt_ref, idx_ref, cache_ref, wcache_ref, q_ref, oc_ref, gt_ref, slc_ref, win_ref, rope_ref,
                 u_ref, v_ref, lg_ref, lb_ref, w0_ref, b0_ref, na_ref, nb_ref,
                 a_ref, b_ref, vn_ref, buf, wbuf, sem, *, layer, pos, hpg, nb_past):
    b = pl.program_id(0)
    halves = PAGE_SIZE // SLC_BLOCK

    def copies():
        out = []
        for g in range(N_KV):
            for k in range(N_SEL):
                jp = jnp.minimum(idx_ref[b, g, k], nb_past - 1)
                phys = pt_ref[b, jp // halves]
                rows = pl.ds(pl.multiple_of((jp % halves) * SLC_BLOCK, SLC_BLOCK), SLC_BLOCK)
                for t in range(2):
                    out.append(pltpu.make_async_copy(cache_ref.at[layer, phys, rows, g, t, :], buf.at[t, g, k],
                                                     sem.at[0]))
            for t in range(2):
                out.append(pltpu.make_async_copy(wcache_ref.at[layer, b, :, g, t, :], wbuf.at[g, t], sem.at[1]))
        return out

    for cp in copies():
        cp.start()

    vn = _layernorm(_gelu(v_ref[...]), lg_ref[...], lb_ref[...])
    vn_ref[...] = vn
    a_out = _gelu(u_ref[...]) * (w0_ref[...] * vn + b0_ref[...])
    a_ref[...] = _rms(a_out, na_ref[...]).astype(BF16)

    q = q_ref[...]
    n_heads = q.shape[0]
    head_row = lax.broadcasted_iota(I32, (n_heads, 1), 0)
    qrb = _rope(q, rope_ref[0:1, :], rope_ref[1:2, :], rope_ref[2:3, :]).astype(BF16)
    sig = _sigmoid(gt_ref[...])
    n_keys = N_SEL * SLC_BLOCK
    klane = lax.broadcasted_iota(I32, (1, n_keys), 1)
    wb = wbuf.shape[2]
    wdiff = pos - (pos - wb + lax.broadcasted_iota(I32, (1, wb), 1))
    wmask = (wdiff >= 0) & (wdiff < WINDOW)

    for cp in copies():
        cp.wait()

    o_s = jnp.zeros((n_heads, HEAD_DIM), F32)
    o_w = jnp.zeros((n_heads, HEAD_DIM), F32)
    for g in range(N_KV):
        mine = (head_row // hpg) == g
        k0 = g * 2 * HEAD_DIM
        v0 = k0 + HEAD_DIM
        kmask = jnp.zeros((1, n_keys), F32)
        has_new = jnp.zeros((1, 1), F32)
        for k in range(N_SEL):
            blk = idx_ref[b, g, k]
            in_k = (klane // SLC_BLOCK) == k
            kmask = jnp.where(in_k & (blk < nb_past), 1.0, kmask)
            has_new = jnp.where(blk >= nb_past, 1.0, has_new)
        ks = buf[0, g].reshape(n_keys, HEAD_DIM).astype(BF16)
        vs = buf[1, g].reshape(n_keys, HEAD_DIM).astype(BF16)
        s = jnp.where(kmask > 0.5, _dot_nt(qrb, ks) * SCALE, -jnp.inf)
        k_new = slc_ref[:, k0:k0 + HEAD_DIM].astype(BF16)
        v_new = slc_ref[:, v0:v0 + HEAD_DIM].astype(BF16)
        s_new = jnp.sum(qrb.astype(F32) * k_new.astype(F32), axis=-1, keepdims=True) * SCALE
        s_new = jnp.where(has_new > 0.5, s_new, -jnp.inf)
        m = jnp.maximum(jnp.max(s, axis=-1, keepdims=True), s_new)
        m = jnp.where(m > -jnp.inf, m, 0.0)
        e = jnp.exp(s - m)
        e_new = jnp.exp(s_new - m)
        den = jnp.maximum(jnp.sum(e, axis=-1, keepdims=True) + e_new, 1e-30)
        og = _dot((e / den).astype(BF16), vs) + (e_new / den).astype(BF16).astype(F32) * v_new.astype(F32)
        o_s = jnp.where(mine, og, o_s)
        s = jnp.where(wmask, _dot_nt(qrb, wbuf[g, 0].astype(BF16)) * SCALE, -jnp.inf)
        k_new = win_ref[:, k0:k0 + HEAD_DIM].astype(BF16)
        v_new = win_ref[:, v0:v0 + HEAD_DIM].astype(BF16)
        s_new = jnp.sum(qrb.astype(F32) * k_new.astype(F32), axis=-1, keepdims=True) * SCALE
        m = jnp.maximum(jnp.max(s, axis=-1, keepdims=True), s_new)
        e = jnp.exp(s - m)
        e_new = jnp.exp(s_new - m)
        den = jnp.maximum(jnp.sum(e, axis=-1, keepdims=True) + e_new, 1e-30)
        og = (_dot((e / den).astype(BF16), wbuf[g, 1].astype(BF16))
              + (e_new / den).astype(BF16).astype(F32) * v_new.astype(F32))
        o_w = jnp.where(mine, og, o_w)
    b_out = sig[:, 0:1] * oc_ref[...] + sig[:, 1:2] * o_s + sig[:, 2:3] * o_w
    ms = jnp.sum(jnp.sum(b_out * b_out, axis=-1, keepdims=True), axis=0, keepdims=True) / float(n_heads * HEAD_DIM)
    b_ref[...] = (b_out * lax.rsqrt(ms + EPS) * nb_ref[...]).astype(BF16)


def _mix_sample(page_table, idx, cache_slc, cache_win, q3, o_c, gt3, slc_new, win_new, rope_s, z3, layer,
                ln_g, ln_b, w0, b0, norm_a, norm_b3, pos, hpg, a_width):
    n_batch, n_heads, _ = q3.shape
    wb = cache_win.shape[2]
    nb_past = page_table.shape[1] * (PAGE_SIZE // SLC_BLOCK)
    head_spec = pl.BlockSpec((None, n_heads, HEAD_DIM), lambda b, *_: (b, 0, 0))
    new_spec = pl.BlockSpec((None, 1, KV_COLS), lambda b, *_: (b, 0, 0))
    row_spec = lambda blk: pl.BlockSpec((None, 1, a_width), lambda b, *_: (b, 0, blk))
    vec = lambda: pl.BlockSpec((None, 1, a_width), lambda b, *_: (layer, 0, 0))
    grid_spec = pltpu.PrefetchScalarGridSpec(
        num_scalar_prefetch=2,
        grid=(n_batch,),
        in_specs=[pl.BlockSpec(memory_space=pl.ANY), pl.BlockSpec(memory_space=pl.ANY), head_spec, head_spec,
                  pl.BlockSpec((None, n_heads, 3), lambda b, *_: (b, 0, 0)),
                  new_spec, new_spec,
                  pl.BlockSpec(rope_s.shape, lambda b, *_: (0, 0)),
                  row_spec(0), row_spec(1), vec(), vec(), vec(), vec(), vec(),
                  pl.BlockSpec((None, n_heads, HEAD_DIM), lambda b, *_: (layer, 0, 0))],
        out_specs=[pl.BlockSpec((None, 1, a_width), lambda b, *_: (b, 0, 0)),
                   head_spec,
                   pl.BlockSpec((None, 1, a_width), lambda b, *_: (b, 0, 0))],
        scratch_shapes=[pltpu.VMEM((2, N_KV, N_SEL, SLC_BLOCK, HEAD_DIM), F32),
                        pltpu.VMEM((N_KV, 2, wb, HEAD_DIM), F32),
                        pltpu.SemaphoreType.DMA((2,))])
    return pl.pallas_call(
        functools.partial(_smix_kernel, layer=layer, pos=pos, hpg=hpg, nb_past=nb_past),
        grid_spec=grid_spec,
        out_shape=[jax.ShapeDtypeStruct((n_batch, 1, a_width), BF16),
                   jax.ShapeDtypeStruct((n_batch, n_heads, HEAD_DIM), BF16),
                   jax.ShapeDtypeStruct((n_batch, 1, a_width), F32)],
        compiler_params=_params(("arbitrary",), VMEM_LIMIT),
        name="mix_sample",
    )(page_table, idx, cache_slc, cache_win, q3, o_c, gt3, slc_new.reshape(n_batch, 1, KV_COLS),
      win_new.reshape(n_batch, 1, KV_COLS), rope_s, z3, z3,
      ln_g, ln_b, w0, b0, norm_a, norm_b3)


def _rope_tables(pos):
    inv = ROPE_THETA ** (-jnp.arange(ROPE_HALF, dtype=F32) / ROPE_HALF)
    ang = pos.astype(F32)[:, None] * inv[None, :]
    cos, sin = jnp.cos(ang), jnp.sin(ang)
    n = pos.shape[0]
    zeros = lambda w: jnp.zeros((n, w), F32)
    c = jnp.concatenate([cos, cos, jnp.ones((n, HEAD_DIM - ROPE_DIM), F32)], axis=1)
    sa = jnp.concatenate([-sin, zeros(HEAD_DIM - ROPE_HALF)], axis=1)
    sb = jnp.concatenate([zeros(ROPE_HALF), sin, zeros(HEAD_DIM - ROPE_DIM)], axis=1)
    return c, sa, sb


def _block_sum_matrix(n_cmp, n_lane):
    i = jnp.arange(n_cmp)[:, None]
    j = jnp.arange(n_lane)[None, :]
    return ((i // CMP_PER_SLC == j) & (i % CMP_PER_SLC < CMP_INSIDE)).astype(BF16)


def kernel(x_prompt, x_sample, cache_cmp, cache_slc, cache_win, page_table, c_prompt, c_sample, w_ada, b_ada, norm_pre_mix, norm_post_mix, norm_pre_ffn, norm_post_ffn, w_in, ln_v_g, ln_v_b, sgu_w, sgu_b, cmp_pe_k, cmp_pe_v, cmp_w1_k, cmp_w2_k, cmp_w1_v, cmp_w2_v, out_norm_a, out_norm_b, w_out, w_up, w_down):
    batch, seq, d = x_prompt.shape
    n_dec, dec_seq, _ = x_sample.shape
    depth = w_ada.shape[0]
    a_width = d // 2
    b_width = d - a_width
    groups = a_width // HEAD_DIM
    n_heads = b_width // HEAD_DIM
    hpg = n_heads // N_KV
    n_main = 2 * a_width + b_width + 3 * KV_COLS
    n_gate = 3 * n_heads
    n_pages = page_table.shape[1]
    past = n_pages * PAGE_SIZE
    assert dec_seq == 1 and n_dec % 8 == 0 and a_width == b_width and n_gate <= LANES
    assert (2 * a_width + b_width) % (3 * KV_COLS) == 0 and seq % 1024 == 0
    assert past % SLC_BLOCK == 0 and past // SLC_BLOCK + 1 > N_SEL and cache_win.shape[2] <= past
    q_block = 2 * a_width // b_width
    kv_block = (2 * a_width + b_width) // (3 * KV_COLS)
    cmp_block = (2 * a_width + b_width) // KV_COLS
    m_p = batch * seq

    vec3 = lambda a: a.reshape(depth, 1, a.shape[-1])
    big = (w_in, w_out, w_up, w_down)
    wts = [w[0:1].astype(BF16) for w in big]
    w_gate = jnp.pad(w_in[:, :, n_main:], ((0, 0), (0, 0), (0, LANES - n_gate))).astype(BF16)
    cmp_w1 = jnp.stack([cmp_w1_k, cmp_w1_v], axis=1).astype(BF16)
    cmp_w2 = jnp.stack([cmp_w2_k, cmp_w2_v], axis=1).astype(BF16)
    cmp_pe = jnp.stack([cmp_pe_k, cmp_pe_v], axis=1).reshape(depth, 2, CMP_STRIPS, STRIP_COLS)
    pre_mix, post_mix, pre_ffn, post_ffn = map(vec3, (norm_pre_mix, norm_post_mix, norm_pre_ffn, norm_post_ffn))
    ln_g, ln_b, norm_a, norm_b = map(vec3, (ln_v_g, ln_v_b, out_norm_a, out_norm_b))
    norm_b3 = out_norm_b.reshape(depth, n_heads, HEAD_DIM)
    sgu_bt = jnp.swapaxes(sgu_b, 1, 2)
    sgu_w0 = jnp.repeat(sgu_w[:, :, 0, 0], HEAD_DIM, axis=1).reshape(depth, 1, a_width)
    sgu_b0 = jnp.repeat(sgu_b[:, :, 0], HEAD_DIM, axis=1).reshape(depth, 1, a_width)

    mod_rows = -(-(n_dec + batch) // 8) * 8
    c_all = jnp.concatenate([c_sample, c_prompt, jnp.zeros((mod_rows - n_dec - batch, d), F32)], axis=0)
    mod = _ada(c_all, w_ada, b_ada)

    tables_p = _rope_tables(jnp.arange(seq))
    tables_s = _rope_tables(jnp.full((n_dec,), past))
    rope_s = jnp.concatenate([t[0:1] for t in tables_s] + [jnp.zeros((5, LANES), F32)], axis=0)
    n_cmp_p = seq // CMP_STRIDE
    sel_p = _block_sum_matrix(n_cmp_p, LANES)
    key_chunk = 512
    exp_p = (jnp.arange(LANES)[None, :, None]
             == (jnp.arange(seq) // SLC_BLOCK).reshape(seq // key_chunk, 1, key_chunk)).astype(BF16)
    n_cmp_s = past // CMP_STRIDE
    nb_lane = -(-(past // SLC_BLOCK + 1) // LANES) * LANES
    sel_s = _block_sum_matrix(n_cmp_s, nb_lane)

    y_p = x_prompt.reshape(m_p, d)
    y_s = x_sample.reshape(n_dec, d)
    caches_p = [jnp.zeros((depth, batch, rows, N_KV, 2, HEAD_DIM), F32) for rows in (seq, seq, min(WINDOW, seq))]
    outs = [[] for _ in range(4)]
    win_keep = min(WINDOW, seq)
    for l in range(depth):
        w_main, w_out_b, w_up_b, w_down_b = wts
        rows = _Rows(mod, l, d, m_p, 1024, seq, n_dec)
        z, gt = _in_proj(y_p, rows, l, pre_mix, w_main, w_gate, n_main)
        a_n = _sgu(z, l, a_width, ln_g, ln_b, sgu_w, sgu_bt, norm_a)
        *caches_p, slc_b, win_b = _kv_rope_prompt(z, kv_block, tables_p, 256, l, depth, batch, seq, win_keep,
                                                  caches_p)
        kvc = _compress_prompt(z, batch, seq, cmp_block, cmp_pe[l], cmp_w1[l], cmp_w2[l])
        b_n = _attn_prompt(z, gt, kvc, slc_b, win_b, tables_p, l, norm_b, sel_p, exp_p, batch, seq, b_width, q_block)
        y_p = _out_proj(a_n, b_n, y_p, _Rows(mod, l, d, m_p, 512, seq, n_dec), l, w_out_b, post_mix)
        y_p, wts_next = _ffn(y_p, rows, l, pre_ffn, post_ffn, w_up_b, w_down_b,
                             cast_next=big if l + 1 < depth else (), next_layer=l + 1)
        rows = _Rows(mod, l, d, n_dec, n_dec, None, 0)
        z, gt = _in_proj(y_s, rows, l, pre_mix, w_main, w_gate, n_main)
        kv_cmp, kv_slc, kv_win, _, _ = _kv_rope(z, kv_block, tables_s, n_dec)
        kvc = _compress_sample(page_table, cache_cmp, l, kv_cmp, cmp_pe[l], cmp_w1[l], cmp_w2[l])
        q3 = z[:, 2 * a_width:2 * a_width + b_width].reshape(n_dec, n_heads, HEAD_DIM)
        o_c, idx = _select_sample(q3, kvc, sel_s, past, hpg)
        a_n, b_n, v_n = _mix_sample(page_table, idx[:, :, :, 0], cache_slc, cache_win, q3, o_c,
                                    gt[:, :n_gate].reshape(n_dec, n_heads, 3), kv_slc, kv_win,
                                    rope_s, z.reshape(n_dec, 1, n_main), l, ln_g, ln_b, sgu_w0, sgu_b0,
                                    norm_a, norm_b3, past, hpg, a_width)
        y_s = _out_proj(a_n.reshape(n_dec, a_width), b_n.reshape(n_dec, b_width), y_s, rows, l, w_out_b, post_mix)
        y_s, _ = _ffn(y_s, rows, l, pre_ffn, post_ffn, w_up_b, w_down_b)
        wts = [w[None] for w in wts_next]
        outs[0].append(kv_cmp.reshape(n_dec, 1, N_KV, 2, HEAD_DIM))
        outs[1].append(kv_slc.reshape(n_dec, 1, N_KV, 2, HEAD_DIM))
        outs[2].append(kv_win.reshape(n_dec, 1, N_KV, 2, HEAD_DIM))
        outs[3].append(v_n)
    return (y_p.reshape(batch, seq, d), y_s.reshape(n_dec, 1, d), *caches_p, *[jnp.stack(o) for o in outs])
```

```python
import functools

import jax
import jax.numpy as jnp
from jax import lax
from jax.experimental import pallas as pl
from jax.experimental.pallas import tpu as pltpu

F32 = jnp.float32
BF16 = jnp.bfloat16
I32 = jnp.int32

LANES = 128
HEAD_DIM = 128
N_KV = 2
PAGE_SIZE = 128
CHUNK = 128
CMP_BLOCK = 32
CMP_STRIDE = 16
SLC_BLOCK = 64
N_SEL = 16
WINDOW = 512
ROPE_DIM = HEAD_DIM // 4
ROPE_HALF = ROPE_DIM // 2
ROPE_THETA = 500000.0
EPS = 1e-6
FORCE = 1e4
SCALE = HEAD_DIM ** -0.5
EXP2_SCALE = SCALE * 1.4426950408889634
KV_COLS = N_KV * 2 * HEAD_DIM
CMP_STRIPS = CMP_BLOCK // CMP_STRIDE
STRIP_COLS = CMP_STRIDE * HEAD_DIM
CMP_PER_SLC = SLC_BLOCK // CMP_STRIDE
CMP_INSIDE = (SLC_BLOCK - CMP_BLOCK) // CMP_STRIDE + 1
VMEM_LIMIT = 56 * 2 ** 20


def _dot(a, b):
    return jnp.dot(a, b, preferred_element_type=F32)


def _dot_nt(a, b):
    return lax.dot_general(a, b, (((1,), (1,)), ((), ())), preferred_element_type=F32)


def _gelu(x):
    return 0.5 * x * (1.0 + jnp.tanh(0.7978845608028654 * (x + 0.044715 * (x * x * x))))


def _sigmoid(x):
    return 1.0 / (1.0 + jnp.exp(-x))


def _rms(x, g):
    return x * lax.rsqrt(jnp.mean(x * x, axis=-1, keepdims=True) + EPS) * g


def _softmax_neg_inf(s):
    m = jnp.max(s, axis=-1, keepdims=True)
    m = jnp.where(m > -jnp.inf, m, 0.0)
    e = jnp.exp(s - m)
    return e / jnp.maximum(jnp.sum(e, axis=-1, keepdims=True), 1e-30)


def _msoftmax(s, mask):
    return _softmax_neg_inf(jnp.where(mask, s, -jnp.inf))


def _attend(s, v):
    m = jnp.max(s, axis=-1, keepdims=True)
    m = jnp.where(m > -jnp.inf, m, 0.0)
    e = jnp.exp2(s - m)
    return _dot(e.astype(BF16), v) / jnp.maximum(jnp.sum(e, axis=-1, keepdims=True), 1e-30)


def _rope(x, c, sa, sb):
    return x * c + pltpu.roll(x, LANES - ROPE_HALF, axis=1) * sa + pltpu.roll(x, ROPE_HALF, axis=1) * sb


def _split_dot(x, m):
    hi = x.astype(BF16)
    r1 = x - hi.astype(F32)
    mid = r1.astype(BF16)
    lo = (r1 - mid.astype(F32)).astype(BF16)
    return _dot(hi, m) + _dot(mid, m) + _dot(lo, m)


def _modulated_norm(x_ref, g_ref, sc_ref, sh_ref, h_ref):
    h = _rms(x_ref[...], g_ref[...]) * (1.0 + sc_ref[...]) + sh_ref[...]
    h_ref[...] = h.astype(h_ref.dtype)


def _gated_residual(x_ref, gate_ref, g_ref, o_ref):
    o_ref[...] = x_ref[...] + gate_ref[...] * _rms(o_ref[...], g_ref[...])


def _params(sem, vmem=None):
    return pltpu.CompilerParams(dimension_semantics=sem, vmem_limit_bytes=vmem)


def _ada_kernel(c_ref, w_ref, b_ref, o_ref):
    c = c_ref[...]
    o_ref[...] = _dot((c * _sigmoid(c)).astype(BF16), w_ref[...].astype(BF16)) + b_ref[...]


def _ada(c_all, w_ada, b_ada):
    depth, d, n = w_ada.shape
    rows = c_all.shape[0]
    tn = 1024
    return pl.pallas_call(
        _ada_kernel,
        grid=(depth, n // tn),
        in_specs=[pl.BlockSpec((rows, d), lambda l, j: (0, 0)),
                  pl.BlockSpec((None, d, tn), lambda l, j: (l, 0, j)),
                  pl.BlockSpec((None, 1, tn), lambda l, j: (l, 0, j))],
        out_specs=pl.BlockSpec((None, rows, tn), lambda l, j: (l, 0, j)),
        out_shape=jax.ShapeDtypeStruct((depth, rows, n), F32),
        compiler_params=_params(("arbitrary", "arbitrary"), VMEM_LIMIT),
        name="ada",
    )(c_all, w_ada, b_ada.reshape(depth, 1, n))


class _Rows:
    def __init__(self, mod, layer, d, m, tm, rows_per_batch, mod_row0):
        self.m, self.tm, self.d = m, tm, d
        depth, r, n = mod.shape
        if rows_per_batch is None:
            self.mod = mod
            self._spec = lambda k: pl.BlockSpec((None, tm, d), lambda i, *_: (layer, mod_row0 // tm, k))
        else:
            tiles = rows_per_batch // tm
            self.mod = mod.reshape(depth, r, 1, n)
            self._spec = lambda k: pl.BlockSpec((None, None, 1, d),
                                                lambda i, *_: (layer, mod_row0 + i // tiles, 0, k))

    def mod_spec(self, k):
        return self._spec(k)


def _vec_spec(layer, n):
    return pl.BlockSpec((None, 1, n), lambda *_: (layer, 0, 0))


def _in_kernel(x_ref, g_ref, sc_ref, sh_ref, w_ref, wg_ref, z_ref, gt_ref, h_ref):
    @pl.when(pl.program_id(1) == 0)
    def _():
        _modulated_norm(x_ref, g_ref, sc_ref, sh_ref, h_ref)
        gt_ref[...] = _dot(h_ref[...].astype(BF16), wg_ref[...])

    z_ref[...] = _dot(h_ref[...].astype(BF16), w_ref[...])


def _in_proj(x, rows, layer, norm_w, w_main, w_gate, n):
    m, tm, d = rows.m, rows.tm, rows.d
    tn = 768 if n % 768 == 0 else 512
    h_dtype = BF16 if tm % 16 == 0 else F32
    return pl.pallas_call(
        _in_kernel,
        grid=(m // tm, n // tn),
        in_specs=[pl.BlockSpec((tm, d), lambda i, j: (i, 0)),
                  _vec_spec(layer, d), rows.mod_spec(1), rows.mod_spec(0),
                  pl.BlockSpec((None, d, tn), lambda i, j: (layer, 0, j)),
                  pl.BlockSpec((None, d, LANES), lambda i, j: (layer, 0, 0))],
        out_specs=[pl.BlockSpec((tm, tn), lambda i, j: (i, j)),
                   pl.BlockSpec((tm, LANES), lambda i, j: (i, 0))],
        out_shape=[jax.ShapeDtypeStruct((m, n), F32), jax.ShapeDtypeStruct((m, LANES), F32)],
        scratch_shapes=[pltpu.VMEM((tm, d), h_dtype)],
        compiler_params=_params(("arbitrary", "arbitrary"), VMEM_LIMIT),
        name="in_proj",
    )(x, norm_w, rows.mod, rows.mod, w_main, w_gate)


def _layernorm(v, g, b):
    mu = jnp.mean(v, axis=-1, keepdims=True)
    vc = v - mu
    return vc * lax.rsqrt(jnp.mean(vc * vc, axis=-1, keepdims=True) + EPS) * g + b


def _sgu_kernel(u_ref, v_ref, lg_ref, lb_ref, w_ref, bt_ref, na_ref, o_ref, *, groups, chunks):
    row = lax.broadcasted_iota(I32, (CHUNK, CHUNK), 0)
    col = lax.broadcasted_iota(I32, (CHUNK, CHUNK), 1)
    ws = [jnp.where(row >= col, w_ref[g], 0.0).astype(BF16) for g in range(groups)]
    for c in range(chunks):
        rs = slice(c * CHUNK, (c + 1) * CHUNK)
        vn = _layernorm(_gelu(v_ref[rs, :]), lg_ref[...], lb_ref[...]).astype(BF16)
        u = _gelu(u_ref[rs, :])
        outs = []
        for g in range(groups):
            cs = slice(g * HEAD_DIM, (g + 1) * HEAD_DIM)
            outs.append(u[:, cs] * (_dot(ws[g], vn[:, cs]) + bt_ref[:, g:g + 1]))
        o_ref[rs, :] = _rms(jnp.concatenate(outs, axis=1), na_ref[...]).astype(BF16)


def _sgu(z, layer, a_width, ln_g, ln_b, sgu_w, sgu_bt, norm_a):
    m = z.shape[0]
    groups = a_width // HEAD_DIM
    chunks = 2
    tr = chunks * CHUNK
    return pl.pallas_call(
        functools.partial(_sgu_kernel, groups=groups, chunks=chunks),
        grid=(m // tr,),
        in_specs=[pl.BlockSpec((tr, a_width), lambda i: (i, 0)),
                  pl.BlockSpec((tr, a_width), lambda i: (i, 1)),
                  _vec_spec(layer, a_width), _vec_spec(layer, a_width),
                  pl.BlockSpec((None, groups, CHUNK, CHUNK), lambda i: (layer, 0, 0, 0)),
                  pl.BlockSpec((None, CHUNK, groups), lambda i: (layer, 0, 0)),
                  _vec_spec(layer, a_width)],
        out_specs=pl.BlockSpec((tr, a_width), lambda i: (i, 0)),
        out_shape=jax.ShapeDtypeStruct((m, a_width), BF16),
        compiler_params=_params(("arbitrary",), VMEM_LIMIT),
        name="sgu",
    )(z, z, ln_g, ln_b, sgu_w, sgu_bt, norm_a)


def _kv_kernel(kv_ref, rc_ref, rsa_ref, rsb_ref, cmp_ref, slc_ref, win_ref, slcb_ref, winb_ref):
    cmp_ref[...] = kv_ref[:, 0:KV_COLS]
    rc, rsa, rsb = rc_ref[...], rsa_ref[...], rsb_ref[...]
    for t, (o_ref, ob_ref) in enumerate(((slc_ref, slcb_ref), (win_ref, winb_ref))):
        for g in range(N_KV):
            k0 = (t + 1) * KV_COLS + g * 2 * HEAD_DIM
            k = _rope(kv_ref[:, k0:k0 + HEAD_DIM], rc, rsa, rsb)
            v = kv_ref[:, k0 + HEAD_DIM:k0 + 2 * HEAD_DIM]
            o0 = g * 2 * HEAD_DIM
            o_ref[:, o0:o0 + HEAD_DIM] = k
            o_ref[:, o0 + HEAD_DIM:o0 + 2 * HEAD_DIM] = v
            ob_ref[:, o0:o0 + HEAD_DIM] = k.astype(BF16)
            ob_ref[:, o0 + HEAD_DIM:o0 + 2 * HEAD_DIM] = v.astype(BF16)


def _kv_cache_kernel(kv_ref, rc_ref, rsa_ref, rsb_ref, *refs):
    cmp_ref, slc_ref, win_ref, slcb_ref, winb_ref = refs[-5:]
    rc, rsa, rsb = rc_ref[...], rsa_ref[...], rsb_ref[...]
    for g in range(N_KV):
        for t in range(2):
            c0 = (g * 2 + t) * HEAD_DIM
            cmp_ref[:, g, t, :] = kv_ref[:, c0:c0 + HEAD_DIM]
    for j, (o_ref, ob_ref) in enumerate(((slc_ref, slcb_ref), (win_ref, winb_ref))):
        for g in range(N_KV):
            k0 = (j + 1) * KV_COLS + g * 2 * HEAD_DIM
            k = _rope(kv_ref[:, k0:k0 + HEAD_DIM], rc, rsa, rsb)
            v = kv_ref[:, k0 + HEAD_DIM:k0 + 2 * HEAD_DIM]
            o0 = g * 2 * HEAD_DIM
            o_ref[:, g, 0, :] = k
            o_ref[:, g, 1, :] = v
            ob_ref[:, o0:o0 + HEAD_DIM] = k.astype(BF16)
            ob_ref[:, o0 + HEAD_DIM:o0 + 2 * HEAD_DIM] = v.astype(BF16)


def _kv_rope_prompt(z, kv_block, tables, tr, layer, depth, batch, seq, win_keep, prev):
    m = z.shape[0]
    nt = seq // tr
    first = (seq - win_keep) // tr
    tspec = pl.BlockSpec((tr, LANES), lambda i: (i % nt, 0))
    bspec = pl.BlockSpec((tr, KV_COLS), lambda i: (i, 0))
    cache_block = (None, None, tr, N_KV, 2, HEAD_DIM)
    cspec = pl.BlockSpec(cache_block, lambda i: (layer, i // nt, i % nt, 0, 0, 0))
    wspec = pl.BlockSpec(cache_block, lambda i: (layer, i // nt, jnp.maximum(i % nt - first, 0), 0, 0, 0))
    cache_shape = lambda rows: jax.ShapeDtypeStruct((depth, batch, rows, N_KV, 2, HEAD_DIM), F32)
    return pl.pallas_call(
        _kv_cache_kernel,
        grid=(m // tr,),
        in_specs=[pl.BlockSpec((tr, 3 * KV_COLS), lambda i: (i, kv_block)), tspec, tspec, tspec]
        + [pl.BlockSpec(memory_space=pl.ANY)] * len(prev),
        out_specs=[cspec, cspec, wspec, bspec, bspec],
        out_shape=[cache_shape(seq), cache_shape(seq), cache_shape(win_keep)]
        + [jax.ShapeDtypeStruct((m, KV_COLS), BF16)] * 2,
        input_output_aliases={4 + i: i for i in range(len(prev))},
        compiler_params=_params(("arbitrary",), VMEM_LIMIT),
        name="kv_rope_prompt",
    )(z, *tables, *prev)


def _kv_rope(z, kv_block, tables, tr):
    m = z.shape[0]
    t_rows = tables[0].shape[0]
    nt = t_rows // tr
    tspec = pl.BlockSpec((tr, LANES), lambda i: (i % nt, 0))
    ospec = pl.BlockSpec((tr, KV_COLS), lambda i: (i, 0))
    return pl.pallas_call(
        _kv_kernel,
        grid=(m // tr,),
        in_specs=[pl.BlockSpec((tr, 3 * KV_COLS), lambda i: (i, kv_block)), tspec, tspec, tspec],
        out_specs=[ospec] * 5,
        out_shape=[jax.ShapeDtypeStruct((m, KV_COLS), F32)] * 3 + [jax.ShapeDtypeStruct((m, KV_COLS), BF16)] * 2,
        compiler_params=_params(("arbitrary",), VMEM_LIMIT),
        name="kv_rope",
    )(z, *tables)


def _cmp_bias_kernel(pe_ref, w1_ref, o_ref):
    o_ref[...] = _dot(pe_ref[...].astype(BF16), w1_ref[...])


def _cmp_bias(pe, w1):
    depth, _, _, k = pe.shape
    hidden = w1.shape[-1]
    return pl.pallas_call(
        _cmp_bias_kernel,
        grid=(depth, 2),
        in_specs=[pl.BlockSpec((None, None, 8, k), lambda l, s: (l, s, 0, 0)),
                  pl.BlockSpec((None, None, k, hidden), lambda l, s: (l, s, 0, 0))],
        out_specs=pl.BlockSpec((None, None, 8, hidden), lambda l, s: (l, s, 0, 0)),
        out_shape=jax.ShapeDtypeStruct((depth, 2, 8, hidden), F32),
        compiler_params=_params(("arbitrary", "arbitrary"), VMEM_LIMIT),
        name="cmp_bias",
    )(pe, w1)


def _compress_strips(xf, w1_ref, sel):
    xb = xf.astype(BF16)
    return _dot(xb, w1_ref[sel, 0:STRIP_COLS, :]), _dot(xb, w1_ref[sel, STRIP_COLS:2 * STRIP_COLS, :])


def _cmp_kernel(x_ref, pe_ref, w1_ref, w2_ref, o_ref, *, n):
    sel = pl.program_id(1) % 2
    xf = jnp.concatenate([x_ref[pl.ds(r, n, stride=CMP_STRIDE), :] for r in range(CMP_STRIDE)], axis=1)
    p, q = _compress_strips(xf, w1_ref, sel)
    hid = _gelu(p + pltpu.roll(q, n - 1, axis=0) + pe_ref[sel, 0:1, :])
    o_ref[...] = _dot(hid.astype(BF16), w2_ref[sel]).astype(BF16)


def _compress_prompt(z, batch, seq, cmp_block, pe2, w1, w2):
    n = seq // CMP_STRIDE
    hidden = w1.shape[-1]
    heads = 2 * N_KV
    return pl.pallas_call(
        functools.partial(_cmp_kernel, n=n),
        grid=(batch, heads),
        in_specs=[pl.BlockSpec((seq, HEAD_DIM), lambda b, h: (b, cmp_block * heads + h)),
                  pl.BlockSpec((2, 8, hidden), lambda b, h: (0, 0, 0)),
                  pl.BlockSpec((2, 2 * STRIP_COLS, hidden), lambda b, h: (0, 0, 0)),
                  pl.BlockSpec((2, hidden, HEAD_DIM), lambda b, h: (0, 0, 0))],
        out_specs=pl.BlockSpec((None, None, n, HEAD_DIM), lambda b, h: (b, h, 0, 0)),
        out_shape=jax.ShapeDtypeStruct((batch, heads, n, HEAD_DIM), BF16),
        compiler_params=_params(("arbitrary", "arbitrary"), VMEM_LIMIT),
        name="compress_prompt",
    )(z, pe2, w1, w2)


def _attn_kernel(q_ref, gt_ref, cmp_ref, slc_ref, win_ref, rc_ref, rsa_ref, rsb_ref, nb_ref, sel_ref, exp_ref,
                 o_ref, acc_ref, *, tq, seq, hpg, n_cmp, kc):
    q0 = pl.program_id(1) * tq
    n_chunks = (q0 + tq + kc - 1) // kc
    kiota = lax.broadcasted_iota(I32, (1, kc), 1)
    pos = q0 + lax.broadcasted_iota(I32, (tq, 1), 0)
    posh = jnp.concatenate([pos] * hpg, axis=0)
    rc = jnp.concatenate([rc_ref[...]] * hpg, axis=0)
    rsa = jnp.concatenate([rsa_ref[...]] * hpg, axis=0)
    rsb = jnp.concatenate([rsb_ref[...]] * hpg, axis=0)
    sig = _sigmoid(gt_ref[...])
    clane = lax.broadcasted_iota(I32, (1, n_cmp), 1)
    n_blk = seq // SLC_BLOCK
    n_sel = min(N_SEL, n_blk)
    nb8 = -(-n_blk // 8) * 8
    brow = lax.broadcasted_iota(I32, (nb8, 1), 0)
    cur_t = (q0 + lax.broadcasted_iota(I32, (1, tq), 1)) // SLC_BLOCK
    forced_t = (brow == 0) | (brow == cur_t) | (brow == cur_t - 1)
    brow_q = lax.broadcasted_iota(I32, (nb8, tq), 0)
    wk = WINDOW + tq
    wstart = pl.multiple_of(jnp.maximum(q0 - WINDOW, 0), LANES)
    wpos = wstart + lax.broadcasted_iota(I32, (1, wk), 1)
    wdiff = posh - wpos
    wmask = (wdiff >= 0) & (wdiff < WINDOW)
    cmask = (clane * CMP_STRIDE + (CMP_BLOCK - 1)) <= posh
    for g in range(N_KV):
        heads = [g * hpg + h for h in range(hpg)]
        qc = jnp.concatenate([q_ref[:, hd * HEAD_DIM:(hd + 1) * HEAD_DIM] for hd in heads], axis=0)
        qrb = (_rope(qc, rc, rsa, rsb) * EXP2_SCALE).astype(BF16)
        k0 = g * 2 * HEAD_DIM
        v0 = k0 + HEAD_DIM
        p = _msoftmax(_dot_nt(qc.astype(BF16), cmp_ref[2 * g]) * SCALE, cmask)
        o_c = _dot(p.astype(BF16), cmp_ref[2 * g + 1])
        imp = p[0:tq]
        for h in range(1, hpg):
            imp = imp + p[h * tq:(h + 1) * tq]
        impb = _split_dot(imp, sel_ref[...])
        score = jnp.where(brow <= cur_t, jnp.where(forced_t, FORCE, impb.T[0:nb8]), -FORCE)
        rank = jnp.zeros((nb8, tq), F32)
        for i in range(n_blk):
            ci = score[i:i + 1, :]
            tie = jnp.where(brow_q > i, 1.0, 0.0)
            rank = rank + jnp.where(ci > score, 1.0, jnp.where(ci == score, tie, 0.0))
        chosen_t = jnp.where((rank < float(n_sel)) & (brow <= cur_t), 1.0, 0.0)
        chosen_b = jnp.concatenate([chosen_t, jnp.zeros((LANES - nb8, tq), F32)], axis=0).T.astype(BF16)
        rows = hpg * tq

        def chunk(c, carry, diagonal):
            m, l, acc = carry
            r0 = pl.multiple_of(c * kc, kc)
            open_key = _dot(chosen_b, exp_ref[c]) > 0.5
            if diagonal:
                open_key = open_key & ((r0 + kiota) <= pos)
            bias = jnp.where(open_key, 0.0, -jnp.inf)
            s = _dot_nt(qrb, slc_ref[pl.ds(r0, kc), k0:k0 + HEAD_DIM]).reshape(hpg, tq, kc) + bias[None]
            m_new = jnp.maximum(m, jnp.max(s, axis=-1, keepdims=True))
            m_ref = jnp.where(m_new > -jnp.inf, m_new, 0.0)
            alpha = jnp.exp2(m - m_ref)
            e = jnp.exp2(s - m_ref)
            l = alpha * l + jnp.sum(e, axis=-1, keepdims=True)
            pv = _dot(e.reshape(rows, kc).astype(BF16), slc_ref[pl.ds(r0, kc), v0:v0 + HEAD_DIM])
            return m_new, l, alpha.reshape(rows, 1) * acc + pv

        init = (jnp.full((hpg, tq, 1), -jnp.inf, F32), jnp.zeros((hpg, tq, 1), F32),
                jnp.zeros((rows, HEAD_DIM), F32))
        carry = lax.fori_loop(0, n_chunks - 1, functools.partial(chunk, diagonal=False), init)
        _, l, acc = chunk(n_chunks - 1, carry, diagonal=True)
        o_s = acc / jnp.maximum(l.reshape(rows, 1), 1e-30)
        s = jnp.where(wmask, _dot_nt(qrb, win_ref[pl.ds(wstart, wk), k0:k0 + HEAD_DIM]), -jnp.inf)
        o_w = _attend(s, win_ref[pl.ds(wstart, wk), v0:v0 + HEAD_DIM])
        for h, hd in enumerate(heads):
            rs = slice(h * tq, (h + 1) * tq)
            acc_ref[:, hd * HEAD_DIM:(hd + 1) * HEAD_DIM] = (
                sig[:, 3 * hd:3 * hd + 1] * o_c[rs] + sig[:, 3 * hd + 1:3 * hd + 2] * o_s[rs]
                + sig[:, 3 * hd + 2:3 * hd + 3] * o_w[rs])
    o_ref[...] = _rms(acc_ref[...], nb_ref[...]).astype(BF16)


def _attn_prompt(z, gt, kvc, slc_b, win_b, tables, layer, norm_b, sel_m, exp_m, batch, seq, b_width, q_block):
    tq = 128
    nq = seq // tq
    hpg = b_width // HEAD_DIM // N_KV
    n_cmp = kvc.shape[2]
    kc = exp_m.shape[2]
    tspec = pl.BlockSpec((tq, LANES), lambda b, i: (i, 0))
    kvspec = pl.BlockSpec((seq, KV_COLS), lambda b, i: (b, 0))
    return pl.pallas_call(
        functools.partial(_attn_kernel, tq=tq, seq=seq, hpg=hpg, n_cmp=n_cmp, kc=kc),
        grid=(batch, nq),
        in_specs=[pl.BlockSpec((tq, b_width), lambda b, i: (b * nq + i, q_block)),
                  pl.BlockSpec((tq, LANES), lambda b, i: (b * nq + i, 0)),
                  pl.BlockSpec((None, 2 * N_KV, n_cmp, HEAD_DIM), lambda b, i: (b, 0, 0, 0)),
                  kvspec, kvspec, tspec, tspec, tspec,
                  pl.BlockSpec((None, 1, b_width), lambda b, i: (layer, 0, 0)),
                  pl.BlockSpec(sel_m.shape, lambda b, i: (0, 0)),
                  pl.BlockSpec(exp_m.shape, lambda b, i: (0, 0, 0))],
        out_specs=pl.BlockSpec((tq, b_width), lambda b, i: (b * nq + i, 0)),
        out_shape=jax.ShapeDtypeStruct((batch * seq, b_width), BF16),
        scratch_shapes=[pltpu.VMEM((tq, b_width), F32)],
        compiler_params=_params(("arbitrary", "arbitrary"), VMEM_LIMIT),
        name="attn_prompt",
    )(z, gt, kvc, slc_b, win_b, *tables, norm_b, sel_m, exp_m)


def _out_kernel(a_ref, b_ref, w_ref, x_ref, gate_ref, g_ref, o_ref, *, a_width):
    o_ref[...] = _dot(a_ref[...], w_ref[0:a_width, :]) + _dot(b_ref[...], w_ref[a_width:, :])
    _gated_residual(x_ref, gate_ref, g_ref, o_ref)


def _out_proj(a_n, b_n, x, rows, layer, w_out, norm_w):
    m, tm, d = rows.m, rows.tm, rows.d
    a_width = a_n.shape[1]
    b_width = b_n.shape[1]
    return pl.pallas_call(
        functools.partial(_out_kernel, a_width=a_width),
        grid=(m // tm,),
        in_specs=[pl.BlockSpec((tm, a_width), lambda i: (i, 0)),
                  pl.BlockSpec((tm, b_width), lambda i: (i, 0)),
                  pl.BlockSpec((None, a_width + b_width, d), lambda i: (0, 0, 0)),
                  pl.BlockSpec((tm, d), lambda i: (i, 0)),
                  rows.mod_spec(2), _vec_spec(layer, d)],
        out_specs=pl.BlockSpec((tm, d), lambda i: (i, 0)),
        out_shape=jax.ShapeDtypeStruct((m, d), F32),
        compiler_params=_params(("arbitrary",), VMEM_LIMIT),
        name="out_proj",
    )(a_n, b_n, w_out, x, rows.mod, norm_w)


def _ffn_kernel(x_ref, g1_ref, sc_ref, sh_ref, wu_ref, wd_ref, gate_ref, g2_ref, *refs, n_cast):
    cast_in, o_ref, cast_out, h_ref = refs[:n_cast], refs[n_cast], refs[n_cast + 1:2 * n_cast + 1], refs[-1]
    k = pl.program_id(1)
    for src, dst in zip(cast_in, cast_out):
        dst[...] = src[...].astype(BF16)

    @pl.when(k == 0)
    def _():
        _modulated_norm(x_ref, g1_ref, sc_ref, sh_ref, h_ref)
        o_ref[...] = jnp.zeros_like(o_ref)

    up = jnp.maximum(_dot(h_ref[...].astype(BF16), wu_ref[...]), 0.0)
    o_ref[...] += _dot((up * up).astype(BF16), wd_ref[...])

    @pl.when(k == pl.num_programs(1) - 1)
    def _():
        _gated_residual(x_ref, gate_ref, g2_ref, o_ref)


def _ffn(x, rows, layer, norm_pre, norm_post, w_up, w_down, cast_next=(), next_layer=0):
    m, tm, d = rows.m, rows.tm, rows.d
    d_ff = w_up.shape[-1]
    tf = 512
    ni, nk = m // tm, d_ff // tf
    h_dtype = BF16 if tm % 16 == 0 else F32
    x_mode = dict(pipeline_mode=pl.Buffered(1)) if tm * d * 4 >= 8 * 2 ** 20 else {}
    cast_in, cast_out, cast_shape = [], [], []
    for w in cast_next:
        r, c = w.shape[1:]
        band = r // (ni * nk)
        assert band * ni * nk == r and band % 16 == 0
        cast_in.append(pl.BlockSpec((None, band, c), lambda i, k: (next_layer, i * nk + k, 0)))
        cast_out.append(pl.BlockSpec((band, c), lambda i, k: (i * nk + k, 0)))
        cast_shape.append(jax.ShapeDtypeStruct((r, c), BF16))
    y, *casts = pl.pallas_call(
        functools.partial(_ffn_kernel, n_cast=len(cast_next)),
        grid=(ni, nk),
        in_specs=[pl.BlockSpec((tm, d), lambda i, k: (i, 0), **x_mode),
                  _vec_spec(layer, d), rows.mod_spec(4), rows.mod_spec(3),
                  pl.BlockSpec((None, d, tf), lambda i, k: (0, 0, k)),
                  pl.BlockSpec((None, tf, d), lambda i, k: (0, k, 0)),
                  rows.mod_spec(5), _vec_spec(layer, d)] + cast_in,
        out_specs=[pl.BlockSpec((tm, d), lambda i, k: (i, 0))] + cast_out,
        out_shape=[jax.ShapeDtypeStruct((m, d), F32)] + cast_shape,
        scratch_shapes=[pltpu.VMEM((tm, d), h_dtype)],
        compiler_params=_params(("arbitrary", "arbitrary"), VMEM_LIMIT),
        name="ffn",
    )(x, norm_pre, rows.mod, rows.mod, w_up, w_down, rows.mod, norm_post, *cast_next)
    return y, casts


def _scmp_kernel(pt_ref, cache_ref, new_ref, pe_ref, w1_ref, w2_ref, o_ref, buf, sem, p_s, q_s,
                 *, layer, pages, n_chunk, n_batch):
    b = pl.program_id(0)
    c = pl.program_id(1)
    step = b * n_chunk + c
    slot = step % 2
    strips = pages * (PAGE_SIZE // CMP_STRIDE)
    n_past = n_chunk * strips

    def copies(bb, cc, sl):
        return [pltpu.make_async_copy(cache_ref.at[layer, pt_ref[bb, cc * pages + p], :, gs // 2, gs % 2, :],
                                      buf.at[sl, gs, p], sem.at[sl])
                for p in range(pages) for gs in range(2 * N_KV)]

    @pl.when(step == 0)
    def _():
        for cp in copies(0, 0, 0):
            cp.start()

    @pl.when(step + 1 < n_batch * n_chunk)
    def _():
        nxt = step + 1
        for cp in copies(nxt // n_chunk, nxt % n_chunk, 1 - slot):
            cp.start()

    for cp in copies(b, c, slot):
        cp.wait()

    cur = buf.at[slot]
    row0 = pl.multiple_of(c * strips, strips)
    for gs in range(2 * N_KV):
        sel = gs % 2
        xf = jnp.concatenate(
            [cur[gs, :, pl.ds(r, PAGE_SIZE // CMP_STRIDE, stride=CMP_STRIDE), :].reshape(strips, HEAD_DIM)
             for r in range(CMP_STRIDE)], axis=1)
        p, q = _compress_strips(xf, w1_ref, sel)
        p_s[gs, pl.ds(row0, strips), :] = p
        q_s[gs, pl.ds(row0, strips), :] = q

    @pl.when(c == n_chunk - 1)
    def _():
        for gs in range(2 * N_KV):
            sel = gs % 2
            new = new_ref[:, gs * HEAD_DIM:(gs + 1) * HEAD_DIM]
            xn = jnp.concatenate([new, jnp.zeros((1, STRIP_COLS - HEAD_DIM), F32)], axis=1)
            xn = jnp.broadcast_to(xn, (8, STRIP_COLS)).astype(BF16)
            q_s[gs, n_past:n_past + 8, :] = _dot(xn, w1_ref[sel, STRIP_COLS:2 * STRIP_COLS, :])
            hid = _gelu(p_s[gs] + q_s[gs, 1:n_past + 1, :] + pe_ref[sel, 0:1, :])
            o_ref[gs] = _dot(hid.astype(BF16), w2_ref[sel]).astype(BF16)


def _compress_sample(page_table, cache, layer, new_cmp, pe2, w1, w2):
    n_batch, n_pages = page_table.shape
    pages = 32 if n_pages % 32 == 0 else n_pages
    n_chunk = n_pages // pages
    n_past = n_pages * (PAGE_SIZE // CMP_STRIDE)
    hidden = w1.shape[-1]
    grid_spec = pltpu.PrefetchScalarGridSpec(
        num_scalar_prefetch=1,
        grid=(n_batch, n_chunk),
        in_specs=[pl.BlockSpec(memory_space=pl.ANY),
                  pl.BlockSpec((None, 1, KV_COLS), lambda b, c, pt: (b, 0, 0)),
                  pl.BlockSpec((2, 8, hidden), lambda b, c, pt: (0, 0, 0)),
                  pl.BlockSpec((2, 2 * STRIP_COLS, hidden), lambda b, c, pt: (0, 0, 0)),
                  pl.BlockSpec((2, hidden, HEAD_DIM), lambda b, c, pt: (0, 0, 0))],
        out_specs=pl.BlockSpec((None, 2 * N_KV, n_past, HEAD_DIM), lambda b, c, pt: (b, 0, 0, 0)),
        scratch_shapes=[pltpu.VMEM((2, 2 * N_KV, pages, PAGE_SIZE, HEAD_DIM), F32),
                        pltpu.SemaphoreType.DMA((2,)),
                        pltpu.VMEM((2 * N_KV, n_past, hidden), F32),
                        pltpu.VMEM((2 * N_KV, n_past + 8, hidden), F32)])
    return pl.pallas_call(
        functools.partial(_scmp_kernel, layer=layer, pages=pages, n_chunk=n_chunk, n_batch=n_batch),
        grid_spec=grid_spec,
        out_shape=jax.ShapeDtypeStruct((n_batch, 2 * N_KV, n_past, HEAD_DIM), BF16),
        compiler_params=_params(("arbitrary", "arbitrary"), VMEM_LIMIT),
        name="compress_sample",
    )(page_table, cache, new_cmp.reshape(n_batch, 1, KV_COLS), pe2, w1, w2)


def _ssel_kernel(q_ref, kvc_ref, sel_ref, oc_ref, idx_ref, *, pos, hpg, n_cmp, n_lane):
    qb = q_ref[...].astype(BF16)
    n_heads = qb.shape[0]
    head_row = lax.broadcasted_iota(I32, (n_heads, 1), 0)
    clane = lax.broadcasted_iota(I32, (1, n_cmp), 1)
    cmask = (clane * CMP_STRIDE + (CMP_BLOCK - 1)) <= pos
    lane = lax.broadcasted_iota(I32, (1, n_lane), 1)
    cur = pos // SLC_BLOCK
    forced = (lane == 0) | (lane == cur) | (lane == cur - 1)
    ri = lax.broadcasted_iota(I32, (n_lane, n_lane), 0)
    ci = lax.broadcasted_iota(I32, (n_lane, n_lane), 1)
    slot_id = lax.broadcasted_iota(I32, (N_SEL, 1), 0)
    o_c = jnp.zeros((n_heads, HEAD_DIM), F32)
    for g in range(N_KV):
        mine = (head_row // hpg) == g
        p = _msoftmax(_dot_nt(qb, kvc_ref[2 * g]) * SCALE, cmask)
        o_c = jnp.where(mine, _dot(p.astype(BF16), kvc_ref[2 * g + 1]), o_c)
        imp = jnp.sum(jnp.where(mine, p, 0.0), axis=0, keepdims=True)
        impb = _split_dot(jnp.broadcast_to(imp, (8, n_cmp)), sel_ref[...])[0:1]
        score = jnp.where(lane <= cur, jnp.where(forced, FORCE, impb), -FORCE)
        srow = jnp.broadcast_to(score, (n_lane, n_lane))
        scol = srow.T
        ahead = (scol > srow) | ((scol == srow) & (ri < ci))
        rank = jnp.sum(jnp.where(ahead, 1.0, 0.0), axis=0, keepdims=True)
        chosen = jnp.where((rank < float(N_SEL)) & (lane <= cur), 1.0, 0.0)
        ccol = jnp.broadcast_to(chosen, (n_lane, n_lane)).T
        before = jnp.sum(jnp.where(ri < ci, ccol, 0.0), axis=0, keepdims=True)
        hit = (chosen > 0.5) & (before == slot_id.astype(F32))
        idx = jnp.sum(jnp.where(hit, lane.astype(F32), 0.0), axis=1, keepdims=True)
        idx_ref[g] = jnp.broadcast_to(idx, (N_SEL, LANES)).astype(I32)
    oc_ref[...] = o_c


def _select_sample(q3, kvc, sel_m, pos, hpg):
    n_batch, n_heads, _ = q3.shape
    n_cmp = kvc.shape[2]
    n_lane = sel_m.shape[1]
    return pl.pallas_call(
        functools.partial(_ssel_kernel, pos=pos, hpg=hpg, n_cmp=n_cmp, n_lane=n_lane),
        grid=(n_batch,),
        in_specs=[pl.BlockSpec((None, n_heads, HEAD_DIM), lambda b: (b, 0, 0)),
                  pl.BlockSpec((None, 2 * N_KV, n_cmp, HEAD_DIM), lambda b: (b, 0, 0, 0)),
                  pl.BlockSpec(sel_m.shape, lambda b: (0, 0))],
        out_specs=[pl.BlockSpec((None, n_heads, HEAD_DIM), lambda b: (b, 0, 0)),
                   pl.BlockSpec((None, N_KV, N_SEL, LANES), lambda b: (b, 0, 0, 0))],
        out_shape=[jax.ShapeDtypeStruct((n_batch, n_heads, HEAD_DIM), F32),
                   jax.ShapeDtypeStruct((n_batch, N_KV, N_SEL, LANES), I32)],
        compiler_params=_params(("arbitrary",), VMEM_LIMIT),
        name="select_sample",
    )(q3, kvc, sel_m)


def _smix_kernel(pt_ref, idx_ref, cache_ref, wcache_ref, q_ref, oc_ref, gt_ref, slc_ref, win_ref, rope_ref,
                 u_ref, v_ref, lg_ref, lb_ref, w0_ref, b0_ref, na_ref, nb_ref,
                 a_ref, b_ref, vn_ref, buf, wbuf, sem, *, layer, pos, hpg, nb_past):
    b = pl.program_id(0)
    halves = PAGE_SIZE // SLC_BLOCK

    def copies():
        out = []
        for g in range(N_KV):
            for k in range(N_SEL):
                jp = jnp.minimum(idx_ref[b, g, k], nb_past - 1)
                phys = pt_ref[b, jp // halves]
                rows = pl.ds(pl.multiple_of((jp % halves) * SLC_BLOCK, SLC_BLOCK), SLC_BLOCK)
                for t in range(2):
                    out.append(pltpu.make_async_copy(cache_ref.at[layer, phys, rows, g, t, :], buf.at[t, g, k],
                                                     sem.at[0]))
            for t in range(2):
                out.append(pltpu.make_async_copy(wcache_ref.at[layer, b, :, g, t, :], wbuf.at[g, t], sem.at[1]))
        return out

    for cp in copies():
        cp.start()

    vn = _layernorm(_gelu(v_ref[...]), lg_ref[...], lb_ref[...])
    vn_ref[...] = vn
    a_out = _gelu(u_ref[...]) * (w0_ref[...] * vn + b0_ref[...])
    a_ref[...] = _rms(a_out, na_ref[...]).astype(BF16)

    q = q_ref[...]
    n_heads = q.shape[0]
    head_row = lax.broadcasted_iota(I32, (n_heads, 1), 0)
    qrb = _rope(q, rope_ref[0:1, :], rope_ref[1:2, :], rope_ref[2:3, :]).astype(BF16)
    sig = _sigmoid(gt_ref[...])
    n_keys = N_SEL * SLC_BLOCK
    klane = lax.broadcasted_iota(I32, (1, n_keys), 1)
    wb = wbuf.shape[2]
    wdiff = pos - (pos - wb + lax.broadcasted_iota(I32, (1, wb), 1))
    wmask = (wdiff >= 0) & (wdiff < WINDOW)

    for cp in copies():
        cp.wait()

    o_s = jnp.zeros((n_heads, HEAD_DIM), F32)
    o_w = jnp.zeros((n_heads, HEAD_DIM), F32)
    for g in range(N_KV):
        mine = (head_row // hpg) == g
        k0 = g * 2 * HEAD_DIM
        v0 = k0 + HEAD_DIM
        kmask = jnp.zeros((1, n_keys), F32)
        has_new = jnp.zeros((1, 1), F32)
        for k in range(N_SEL):
            blk = idx_ref[b, g, k]
            in_k = (klane // SLC_BLOCK) == k
            kmask = jnp.where(in_k & (blk < nb_past), 1.0, kmask)
            has_new = jnp.where(blk >= nb_past, 1.0, has_new)
        ks = buf[0, g].reshape(n_keys, HEAD_DIM).astype(BF16)
        vs = buf[1, g].reshape(n_keys, HEAD_DIM).astype(BF16)
        s = jnp.where(kmask > 0.5, _dot_nt(qrb, ks) * SCALE, -jnp.inf)
        k_new = slc_ref[:, k0:k0 + HEAD_DIM].astype(BF16)
        v_new = slc_ref[:, v0:v0 + HEAD_DIM].astype(BF16)
        s_new = jnp.sum(qrb.astype(F32) * k_new.astype(F32), axis=-1, keepdims=True) * SCALE
        s_new = jnp.where(has_new > 0.5, s_new, -jnp.inf)
        m = jnp.maximum(jnp.max(s, axis=-1, keepdims=True), s_new)
        m = jnp.where(m > -jnp.inf, m, 0.0)
        e = jnp.exp(s - m)
        e_new = jnp.exp(s_new - m)
        den = jnp.maximum(jnp.sum(e, axis=-1, keepdims=True) + e_new, 1e-30)
        og = _dot((e / den).astype(BF16), vs) + (e_new / den).astype(BF16).astype(F32) * v_new.astype(F32)
        o_s = jnp.where(mine, og, o_s)
        s = jnp.where(wmask, _dot_nt(qrb, wbuf[g, 0].astype(BF16)) * SCALE, -jnp.inf)
        k_new = win_ref[:, k0:k0 + HEAD_DIM].astype(BF16)
        v_new = win_ref[:, v0:v0 + HEAD_DIM].astype(BF16)
        s_new = jnp.sum(qrb.astype(F32) * k_new.astype(F32), axis=-1, keepdims=True) * SCALE
        m = jnp.maximum(jnp.max(s, axis=-1, keepdims=True), s_new)
        e = jnp.exp(s - m)
        e_new = jnp.exp(s_new - m)
        den = jnp.maximum(jnp.sum(e, axis=-1, keepdims=True) + e_new, 1e-30)
        og = (_dot((e / den).astype(BF16), wbuf[g, 1].astype(BF16))
              + (e_new / den).astype(BF16).astype(F32) * v_new.astype(F32))
        o_w = jnp.where(mine, og, o_w)
    b_out = sig[:, 0:1] * oc_ref[...] + sig[:, 1:2] * o_s + sig[:, 2:3] * o_w
    ms = jnp.sum(jnp.sum(b_out * b_out, axis=-1, keepdims=True), axis=0, keepdims=True) / float(n_heads * HEAD_DIM)
    b_ref[...] = (b_out * lax.rsqrt(ms + EPS) * nb_ref[...]).astype(BF16)


def _mix_sample(page_table, idx, cache_slc, cache_win, q3, o_c, gt3, slc_new, win_new, rope_s, z3, layer,
                ln_g, ln_b, w0, b0, norm_a, norm_b3, pos, hpg, a_width):
    n_batch, n_heads, _ = q3.shape
    wb = cache_win.shape[2]
    nb_past = page_table.shape[1] * (PAGE_SIZE // SLC_BLOCK)
    head_spec = pl.BlockSpec((None, n_heads, HEAD_DIM), lambda b, *_: (b, 0, 0))
    new_spec = pl.BlockSpec((None, 1, KV_COLS), lambda b, *_: (b, 0, 0))
    row_spec = lambda blk: pl.BlockSpec((None, 1, a_width), lambda b, *_: (b, 0, blk))
    vec = lambda: pl.BlockSpec((None, 1, a_width), lambda b, *_: (layer, 0, 0))
    grid_spec = pltpu.PrefetchScalarGridSpec(
        num_scalar_prefetch=2,
        grid=(n_batch,),
        in_specs=[pl.BlockSpec(memory_space=pl.ANY), pl.BlockSpec(memory_space=pl.ANY), head_spec, head_spec,
                  pl.BlockSpec((None, n_heads, 3), lambda b, *_: (b, 0, 0)),
                  new_spec, new_spec,
                  pl.BlockSpec(rope_s.shape, lambda b, *_: (0, 0)),
                  row_spec(0), row_spec(1), vec(), vec(), vec(), vec(), vec(),
                  pl.BlockSpec((None, n_heads, HEAD_DIM), lambda b, *_: (layer, 0, 0))],
        out_specs=[pl.BlockSpec((None, 1, a_width), lambda b, *_: (b, 0, 0)),
                   head_spec,
                   pl.BlockSpec((None, 1, a_width), lambda b, *_: (b, 0, 0))],
        scratch_shapes=[pltpu.VMEM((2, N_KV, N_SEL, SLC_BLOCK, HEAD_DIM), F32),
                        pltpu.VMEM((N_KV, 2, wb, HEAD_DIM), F32),
                        pltpu.SemaphoreType.DMA((2,))])
    return pl.pallas_call(
        functools.partial(_smix_kernel, layer=layer, pos=pos, hpg=hpg, nb_past=nb_past),
        grid_spec=grid_spec,
        out_shape=[jax.ShapeDtypeStruct((n_batch, 1, a_width), BF16),
                   jax.ShapeDtypeStruct((n_batch, n_heads, HEAD_DIM), BF16),
                   jax.ShapeDtypeStruct((n_batch, 1, a_width), F32)],
        compiler_params=_params(("arbitrary",), VMEM_LIMIT),
        name="mix_sample",
    )(page_table, idx, cache_slc, cache_win, q3, o_c, gt3, slc_new.reshape(n_batch, 1, KV_COLS),
      win_new.reshape(n_batch, 1, KV_COLS), rope_s, z3, z3,
      ln_g, ln_b, w0, b0, norm_a, norm_b3)


def _rope_tables(pos):
    inv = ROPE_THETA ** (-jnp.arange(ROPE_HALF, dtype=F32) / ROPE_HALF)
    ang = pos.astype(F32)[:, None] * inv[None, :]
    cos, sin = jnp.cos(ang), jnp.sin(ang)
    n = pos.shape[0]
    zeros = lambda w: jnp.zeros((n, w), F32)
    c = jnp.concatenate([cos, cos, jnp.ones((n, HEAD_DIM - ROPE_DIM), F32)], axis=1)
    sa = jnp.concatenate([-sin, zeros(HEAD_DIM - ROPE_HALF)], axis=1)
    sb = jnp.concatenate([zeros(ROPE_HALF), sin, zeros(HEAD_DIM - ROPE_DIM)], axis=1)
    return c, sa, sb


def _block_sum_matrix(n_cmp, n_lane):
    i = jnp.arange(n_cmp)[:, None]
    j = jnp.arange(n_lane)[None, :]
    return ((i // CMP_PER_SLC == j) & (i % CMP_PER_SLC < CMP_INSIDE)).astype(BF16)


def kernel(x_prompt, x_sample, cache_cmp, cache_slc, cache_win, page_table, c_prompt, c_sample, w_ada, b_ada, norm_pre_mix, norm_post_mix, norm_pre_ffn, norm_post_ffn, w_in, ln_v_g, ln_v_b, sgu_w, sgu_b, cmp_pe_k, cmp_pe_v, cmp_w1_k, cmp_w2_k, cmp_w1_v, cmp_w2_v, out_norm_a, out_norm_b, w_out, w_up, w_down):
    batch, seq, d = x_prompt.shape
    n_dec, dec_seq, _ = x_sample.shape
    depth = w_ada.shape[0]
    a_width = d // 2
    b_width = d - a_width
    groups = a_width // HEAD_DIM
    n_heads = b_width // HEAD_DIM
    hpg = n_heads // N_KV
    n_main = 2 * a_width + b_width + 3 * KV_COLS
    n_gate = 3 * n_heads
    n_pages = page_table.shape[1]
    past = n_pages * PAGE_SIZE
    assert dec_seq == 1 and n_dec % 8 == 0 and a_width == b_width and n_gate <= LANES
    assert (2 * a_width + b_width) % (3 * KV_COLS) == 0 and seq % 1024 == 0
    assert past % SLC_BLOCK == 0 and past // SLC_BLOCK + 1 > N_SEL and cache_win.shape[2] <= past
    q_block = 2 * a_width // b_width
    kv_block = (2 * a_width + b_width) // (3 * KV_COLS)
    cmp_block = (2 * a_width + b_width) // KV_COLS
    m_p = batch * seq

    vec3 = lambda a: a.reshape(depth, 1, a.shape[-1])
    w_main = w_in.astype(BF16)
    big = (w_out, w_up, w_down)
    wts = [w[0:1].astype(BF16) for w in big]
    w_gate = jnp.pad(w_in[:, :, n_main:], ((0, 0), (0, 0), (0, LANES - n_gate))).astype(BF16)
    cmp_w1 = jnp.stack([cmp_w1_k, cmp_w1_v], axis=1).astype(BF16)
    cmp_w2 = jnp.stack([cmp_w2_k, cmp_w2_v], axis=1).astype(BF16)
    cmp_pe = jnp.broadcast_to(jnp.stack([cmp_pe_k, cmp_pe_v], axis=1).reshape(depth, 2, 1, CMP_BLOCK * HEAD_DIM),
                              (depth, 2, 8, CMP_BLOCK * HEAD_DIM))
    cmp_pe = _cmp_bias(cmp_pe, cmp_w1)
    pre_mix, post_mix, pre_ffn, post_ffn = map(vec3, (norm_pre_mix, norm_post_mix, norm_pre_ffn, norm_post_ffn))
    ln_g, ln_b, norm_a, norm_b = map(vec3, (ln_v_g, ln_v_b, out_norm_a, out_norm_b))
    norm_b3 = out_norm_b.reshape(depth, n_heads, HEAD_DIM)
    sgu_bt = jnp.swapaxes(sgu_b, 1, 2)
    sgu_w0 = jnp.repeat(sgu_w[:, :, 0, 0], HEAD_DIM, axis=1).reshape(depth, 1, a_width)
    sgu_b0 = jnp.repeat(sgu_b[:, :, 0], HEAD_DIM, axis=1).reshape(depth, 1, a_width)

    mod_rows = -(-(n_dec + batch) // 8) * 8
    c_all = jnp.concatenate([c_sample, c_prompt, jnp.zeros((mod_rows - n_dec - batch, d), F32)], axis=0)
    mod = _ada(c_all, w_ada, b_ada)

    tables_p = _rope_tables(jnp.arange(seq))
    tables_s = _rope_tables(jnp.full((n_dec,), past))
    rope_s = jnp.concatenate([t[0:1] for t in tables_s] + [jnp.zeros((5, LANES), F32)], axis=0)
    n_cmp_p = seq // CMP_STRIDE
    sel_p = _block_sum_matrix(n_cmp_p, LANES)
    key_chunk = 512
    exp_p = (jnp.arange(LANES)[None, :, None]
             == (jnp.arange(seq) // SLC_BLOCK).reshape(seq // key_chunk, 1, key_chunk)).astype(BF16)
    n_cmp_s = past // CMP_STRIDE
    nb_lane = -(-(past // SLC_BLOCK + 1) // LANES) * LANES
    sel_s = _block_sum_matrix(n_cmp_s, nb_lane)

    y_p = x_prompt.reshape(m_p, d)
    y_s = x_sample.reshape(n_dec, d)
    caches_p = [jnp.zeros((depth, batch, rows, N_KV, 2, HEAD_DIM), F32) for rows in (seq, seq, min(WINDOW, seq))]
    outs = [[] for _ in range(4)]
    win_keep = min(WINDOW, seq)
    for l in range(depth):
        w_out_b, w_up_b, w_down_b = wts
        rows = _Rows(mod, l, d, m_p, 1024, seq, n_dec)
        z, gt = _in_proj(y_p, rows, l, pre_mix, w_main, w_gate, n_main)
        a_n = _sgu(z, l, a_width, ln_g, ln_b, sgu_w, sgu_bt, norm_a)
        *caches_p, slc_b, win_b = _kv_rope_prompt(z, kv_block, tables_p, 256, l, depth, batch, seq, win_keep,
                                                  caches_p)
        kvc = _compress_prompt(z, batch, seq, cmp_block, cmp_pe[l], cmp_w1[l], cmp_w2[l])
        b_n = _attn_prompt(z, gt, kvc, slc_b, win_b, tables_p, l, norm_b, sel_p, exp_p, batch, seq, b_width, q_block)
        y_p = _out_proj(a_n, b_n, y_p, _Rows(mod, l, d, m_p, 512, seq, n_dec), l, w_out_b, post_mix)
        y_p, wts_next = _ffn(y_p, rows, l, pre_ffn, post_ffn, w_up_b, w_down_b,
                             cast_next=big if l + 1 < depth else (), next_layer=l + 1)
        rows = _Rows(mod, l, d, n_dec, n_dec, None, 0)
        z, gt = _in_proj(y_s, rows, l, pre_mix, w_main, w_gate, n_main)
        kv_cmp, kv_slc, kv_win, _, _ = _kv_rope(z, kv_block, tables_s, n_dec)
        kvc = _compress_sample(page_table, cache_cmp, l, kv_cmp, cmp_pe[l], cmp_w1[l], cmp_w2[l])
        q3 = z[:, 2 * a_width:2 * a_width + b_width].reshape(n_dec, n_heads, HEAD_DIM)
        o_c, idx = _select_sample(q3, kvc, sel_s, past, hpg)
        a_n, b_n, v_n = _mix_sample(page_table, idx[:, :, :, 0], cache_slc, cache_win, q3, o_c,
                                    gt[:, :n_gate].reshape(n_dec, n_heads, 3), kv_slc, kv_win,
                                    rope_s, z.reshape(n_dec, 1, n_main), l, ln_g, ln_b, sgu_w0, sgu_b0,
                                    norm_a, norm_b3, past, hpg, a_width)
        y_s = _out_proj(a_n.reshape(n_dec, a_width), b_n.reshape(n_dec, b_width), y_s, rows, l, w_out_b, post_mix)
        y_s, _ = _ffn(y_s, rows, l, pre_ffn, post_ffn, w_up_b, w_down_b)
        wts = [w[None] for w in wts_next]
        outs[0].append(kv_cmp.reshape(n_dec, 1, N_KV, 2, HEAD_DIM))
        outs[1].append(kv_slc.reshape(n_dec, 1, N_KV, 2, HEAD_DIM))
        outs[2].append(kv_win.reshape(n_dec, 1, N_KV, 2, HEAD_DIM))
        outs[3].append(v_n)
    return (y_p.reshape(batch, seq, d), y_s.reshape(n_dec, 1, d), *caches_p, *[jnp.stack(o) for o in outs])
```

```python
import functools

import jax
import jax.numpy as jnp
from jax import lax
from jax.experimental import pallas as pl
from jax.experimental.pallas import tpu as pltpu

F32 = jnp.float32
BF16 = jnp.bfloat16
I32 = jnp.int32

LANES = 128
ROW_STRIP = 16
HEAD_DIM = 128
N_KV = 2
PAGE_SIZE = 128
CHUNK = 128
CMP_BLOCK = 32
CMP_STRIDE = 16
SLC_BLOCK = 64
N_SEL = 16
WINDOW = 512
ROPE_DIM = HEAD_DIM // 4
ROPE_HALF = ROPE_DIM // 2
ROPE_THETA = 500000.0
EPS = 1e-6
FORCE = 1e4
SCALE = HEAD_DIM ** -0.5
EXP2_SCALE = SCALE * 1.4426950408889634
KV_COLS = N_KV * 2 * HEAD_DIM
CMP_STRIPS = CMP_BLOCK // CMP_STRIDE
STRIP_COLS = CMP_STRIDE * HEAD_DIM
CMP_PER_SLC = SLC_BLOCK // CMP_STRIDE
CMP_INSIDE = (SLC_BLOCK - CMP_BLOCK) // CMP_STRIDE + 1
VMEM_LIMIT = 56 * 2 ** 20


def _dot(a, b):
    return jnp.dot(a, b, preferred_element_type=F32)


def _dot_nt(a, b):
    return lax.dot_general(a, b, (((1,), (1,)), ((), ())), preferred_element_type=F32)


def _gelu(x):
    return 0.5 * x * (1.0 + jnp.tanh(0.7978845608028654 * (x + 0.044715 * (x * x * x))))


def _sigmoid(x):
    return 1.0 / (1.0 + jnp.exp(-x))


def _rms(x, g):
    return x * lax.rsqrt(jnp.mean(x * x, axis=-1, keepdims=True) + EPS) * g


def _softmax_neg_inf(s):
    m = jnp.max(s, axis=-1, keepdims=True)
    m = jnp.where(m > -jnp.inf, m, 0.0)
    e = jnp.exp(s - m)
    return e / jnp.maximum(jnp.sum(e, axis=-1, keepdims=True), 1e-30)


def _msoftmax(s, mask):
    return _softmax_neg_inf(jnp.where(mask, s, -jnp.inf))


def _attend(s, v):
    m = jnp.max(s, axis=-1, keepdims=True)
    m = jnp.where(m > -jnp.inf, m, 0.0)
    e = jnp.exp2(s - m)
    return _dot(e.astype(BF16), v) / jnp.maximum(jnp.sum(e, axis=-1, keepdims=True), 1e-30)


def _rope(x, c, sa, sb):
    return x * c + pltpu.roll(x, LANES - ROPE_HALF, axis=1) * sa + pltpu.roll(x, ROPE_HALF, axis=1) * sb


def _split_dot(x, m):
    hi = x.astype(BF16)
    r1 = x - hi.astype(F32)
    mid = r1.astype(BF16)
    lo = (r1 - mid.astype(F32)).astype(BF16)
    return _dot(hi, m) + _dot(mid, m) + _dot(lo, m)


def _row_strips(ref):
    n = ref.shape[0]
    step = ROW_STRIP if n % ROW_STRIP == 0 else n
    return [slice(r, r + step) for r in range(0, n, step)]


def _mod_rows(ref, rs):
    return ref[...] if ref.shape[0] == 1 else ref[rs, :]


def _modulated_norm(x_ref, g_ref, sc_ref, sh_ref, h_ref):
    for rs in _row_strips(x_ref):
        h = _rms(x_ref[rs, :], g_ref[...]) * (1.0 + _mod_rows(sc_ref, rs)) + _mod_rows(sh_ref, rs)
        h_ref[rs, :] = h.astype(h_ref.dtype)


def _gated_residual(x_ref, gate_ref, g_ref, o_ref):
    for rs in _row_strips(x_ref):
        o_ref[rs, :] = x_ref[rs, :] + _mod_rows(gate_ref, rs) * _rms(o_ref[rs, :], g_ref[...])


def _params(sem, vmem=None):
    return pltpu.CompilerParams(dimension_semantics=sem, vmem_limit_bytes=vmem)


def _ada_kernel(c_ref, w_ref, b_ref, o_ref):
    c = c_ref[...]
    o_ref[...] = _dot((c * _sigmoid(c)).astype(BF16), w_ref[...].astype(BF16)) + b_ref[...]


def _ada(c_all, w_ada, b_ada):
    depth, d, n = w_ada.shape
    rows = c_all.shape[0]
    tn = 1024
    return pl.pallas_call(
        _ada_kernel,
        grid=(depth, n // tn),
        in_specs=[pl.BlockSpec((rows, d), lambda l, j: (0, 0)),
                  pl.BlockSpec((None, d, tn), lambda l, j: (l, 0, j)),
                  pl.BlockSpec((None, 1, tn), lambda l, j: (l, 0, j))],
        out_specs=pl.BlockSpec((None, rows, tn), lambda l, j: (l, 0, j)),
        out_shape=jax.ShapeDtypeStruct((depth, rows, n), F32),
        compiler_params=_params(("arbitrary", "arbitrary"), VMEM_LIMIT),
        name="ada",
    )(c_all, w_ada, b_ada.reshape(depth, 1, n))


class _Rows:
    def __init__(self, mod, layer, d, m, tm, rows_per_batch, mod_row0):
        self.m, self.tm, self.d = m, tm, d
        depth, r, n = mod.shape
        if rows_per_batch is None:
            self.mod = mod
            self._spec = lambda k: pl.BlockSpec((None, tm, d), lambda i, *_: (layer, mod_row0 // tm, k))
        else:
            tiles = rows_per_batch // tm
            self.mod = mod.reshape(depth, r, 1, n)
            self._spec = lambda k: pl.BlockSpec((None, None, 1, d),
                                                lambda i, *_: (layer, mod_row0 + i // tiles, 0, k))

    def mod_spec(self, k):
        return self._spec(k)


def _vec_spec(layer, n):
    return pl.BlockSpec((None, 1, n), lambda *_: (layer, 0, 0))


def _in_kernel(x_ref, g_ref, sc_ref, sh_ref, w_ref, wg_ref, z_ref, gt_ref, h_ref):
    @pl.when(pl.program_id(1) == 0)
    def _():
        _modulated_norm(x_ref, g_ref, sc_ref, sh_ref, h_ref)
        gt_ref[...] = _dot(h_ref[...].astype(BF16), wg_ref[...])

    z_ref[...] = _dot(h_ref[...].astype(BF16), w_ref[...])


def _in_proj(x, rows, layer, norm_w, w_main, w_gate, n):
    m, tm, d = rows.m, rows.tm, rows.d
    tn = 768 if n % 768 == 0 else 512
    h_dtype = BF16 if tm % 16 == 0 else F32
    return pl.pallas_call(
        _in_kernel,
        grid=(m // tm, n // tn),
        in_specs=[pl.BlockSpec((tm, d), lambda i, j: (i, 0)),
                  _vec_spec(layer, d), rows.mod_spec(1), rows.mod_spec(0),
                  pl.BlockSpec((None, d, tn), lambda i, j: (layer, 0, j)),
                  pl.BlockSpec((None, d, LANES), lambda i, j: (layer, 0, 0))],
        out_specs=[pl.BlockSpec((tm, tn), lambda i, j: (i, j)),
                   pl.BlockSpec((tm, LANES), lambda i, j: (i, 0))],
        out_shape=[jax.ShapeDtypeStruct((m, n), F32), jax.ShapeDtypeStruct((m, LANES), F32)],
        scratch_shapes=[pltpu.VMEM((tm, d), h_dtype)],
        compiler_params=_params(("arbitrary", "arbitrary"), VMEM_LIMIT),
        name="in_proj",
    )(x, norm_w, rows.mod, rows.mod, w_main, w_gate)


def _layernorm(v, g, b):
    mu = jnp.mean(v, axis=-1, keepdims=True)
    vc = v - mu
    return vc * lax.rsqrt(jnp.mean(vc * vc, axis=-1, keepdims=True) + EPS) * g + b


def _sgu_kernel(u_ref, v_ref, lg_ref, lb_ref, w_ref, bt_ref, na_ref, o_ref, *, groups, chunks):
    row = lax.broadcasted_iota(I32, (CHUNK, CHUNK), 0)
    col = lax.broadcasted_iota(I32, (CHUNK, CHUNK), 1)
    ws = [jnp.where(row >= col, w_ref[g], 0.0).astype(BF16) for g in range(groups)]
    for c in range(chunks):
        rs = slice(c * CHUNK, (c + 1) * CHUNK)
        vn = _layernorm(_gelu(v_ref[rs, :]), lg_ref[...], lb_ref[...]).astype(BF16)
        u = _gelu(u_ref[rs, :])
        outs = []
        for g in range(groups):
            cs = slice(g * HEAD_DIM, (g + 1) * HEAD_DIM)
            outs.append(u[:, cs] * (_dot(ws[g], vn[:, cs]) + bt_ref[:, g:g + 1]))
        o_ref[rs, :] = _rms(jnp.concatenate(outs, axis=1), na_ref[...]).astype(BF16)


def _sgu(z, layer, a_width, ln_g, ln_b, sgu_w, sgu_bt, norm_a):
    m = z.shape[0]
    groups = a_width // HEAD_DIM
    chunks = 2
    tr = chunks * CHUNK
    return pl.pallas_call(
        functools.partial(_sgu_kernel, groups=groups, chunks=chunks),
        grid=(m // tr,),
        in_specs=[pl.BlockSpec((tr, a_width), lambda i: (i, 0)),
                  pl.BlockSpec((tr, a_width), lambda i: (i, 1)),
                  _vec_spec(layer, a_width), _vec_spec(layer, a_width),
                  pl.BlockSpec((None, groups, CHUNK, CHUNK), lambda i: (layer, 0, 0, 0)),
                  pl.BlockSpec((None, CHUNK, groups), lambda i: (layer, 0, 0)),
                  _vec_spec(layer, a_width)],
        out_specs=pl.BlockSpec((tr, a_width), lambda i: (i, 0)),
        out_shape=jax.ShapeDtypeStruct((m, a_width), BF16),
        compiler_params=_params(("arbitrary",), VMEM_LIMIT),
        name="sgu",
    )(z, z, ln_g, ln_b, sgu_w, sgu_bt, norm_a)


def _kv_kernel(kv_ref, rc_ref, rsa_ref, rsb_ref, cmp_ref, slc_ref, win_ref, slcb_ref, winb_ref):
    cmp_ref[...] = kv_ref[:, 0:KV_COLS]
    rc, rsa, rsb = rc_ref[...], rsa_ref[...], rsb_ref[...]
    for t, (o_ref, ob_ref) in enumerate(((slc_ref, slcb_ref), (win_ref, winb_ref))):
        for g in range(N_KV):
            k0 = (t + 1) * KV_COLS + g * 2 * HEAD_DIM
            k = _rope(kv_ref[:, k0:k0 + HEAD_DIM], rc, rsa, rsb)
            v = kv_ref[:, k0 + HEAD_DIM:k0 + 2 * HEAD_DIM]
            o0 = g * 2 * HEAD_DIM
            o_ref[:, o0:o0 + HEAD_DIM] = k
            o_ref[:, o0 + HEAD_DIM:o0 + 2 * HEAD_DIM] = v
            ob_ref[:, o0:o0 + HEAD_DIM] = k.astype(BF16)
            ob_ref[:, o0 + HEAD_DIM:o0 + 2 * HEAD_DIM] = v.astype(BF16)


def _kv_cache_kernel(kv_ref, rc_ref, rsa_ref, rsb_ref, *refs):
    cmp_ref, slc_ref, win_ref, slcb_ref, winb_ref = refs[-5:]
    rc, rsa, rsb = rc_ref[...], rsa_ref[...], rsb_ref[...]
    for g in range(N_KV):
        for t in range(2):
            c0 = (g * 2 + t) * HEAD_DIM
            cmp_ref[:, g, t, :] = kv_ref[:, c0:c0 + HEAD_DIM]
    for j, (o_ref, ob_ref) in enumerate(((slc_ref, slcb_ref), (win_ref, winb_ref))):
        for g in range(N_KV):
            k0 = (j + 1) * KV_COLS + g * 2 * HEAD_DIM
            k = _rope(kv_ref[:, k0:k0 + HEAD_DIM], rc, rsa, rsb)
            v = kv_ref[:, k0 + HEAD_DIM:k0 + 2 * HEAD_DIM]
            o0 = g * 2 * HEAD_DIM
            o_ref[:, g, 0, :] = k
            o_ref[:, g, 1, :] = v
            ob_ref[:, o0:o0 + HEAD_DIM] = k.astype(BF16)
            ob_ref[:, o0 + HEAD_DIM:o0 + 2 * HEAD_DIM] = v.astype(BF16)


def _kv_rope_prompt(z, kv_block, tables, tr, layer, depth, batch, seq, win_keep, prev):
    m = z.shape[0]
    nt = seq // tr
    first = (seq - win_keep) // tr
    tspec = pl.BlockSpec((tr, LANES), lambda i: (i % nt, 0))
    bspec = pl.BlockSpec((tr, KV_COLS), lambda i: (i, 0))
    cache_block = (None, None, tr, N_KV, 2, HEAD_DIM)
    cspec = pl.BlockSpec(cache_block, lambda i: (layer, i // nt, i % nt, 0, 0, 0))
    wspec = pl.BlockSpec(cache_block, lambda i: (layer, i // nt, jnp.maximum(i % nt - first, 0), 0, 0, 0))
    cache_shape = lambda rows: jax.ShapeDtypeStruct((depth, batch, rows, N_KV, 2, HEAD_DIM), F32)
    return pl.pallas_call(
        _kv_cache_kernel,
        grid=(m // tr,),
        in_specs=[pl.BlockSpec((tr, 3 * KV_COLS), lambda i: (i, kv_block)), tspec, tspec, tspec]
        + [pl.BlockSpec(memory_space=pl.ANY)] * len(prev),
        out_specs=[cspec, cspec, wspec, bspec, bspec],
        out_shape=[cache_shape(seq), cache_shape(seq), cache_shape(win_keep)]
        + [jax.ShapeDtypeStruct((m, KV_COLS), BF16)] * 2,
        input_output_aliases={4 + i: i for i in range(len(prev))},
        compiler_params=_params(("arbitrary",), VMEM_LIMIT),
        name="kv_rope_prompt",
    )(z, *tables, *prev)


def _kv_rope(z, kv_block, tables, tr):
    m = z.shape[0]
    t_rows = tables[0].shape[0]
    nt = t_rows // tr
    tspec = pl.BlockSpec((tr, LANES), lambda i: (i % nt, 0))
    ospec = pl.BlockSpec((tr, KV_COLS), lambda i: (i, 0))
    return pl.pallas_call(
        _kv_kernel,
        grid=(m // tr,),
        in_specs=[pl.BlockSpec((tr, 3 * KV_COLS), lambda i: (i, kv_block)), tspec, tspec, tspec],
        out_specs=[ospec] * 5,
        out_shape=[jax.ShapeDtypeStruct((m, KV_COLS), F32)] * 3 + [jax.ShapeDtypeStruct((m, KV_COLS), BF16)] * 2,
        compiler_params=_params(("arbitrary",), VMEM_LIMIT),
        name="kv_rope",
    )(z, *tables)


def _cmp_bias_kernel(pe_ref, w1_ref, o_ref):
    o_ref[...] = _dot(pe_ref[...].astype(BF16), w1_ref[...])


def _cmp_bias(pe, w1):
    depth, _, _, k = pe.shape
    hidden = w1.shape[-1]
    return pl.pallas_call(
        _cmp_bias_kernel,
        grid=(depth, 2),
        in_specs=[pl.BlockSpec((None, None, 8, k), lambda l, s: (l, s, 0, 0)),
                  pl.BlockSpec((None, None, k, hidden), lambda l, s: (l, s, 0, 0))],
        out_specs=pl.BlockSpec((None, None, 8, hidden), lambda l, s: (l, s, 0, 0)),
        out_shape=jax.ShapeDtypeStruct((depth, 2, 8, hidden), F32),
        compiler_params=_params(("arbitrary", "arbitrary"), VMEM_LIMIT),
        name="cmp_bias",
    )(pe, w1)


def _compress_strips(xf, w1_ref, sel):
    xb = xf.astype(BF16)
    return _dot(xb, w1_ref[sel, 0:STRIP_COLS, :]), _dot(xb, w1_ref[sel, STRIP_COLS:2 * STRIP_COLS, :])


def _cmp_kernel(x_ref, pe_ref, w1_ref, w2_ref, o_ref, *, n):
    sel = pl.program_id(1) % 2
    xf = jnp.concatenate([x_ref[pl.ds(r, n, stride=CMP_STRIDE), :] for r in range(CMP_STRIDE)], axis=1)
    p, q = _compress_strips(xf, w1_ref, sel)
    hid = _gelu(p + pltpu.roll(q, n - 1, axis=0) + pe_ref[sel, 0:1, :])
    o_ref[...] = _dot(hid.astype(BF16), w2_ref[sel]).astype(BF16)


def _compress_prompt(z, batch, seq, cmp_block, pe2, w1, w2):
    n = seq // CMP_STRIDE
    hidden = w1.shape[-1]
    heads = 2 * N_KV
    return pl.pallas_call(
        functools.partial(_cmp_kernel, n=n),
        grid=(batch, heads),
        in_specs=[pl.BlockSpec((seq, HEAD_DIM), lambda b, h: (b, cmp_block * heads + h)),
                  pl.BlockSpec((2, 8, hidden), lambda b, h: (0, 0, 0)),
                  pl.BlockSpec((2, 2 * STRIP_COLS, hidden), lambda b, h: (0, 0, 0)),
                  pl.BlockSpec((2, hidden, HEAD_DIM), lambda b, h: (0, 0, 0))],
        out_specs=pl.BlockSpec((None, None, n, HEAD_DIM), lambda b, h: (b, h, 0, 0)),
        out_shape=jax.ShapeDtypeStruct((batch, heads, n, HEAD_DIM), BF16),
        compiler_params=_params(("arbitrary", "arbitrary"), VMEM_LIMIT),
        name="compress_prompt",
    )(z, pe2, w1, w2)


def _attn_kernel(q_ref, gt_ref, cmp_ref, slc_ref, win_ref, rc_ref, rsa_ref, rsb_ref, nb_ref, sel_ref, exp_ref,
                 o_ref, acc_ref, *, tq, seq, hpg, n_cmp, kc):
    q0 = pl.program_id(1) * tq
    n_chunks = (q0 + tq + kc - 1) // kc
    kiota = lax.broadcasted_iota(I32, (1, kc), 1)
    pos = q0 + lax.broadcasted_iota(I32, (tq, 1), 0)
    posh = jnp.concatenate([pos] * hpg, axis=0)
    rc = jnp.concatenate([rc_ref[...]] * hpg, axis=0)
    rsa = jnp.concatenate([rsa_ref[...]] * hpg, axis=0)
    rsb = jnp.concatenate([rsb_ref[...]] * hpg, axis=0)
    sig = _sigmoid(gt_ref[...])
    clane = lax.broadcasted_iota(I32, (1, n_cmp), 1)
    n_blk = seq // SLC_BLOCK
    n_sel = min(N_SEL, n_blk)
    nb8 = -(-n_blk // 8) * 8
    brow = lax.broadcasted_iota(I32, (nb8, 1), 0)
    cur_t = (q0 + lax.broadcasted_iota(I32, (1, tq), 1)) // SLC_BLOCK
    forced_t = (brow == 0) | (brow == cur_t) | (brow == cur_t - 1)
    brow_q = lax.broadcasted_iota(I32, (nb8, tq), 0)
    wk = WINDOW + tq
    wstart = pl.multiple_of(jnp.maximum(q0 - WINDOW, 0), LANES)
    wpos = wstart + lax.broadcasted_iota(I32, (1, wk), 1)
    wdiff = posh - wpos
    wmask = (wdiff >= 0) & (wdiff < WINDOW)
    cmask = (clane * CMP_STRIDE + (CMP_BLOCK - 1)) <= posh
    for g in range(N_KV):
        heads = [g * hpg + h for h in range(hpg)]
        qc = jnp.concatenate([q_ref[:, hd * HEAD_DIM:(hd + 1) * HEAD_DIM] for hd in heads], axis=0)
        qrb = (_rope(qc, rc, rsa, rsb) * EXP2_SCALE).astype(BF16)
        k0 = g * 2 * HEAD_DIM
        v0 = k0 + HEAD_DIM
        p = _msoftmax(_dot_nt(qc.astype(BF16), cmp_ref[2 * g]) * SCALE, cmask)
        o_c = _dot(p.astype(BF16), cmp_ref[2 * g + 1])
        imp = p[0:tq]
        for h in range(1, hpg):
            imp = imp + p[h * tq:(h + 1) * tq]
        impb = _split_dot(imp, sel_ref[...])
        score = jnp.where(brow <= cur_t, jnp.where(forced_t, FORCE, impb.T[0:nb8]), -FORCE)
        rank = jnp.zeros((nb8, tq), F32)
        for i in range(n_blk):
            ci = score[i:i + 1, :]
            tie = jnp.where(brow_q > i, 1.0, 0.0)
            rank = rank + jnp.where(ci > score, 1.0, jnp.where(ci == score, tie, 0.0))
        chosen_t = jnp.where((rank < float(n_sel)) & (brow <= cur_t), 1.0, 0.0)
        chosen_b = jnp.concatenate([chosen_t, jnp.zeros((LANES - nb8, tq), F32)], axis=0).T.astype(BF16)
        rows = hpg * tq

        def chunk(c, carry, diagonal):
            m, l, acc = carry
            r0 = pl.multiple_of(c * kc, kc)
            open_key = _dot(chosen_b, exp_ref[c]) > 0.5
            if diagonal:
                open_key = open_key & ((r0 + kiota) <= pos)
            bias = jnp.where(open_key, 0.0, -jnp.inf)
            s = _dot_nt(qrb, slc_ref[pl.ds(r0, kc), k0:k0 + HEAD_DIM]).reshape(hpg, tq, kc) + bias[None]
            m_new = jnp.maximum(m, jnp.max(s, axis=-1, keepdims=True))
            m_ref = jnp.where(m_new > -jnp.inf, m_new, 0.0)
            alpha = jnp.exp2(m - m_ref)
            e = jnp.exp2(s - m_ref)
            l = alpha * l + jnp.sum(e, axis=-1, keepdims=True)
            pv = _dot(e.reshape(rows, kc).astype(BF16), slc_ref[pl.ds(r0, kc), v0:v0 + HEAD_DIM])
            return m_new, l, alpha.reshape(rows, 1) * acc + pv

        init = (jnp.full((hpg, tq, 1), -jnp.inf, F32), jnp.zeros((hpg, tq, 1), F32),
                jnp.zeros((rows, HEAD_DIM), F32))
        carry = lax.fori_loop(0, n_chunks - 1, functools.partial(chunk, diagonal=False), init)
        _, l, acc = chunk(n_chunks - 1, carry, diagonal=True)
        o_s = acc / jnp.maximum(l.reshape(rows, 1), 1e-30)
        s = jnp.where(wmask, _dot_nt(qrb, win_ref[pl.ds(wstart, wk), k0:k0 + HEAD_DIM]), -jnp.inf)
        o_w = _attend(s, win_ref[pl.ds(wstart, wk), v0:v0 + HEAD_DIM])
        for h, hd in enumerate(heads):
            rs = slice(h * tq, (h + 1) * tq)
            acc_ref[:, hd * HEAD_DIM:(hd + 1) * HEAD_DIM] = (
                sig[:, 3 * hd:3 * hd + 1] * o_c[rs] + sig[:, 3 * hd + 1:3 * hd + 2] * o_s[rs]
                + sig[:, 3 * hd + 2:3 * hd + 3] * o_w[rs])
    o_ref[...] = _rms(acc_ref[...], nb_ref[...]).astype(BF16)


def _attn_prompt(z, gt, kvc, slc_b, win_b, tables, layer, norm_b, sel_m, exp_m, batch, seq, b_width, q_block):
    tq = 128
    nq = seq // tq
    hpg = b_width // HEAD_DIM // N_KV
    n_cmp = kvc.shape[2]
    kc = exp_m.shape[2]
    tspec = pl.BlockSpec((tq, LANES), lambda b, i: (i, 0))
    kvspec = pl.BlockSpec((seq, KV_COLS), lambda b, i: (b, 0))
    return pl.pallas_call(
        functools.partial(_attn_kernel, tq=tq, seq=seq, hpg=hpg, n_cmp=n_cmp, kc=kc),
        grid=(batch, nq),
        in_specs=[pl.BlockSpec((tq, b_width), lambda b, i: (b * nq + i, q_block)),
                  pl.BlockSpec((tq, LANES), lambda b, i: (b * nq + i, 0)),
                  pl.BlockSpec((None, 2 * N_KV, n_cmp, HEAD_DIM), lambda b, i: (b, 0, 0, 0)),
                  kvspec, kvspec, tspec, tspec, tspec,
                  pl.BlockSpec((None, 1, b_width), lambda b, i: (layer, 0, 0)),
                  pl.BlockSpec(sel_m.shape, lambda b, i: (0, 0)),
                  pl.BlockSpec(exp_m.shape, lambda b, i: (0, 0, 0))],
        out_specs=pl.BlockSpec((tq, b_width), lambda b, i: (b * nq + i, 0)),
        out_shape=jax.ShapeDtypeStruct((batch * seq, b_width), BF16),
        scratch_shapes=[pltpu.VMEM((tq, b_width), F32)],
        compiler_params=_params(("arbitrary", "arbitrary"), VMEM_LIMIT),
        name="attn_prompt",
    )(z, gt, kvc, slc_b, win_b, *tables, norm_b, sel_m, exp_m)


def _out_kernel(a_ref, b_ref, w_ref, x_ref, gate_ref, g_ref, o_ref, *, a_width):
    o_ref[...] = _dot(a_ref[...], w_ref[0:a_width, :]) + _dot(b_ref[...], w_ref[a_width:, :])
    _gated_residual(x_ref, gate_ref, g_ref, o_ref)


def _out_proj(a_n, b_n, x, rows, layer, w_out, norm_w):
    m, tm, d = rows.m, rows.tm, rows.d
    a_width = a_n.shape[1]
    b_width = b_n.shape[1]
    return pl.pallas_call(
        functools.partial(_out_kernel, a_width=a_width),
        grid=(m // tm,),
        in_specs=[pl.BlockSpec((tm, a_width), lambda i: (i, 0)),
                  pl.BlockSpec((tm, b_width), lambda i: (i, 0)),
                  pl.BlockSpec((None, a_width + b_width, d), lambda i: (0, 0, 0)),
                  pl.BlockSpec((tm, d), lambda i: (i, 0)),
                  rows.mod_spec(2), _vec_spec(layer, d)],
        out_specs=pl.BlockSpec((tm, d), lambda i: (i, 0)),
        out_shape=jax.ShapeDtypeStruct((m, d), F32),
        compiler_params=_params(("arbitrary",), VMEM_LIMIT),
        name="out_proj",
    )(a_n, b_n, w_out, x, rows.mod, norm_w)


def _ffn_kernel(x_ref, g1_ref, sc_ref, sh_ref, wu_ref, wd_ref, gate_ref, g2_ref, *refs, n_cast):
    cast_in, o_ref, cast_out, h_ref = refs[:n_cast], refs[n_cast], refs[n_cast + 1:2 * n_cast + 1], refs[-1]
    k = pl.program_id(1)
    for src, dst in zip(cast_in, cast_out):
        dst[...] = src[...].astype(BF16)

    @pl.when(k == 0)
    def _():
        _modulated_norm(x_ref, g1_ref, sc_ref, sh_ref, h_ref)
        o_ref[...] = jnp.zeros_like(o_ref)

    up = jnp.maximum(_dot(h_ref[...].astype(BF16), wu_ref[...]), 0.0)
    o_ref[...] += _dot((up * up).astype(BF16), wd_ref[...])

    @pl.when(k == pl.num_programs(1) - 1)
    def _():
        _gated_residual(x_ref, gate_ref, g2_ref, o_ref)


def _ffn(x, rows, layer, norm_pre, norm_post, w_up, w_down, cast_next=(), next_layer=0):
    m, tm, d = rows.m, rows.tm, rows.d
    d_ff = w_up.shape[-1]
    tf = 512
    ni, nk = m // tm, d_ff // tf
    h_dtype = BF16 if tm % 16 == 0 else F32
    x_mode = dict(pipeline_mode=pl.Buffered(1)) if tm * d * 4 >= 8 * 2 ** 20 else {}
    cast_in, cast_out, cast_shape = [], [], []
    for w in cast_next:
        r, c = w.shape[1:]
        band = r // (ni * nk)
        assert band * ni * nk == r and band % 16 == 0
        cast_in.append(pl.BlockSpec((None, band, c), lambda i, k: (next_layer, i * nk + k, 0)))
        cast_out.append(pl.BlockSpec((band, c), lambda i, k: (i * nk + k, 0)))
        cast_shape.append(jax.ShapeDtypeStruct((r, c), BF16))
    y, *casts = pl.pallas_call(
        functools.partial(_ffn_kernel, n_cast=len(cast_next)),
        grid=(ni, nk),
        in_specs=[pl.BlockSpec((tm, d), lambda i, k: (i, 0), **x_mode),
                  _vec_spec(layer, d), rows.mod_spec(4), rows.mod_spec(3),
                  pl.BlockSpec((None, d, tf), lambda i, k: (0, 0, k)),
                  pl.BlockSpec((None, tf, d), lambda i, k: (0, k, 0)),
                  rows.mod_spec(5), _vec_spec(layer, d)] + cast_in,
        out_specs=[pl.BlockSpec((tm, d), lambda i, k: (i, 0))] + cast_out,
        out_shape=[jax.ShapeDtypeStruct((m, d), F32)] + cast_shape,
        scratch_shapes=[pltpu.VMEM((tm, d), h_dtype)],
        compiler_params=_params(("arbitrary", "arbitrary"), VMEM_LIMIT),
        name="ffn",
    )(x, norm_pre, rows.mod, rows.mod, w_up, w_down, rows.mod, norm_post, *cast_next)
    return y, casts


def _scmp_kernel(pt_ref, cache_ref, new_ref, pe_ref, w1_ref, w2_ref, o_ref, buf, sem, p_s, q_s,
                 *, layer, pages, n_chunk, n_batch):
    b = pl.program_id(0)
    c = pl.program_id(1)
    step = b * n_chunk + c
    slot = step % 2
    strips = pages * (PAGE_SIZE // CMP_STRIDE)
    n_past = n_chunk * strips

    def copies(bb, cc, sl):
        return [pltpu.make_async_copy(cache_ref.at[layer, pt_ref[bb, cc * pages + p], :, gs // 2, gs % 2, :],
                                      buf.at[sl, gs, p], sem.at[sl])
                for p in range(pages) for gs in range(2 * N_KV)]

    @pl.when(step == 0)
    def _():
        for n, cp in enumerate(copies(0, 0, 0)):
            cp.start(priority=n % 2)

    @pl.when(step + 1 < n_batch * n_chunk)
    def _():
        nxt = step + 1
        for n, cp in enumerate(copies(nxt // n_chunk, nxt % n_chunk, 1 - slot)):
            cp.start(priority=n % 2)

    for cp in copies(b, c, slot):
        cp.wait()

    cur = buf.at[slot]
    row0 = pl.multiple_of(c * strips, strips)
    for gs in range(2 * N_KV):
        sel = gs % 2
        xf = jnp.concatenate(
            [cur[gs, :, pl.ds(r, PAGE_SIZE // CMP_STRIDE, stride=CMP_STRIDE), :].reshape(strips, HEAD_DIM)
             for r in range(CMP_STRIDE)], axis=1)
        p, q = _compress_strips(xf, w1_ref, sel)
        p_s[gs, pl.ds(row0, strips), :] = p
        q_s[gs, pl.ds(row0, strips), :] = q

    @pl.when(c == n_chunk - 1)
    def _():
        for gs in range(2 * N_KV):
            sel = gs % 2
            new = new_ref[:, gs * HEAD_DIM:(gs + 1) * HEAD_DIM]
            xn = jnp.concatenate([new, jnp.zeros((1, STRIP_COLS - HEAD_DIM), F32)], axis=1)
            xn = jnp.broadcast_to(xn, (8, STRIP_COLS)).astype(BF16)
            q_s[gs, n_past:n_past + 8, :] = _dot(xn, w1_ref[sel, STRIP_COLS:2 * STRIP_COLS, :])
            hid = _gelu(p_s[gs] + q_s[gs, 1:n_past + 1, :] + pe_ref[sel, 0:1, :])
            o_ref[gs] = _dot(hid.astype(BF16), w2_ref[sel]).astype(BF16)


def _compress_sample(page_table, cache, layer, new_cmp, pe2, w1, w2):
    n_batch, n_pages = page_table.shape
    pages = 32 if n_pages % 32 == 0 else n_pages
    n_chunk = n_pages // pages
    n_past = n_pages * (PAGE_SIZE // CMP_STRIDE)
    hidden = w1.shape[-1]
    grid_spec = pltpu.PrefetchScalarGridSpec(
        num_scalar_prefetch=1,
        grid=(n_batch, n_chunk),
        in_specs=[pl.BlockSpec(memory_space=pl.ANY),
                  pl.BlockSpec((None, 1, KV_COLS), lambda b, c, pt: (b, 0, 0)),
                  pl.BlockSpec((2, 8, hidden), lambda b, c, pt: (0, 0, 0)),
                  pl.BlockSpec((2, 2 * STRIP_COLS, hidden), lambda b, c, pt: (0, 0, 0)),
                  pl.BlockSpec((2, hidden, HEAD_DIM), lambda b, c, pt: (0, 0, 0))],
        out_specs=pl.BlockSpec((None, 2 * N_KV, n_past, HEAD_DIM), lambda b, c, pt: (b, 0, 0, 0)),
        scratch_shapes=[pltpu.VMEM((2, 2 * N_KV, pages, PAGE_SIZE, HEAD_DIM), F32),
                        pltpu.SemaphoreType.DMA((2,)),
                        pltpu.VMEM((2 * N_KV, n_past, hidden), F32),
                        pltpu.VMEM((2 * N_KV, n_past + 8, hidden), F32)])
    return pl.pallas_call(
        functools.partial(_scmp_kernel, layer=layer, pages=pages, n_chunk=n_chunk, n_batch=n_batch),
        grid_spec=grid_spec,
        out_shape=jax.ShapeDtypeStruct((n_batch, 2 * N_KV, n_past, HEAD_DIM), BF16),
        compiler_params=_params(("arbitrary", "arbitrary"), VMEM_LIMIT),
        name="compress_sample",
    )(page_table, cache, new_cmp.reshape(n_batch, 1, KV_COLS), pe2, w1, w2)


def _ssel_kernel(q_ref, kvc_ref, sel_ref, oc_ref, idx_ref, *, pos, hpg, n_cmp, n_lane):
    qb = q_ref[...].astype(BF16)
    n_heads = qb.shape[0]
    head_row = lax.broadcasted_iota(I32, (n_heads, 1), 0)
    clane = lax.broadcasted_iota(I32, (1, n_cmp), 1)
    cmask = (clane * CMP_STRIDE + (CMP_BLOCK - 1)) <= pos
    lane = lax.broadcasted_iota(I32, (1, n_lane), 1)
    cur = pos // SLC_BLOCK
    forced = (lane == 0) | (lane == cur) | (lane == cur - 1)
    ri = lax.broadcasted_iota(I32, (n_lane, n_lane), 0)
    ci = lax.broadcasted_iota(I32, (n_lane, n_lane), 1)
    slot_id = lax.broadcasted_iota(I32, (N_SEL, 1), 0)
    o_c = jnp.zeros((n_heads, HEAD_DIM), F32)
    for g in range(N_KV):
        mine = (head_row // hpg) == g
        p = _msoftmax(_dot_nt(qb, kvc_ref[2 * g]) * SCALE, cmask)
        o_c = jnp.where(mine, _dot(p.astype(BF16), kvc_ref[2 * g + 1]), o_c)
        imp = jnp.sum(jnp.where(mine, p, 0.0), axis=0, keepdims=True)
        impb = _split_dot(jnp.broadcast_to(imp, (8, n_cmp)), sel_ref[...])[0:1]
        score = jnp.where(lane <= cur, jnp.where(forced, FORCE, impb), -FORCE)
        srow = jnp.broadcast_to(score, (n_lane, n_lane))
        scol = srow.T
        ahead = (scol > srow) | ((scol == srow) & (ri < ci))
        rank = jnp.sum(jnp.where(ahead, 1.0, 0.0), axis=0, keepdims=True)
        chosen = jnp.where((rank < float(N_SEL)) & (lane <= cur), 1.0, 0.0)
        ccol = jnp.broadcast_to(chosen, (n_lane, n_lane)).T
        before = jnp.sum(jnp.where(ri < ci, ccol, 0.0), axis=0, keepdims=True)
        hit = (chosen > 0.5) & (before == slot_id.astype(F32))
        idx = jnp.sum(jnp.where(hit, lane.astype(F32), 0.0), axis=1, keepdims=True)
        idx_ref[g] = jnp.broadcast_to(idx, (N_SEL, LANES)).astype(I32)
    oc_ref[...] = o_c


def _select_sample(q3, kvc, sel_m, pos, hpg):
    n_batch, n_heads, _ = q3.shape
    n_cmp = kvc.shape[2]
    n_lane = sel_m.shape[1]
    return pl.pallas_call(
        functools.partial(_ssel_kernel, pos=pos, hpg=hpg, n_cmp=n_cmp, n_lane=n_lane),
        grid=(n_batch,),
        in_specs=[pl.BlockSpec((None, n_heads, HEAD_DIM), lambda b: (b, 0, 0)),
                  pl.BlockSpec((None, 2 * N_KV, n_cmp, HEAD_DIM), lambda b: (b, 0, 0, 0)),
                  pl.BlockSpec(sel_m.shape, lambda b: (0, 0))],
        out_specs=[pl.BlockSpec((None, n_heads, HEAD_DIM), lambda b: (b, 0, 0)),
                   pl.BlockSpec((None, N_KV, N_SEL, LANES), lambda b: (b, 0, 0, 0))],
        out_shape=[jax.ShapeDtypeStruct((n_batch, n_heads, HEAD_DIM), F32),
                   jax.ShapeDtypeStruct((n_batch, N_KV, N_SEL, LANES), I32)],
        compiler_params=_params(("arbitrary",), VMEM_LIMIT),
        name="select_sample",
    )(q3, kvc, sel_m)


def _smix_kernel(pt_ref, idx_ref, cache_ref, wcache_ref, q_ref, oc_ref, gt_ref, slc_ref, win_ref, rope_ref,
                 u_ref, v_ref, lg_ref, lb_ref, w0_ref, b0_ref, na_ref, nb_ref,
                 a_ref, b_ref, vn_ref, buf, wbuf, sem, *, layer, pos, hpg, nb_past):
    b = pl.program_id(0)
    halves = PAGE_SIZE // SLC_BLOCK

    def copies():
        out = []
        for g in range(N_KV):
            for k in range(N_SEL):
                jp = jnp.minimum(idx_ref[b, g, k], nb_past - 1)
                phys = pt_ref[b, jp // halves]
                rows = pl.ds(pl.multiple_of((jp % halves) * SLC_BLOCK, SLC_BLOCK), SLC_BLOCK)
                for t in range(2):
                    out.append(pltpu.make_async_copy(cache_ref.at[layer, phys, rows, g, t, :], buf.at[t, g, k],
                                                     sem.at[0]))
            for t in range(2):
                out.append(pltpu.make_async_copy(wcache_ref.at[layer, b, :, g, t, :], wbuf.at[g, t], sem.at[1]))
        return out

    for n, cp in enumerate(copies()):
        cp.start(priority=n % 2)

    vn = _layernorm(_gelu(v_ref[...]), lg_ref[...], lb_ref[...])
    vn_ref[...] = vn
    a_out = _gelu(u_ref[...]) * (w0_ref[...] * vn + b0_ref[...])
    a_ref[...] = _rms(a_out, na_ref[...]).astype(BF16)

    q = q_ref[...]
    n_heads = q.shape[0]
    head_row = lax.broadcasted_iota(I32, (n_heads, 1), 0)
    qrb = _rope(q, rope_ref[0:1, :], rope_ref[1:2, :], rope_ref[2:3, :]).astype(BF16)
    sig = _sigmoid(gt_ref[...])
    n_keys = N_SEL * SLC_BLOCK
    klane = lax.broadcasted_iota(I32, (1, n_keys), 1)
    wb = wbuf.shape[2]
    wdiff = pos - (pos - wb + lax.broadcasted_iota(I32, (1, wb), 1))
    wmask = (wdiff >= 0) & (wdiff < WINDOW)

    for cp in copies():
        cp.wait()

    o_s = jnp.zeros((n_heads, HEAD_DIM), F32)
    o_w = jnp.zeros((n_heads, HEAD_DIM), F32)
    for g in range(N_KV):
        mine = (head_row // hpg) == g
        k0 = g * 2 * HEAD_DIM
        v0 = k0 + HEAD_DIM
        kmask = jnp.zeros((1, n_keys), F32)
        has_new = jnp.zeros((1, 1), F32)
        for k in range(N_SEL):
            blk = idx_ref[b, g, k]
            in_k = (klane // SLC_BLOCK) == k
            kmask = jnp.where(in_k & (blk < nb_past), 1.0, kmask)
            has_new = jnp.where(blk >= nb_past, 1.0, has_new)
        ks = buf[0, g].reshape(n_keys, HEAD_DIM).astype(BF16)
        vs = buf[1, g].reshape(n_keys, HEAD_DIM).astype(BF16)
        s = jnp.where(kmask > 0.5, _dot_nt(qrb, ks) * SCALE, -jnp.inf)
        k_new = slc_ref[:, k0:k0 + HEAD_DIM].astype(BF16)
        v_new = slc_ref[:, v0:v0 + HEAD_DIM].astype(BF16)
        s_new = jnp.sum(qrb.astype(F32) * k_new.astype(F32), axis=-1, keepdims=True) * SCALE
        s_new = jnp.where(has_new > 0.5, s_new, -jnp.inf)
        m = jnp.maximum(jnp.max(s, axis=-1, keepdims=True), s_new)
        m = jnp.where(m > -jnp.inf, m, 0.0)
        e = jnp.exp(s - m)
        e_new = jnp.exp(s_new - m)
        den = jnp.maximum(jnp.sum(e, axis=-1, keepdims=True) + e_new, 1e-30)
        og = _dot((e / den).astype(BF16), vs) + (e_new / den).astype(BF16).astype(F32) * v_new.astype(F32)
        o_s = jnp.where(mine, og, o_s)
        s = jnp.where(wmask, _dot_nt(qrb, wbuf[g, 0].astype(BF16)) * SCALE, -jnp.inf)
        k_new = win_ref[:, k0:k0 + HEAD_DIM].astype(BF16)
        v_new = win_ref[:, v0:v0 + HEAD_DIM].astype(BF16)
        s_new = jnp.sum(qrb.astype(F32) * k_new.astype(F32), axis=-1, keepdims=True) * SCALE
        m = jnp.maximum(jnp.max(s, axis=-1, keepdims=True), s_new)
        e = jnp.exp(s - m)
        e_new = jnp.exp(s_new - m)
        den = jnp.maximum(jnp.sum(e, axis=-1, keepdims=True) + e_new, 1e-30)
        og = (_dot((e / den).astype(BF16), wbuf[g, 1].astype(BF16))
              + (e_new / den).astype(BF16).astype(F32) * v_new.astype(F32))
        o_w = jnp.where(mine, og, o_w)
    b_out = sig[:, 0:1] * oc_ref[...] + sig[:, 1:2] * o_s + sig[:, 2:3] * o_w
    ms = jnp.sum(jnp.sum(b_out * b_out, axis=-1, keepdims=True), axis=0, keepdims=True) / float(n_heads * HEAD_DIM)
    b_ref[...] = (b_out * lax.rsqrt(ms + EPS) * nb_ref[...]).astype(BF16)


def _mix_sample(page_table, idx, cache_slc, cache_win, q3, o_c, gt3, slc_new, win_new, rope_s, z3, layer,
                ln_g, ln_b, w0, b0, norm_a, norm_b3, pos, hpg, a_width):
    n_batch, n_heads, _ = q3.shape
    wb = cache_win.shape[2]
    nb_past = page_table.shape[1] * (PAGE_SIZE // SLC_BLOCK)
    head_spec = pl.BlockSpec((None, n_heads, HEAD_DIM), lambda b, *_: (b, 0, 0))
    new_spec = pl.BlockSpec((None, 1, KV_COLS), lambda b, *_: (b, 0, 0))
    row_spec = lambda blk: pl.BlockSpec((None, 1, a_width), lambda b, *_: (b, 0, blk))
    vec = lambda: pl.BlockSpec((None, 1, a_width), lambda b, *_: (layer, 0, 0))
    grid_spec = pltpu.PrefetchScalarGridSpec(
        num_scalar_prefetch=2,
        grid=(n_batch,),
        in_specs=[pl.BlockSpec(memory_space=pl.ANY), pl.BlockSpec(memory_space=pl.ANY), head_spec, head_spec,
                  pl.BlockSpec((None, n_heads, 3), lambda b, *_: (b, 0, 0)),
                  new_spec, new_spec,
                  pl.BlockSpec(rope_s.shape, lambda b, *_: (0, 0)),
                  row_spec(0), row_spec(1), vec(), vec(), vec(), vec(), vec(),
                  pl.BlockSpec((None, n_heads, HEAD_DIM), lambda b, *_: (layer, 0, 0))],
        out_specs=[pl.BlockSpec((None, 1, a_width), lambda b, *_: (b, 0, 0)),
                   head_spec,
                   pl.BlockSpec((None, 1, a_width), lambda b, *_: (b, 0, 0))],
        scratch_shapes=[pltpu.VMEM((2, N_KV, N_SEL, SLC_BLOCK, HEAD_DIM), F32),
                        pltpu.VMEM((N_KV, 2, wb, HEAD_DIM), F32),
                        pltpu.SemaphoreType.DMA((2,))])
    return pl.pallas_call(
        functools.partial(_smix_kernel, layer=layer, pos=pos, hpg=hpg, nb_past=nb_past),
        grid_spec=grid_spec,
        out_shape=[jax.ShapeDtypeStruct((n_batch, 1, a_width), BF16),
                   jax.ShapeDtypeStruct((n_batch, n_heads, HEAD_DIM), BF16),
                   jax.ShapeDtypeStruct((n_batch, 1, a_width), F32)],
        compiler_params=_params(("arbitrary",), VMEM_LIMIT),
        name="mix_sample",
    )(page_table, idx, cache_slc, cache_win, q3, o_c, gt3, slc_new.reshape(n_batch, 1, KV_COLS),
      win_new.reshape(n_batch, 1, KV_COLS), rope_s, z3, z3,
      ln_g, ln_b, w0, b0, norm_a, norm_b3)


def _rope_tables(pos):
    inv = ROPE_THETA ** (-jnp.arange(ROPE_HALF, dtype=F32) / ROPE_HALF)
    ang = pos.astype(F32)[:, None] * inv[None, :]
    cos, sin = jnp.cos(ang), jnp.sin(ang)
    n = pos.shape[0]
    zeros = lambda w: jnp.zeros((n, w), F32)
    c = jnp.concatenate([cos, cos, jnp.ones((n, HEAD_DIM - ROPE_DIM), F32)], axis=1)
    sa = jnp.concatenate([-sin, zeros(HEAD_DIM - ROPE_HALF)], axis=1)
    sb = jnp.concatenate([zeros(ROPE_HALF), sin, zeros(HEAD_DIM - ROPE_DIM)], axis=1)
    return c, sa, sb


def _block_sum_matrix(n_cmp, n_lane):
    i = jnp.arange(n_cmp)[:, None]
    j = jnp.arange(n_lane)[None, :]
    return ((i // CMP_PER_SLC == j) & (i % CMP_PER_SLC < CMP_INSIDE)).astype(BF16)


def kernel(x_prompt, x_sample, cache_cmp, cache_slc, cache_win, page_table, c_prompt, c_sample, w_ada, b_ada, norm_pre_mix, norm_post_mix, norm_pre_ffn, norm_post_ffn, w_in, ln_v_g, ln_v_b, sgu_w, sgu_b, cmp_pe_k, cmp_pe_v, cmp_w1_k, cmp_w2_k, cmp_w1_v, cmp_w2_v, out_norm_a, out_norm_b, w_out, w_up, w_down):
    batch, seq, d = x_prompt.shape
    n_dec, dec_seq, _ = x_sample.shape
    depth = w_ada.shape[0]
    a_width = d // 2
    b_width = d - a_width
    groups = a_width // HEAD_DIM
    n_heads = b_width // HEAD_DIM
    hpg = n_heads // N_KV
    n_main = 2 * a_width + b_width + 3 * KV_COLS
    n_gate = 3 * n_heads
    n_pages = page_table.shape[1]
    past = n_pages * PAGE_SIZE
    assert dec_seq == 1 and n_dec % 8 == 0 and a_width == b_width and n_gate <= LANES
    assert (2 * a_width + b_width) % (3 * KV_COLS) == 0 and seq % 1024 == 0
    assert past % SLC_BLOCK == 0 and past // SLC_BLOCK + 1 > N_SEL and cache_win.shape[2] <= past
    q_block = 2 * a_width // b_width
    kv_block = (2 * a_width + b_width) // (3 * KV_COLS)
    cmp_block = (2 * a_width + b_width) // KV_COLS
    m_p = batch * seq

    vec3 = lambda a: a.reshape(depth, 1, a.shape[-1])
    w_main = w_in.astype(BF16)
    big = (w_out, w_up, w_down)
    wts = [w[0:1].astype(BF16) for w in big]
    w_gate = jnp.pad(w_in[:, :, n_main:], ((0, 0), (0, 0), (0, LANES - n_gate))).astype(BF16)
    cmp_w1 = jnp.stack([cmp_w1_k, cmp_w1_v], axis=1).astype(BF16)
    cmp_w2 = jnp.stack([cmp_w2_k, cmp_w2_v], axis=1).astype(BF16)
    cmp_pe = jnp.broadcast_to(jnp.stack([cmp_pe_k, cmp_pe_v], axis=1).reshape(depth, 2, 1, CMP_BLOCK * HEAD_DIM),
                              (depth, 2, 8, CMP_BLOCK * HEAD_DIM))
    cmp_pe = _cmp_bias(cmp_pe, cmp_w1)
    pre_mix, post_mix, pre_ffn, post_ffn = map(vec3, (norm_pre_mix, norm_post_mix, norm_pre_ffn, norm_post_ffn))
    ln_g, ln_b, norm_a, norm_b = map(vec3, (ln_v_g, ln_v_b, out_norm_a, out_norm_b))
    norm_b3 = out_norm_b.reshape(depth, n_heads, HEAD_DIM)
    sgu_bt = jnp.swapaxes(sgu_b, 1, 2)
    sgu_w0 = jnp.repeat(sgu_w[:, :, 0, 0], HEAD_DIM, axis=1).reshape(depth, 1, a_width)
    sgu_b0 = jnp.repeat(sgu_b[:, :, 0], HEAD_DIM, axis=1).reshape(depth, 1, a_width)

    mod_rows = -(-(n_dec + batch) // 8) * 8
    c_all = jnp.concatenate([c_sample, c_prompt, jnp.zeros((mod_rows - n_dec - batch, d), F32)], axis=0)
    mod = _ada(c_all, w_ada, b_ada)

    tables_p = _rope_tables(jnp.arange(seq))
    tables_s = _rope_tables(jnp.full((n_dec,), past))
    rope_s = jnp.concatenate([t[0:1] for t in tables_s] + [jnp.zeros((5, LANES), F32)], axis=0)
    n_cmp_p = seq // CMP_STRIDE
    sel_p = _block_sum_matrix(n_cmp_p, LANES)
    key_chunk = 512
    exp_p = (jnp.arange(LANES)[None, :, None]
             == (jnp.arange(seq) // SLC_BLOCK).reshape(seq // key_chunk, 1, key_chunk)).astype(BF16)
    n_cmp_s = past // CMP_STRIDE
    nb_lane = -(-(past // SLC_BLOCK + 1) // LANES) * LANES
    sel_s = _block_sum_matrix(n_cmp_s, nb_lane)

    y_p = x_prompt.reshape(m_p, d)
    y_s = x_sample.reshape(n_dec, d)
    caches_p = [jnp.zeros((depth, batch, rows, N_KV, 2, HEAD_DIM), F32) for rows in (seq, seq, min(WINDOW, seq))]
    outs = [[] for _ in range(4)]
    win_keep = min(WINDOW, seq)
    for l in range(depth):
        w_out_b, w_up_b, w_down_b = wts
        rows = _Rows(mod, l, d, m_p, 1024, seq, n_dec)
        z, gt = _in_proj(y_p, rows, l, pre_mix, w_main, w_gate, n_main)
        a_n = _sgu(z, l, a_width, ln_g, ln_b, sgu_w, sgu_bt, norm_a)
        *caches_p, slc_b, win_b = _kv_rope_prompt(z, kv_block, tables_p, 256, l, depth, batch, seq, win_keep,
                                                  caches_p)
        kvc = _compress_prompt(z, batch, seq, cmp_block, cmp_pe[l], cmp_w1[l], cmp_w2[l])
        b_n = _attn_prompt(z, gt, kvc, slc_b, win_b, tables_p, l, norm_b, sel_p, exp_p, batch, seq, b_width, q_block)
        y_p = _out_proj(a_n, b_n, y_p, _Rows(mod, l, d, m_p, 512, seq, n_dec), l, w_out_b, post_mix)
        y_p, wts_next = _ffn(y_p, rows, l, pre_ffn, post_ffn, w_up_b, w_down_b,
                             cast_next=big if l + 1 < depth else (), next_layer=l + 1)
        rows = _Rows(mod, l, d, n_dec, n_dec, None, 0)
        z, gt = _in_proj(y_s, rows, l, pre_mix, w_main, w_gate, n_main)
        kv_cmp, kv_slc, kv_win, _, _ = _kv_rope(z, kv_block, tables_s, n_dec)
        kvc = _compress_sample(page_table, cache_cmp, l, kv_cmp, cmp_pe[l], cmp_w1[l], cmp_w2[l])
        q3 = z[:, 2 * a_width:2 * a_width + b_width].reshape(n_dec, n_heads, HEAD_DIM)
        o_c, idx = _select_sample(q3, kvc, sel_s, past, hpg)
        a_n, b_n, v_n = _mix_sample(page_table, idx[:, :, :, 0], cache_slc, cache_win, q3, o_c,
                                    gt[:, :n_gate].reshape(n_dec, n_heads, 3), kv_slc, kv_win,
                                    rope_s, z.reshape(n_dec, 1, n_main), l, ln_g, ln_b, sgu_w0, sgu_b0,
                                    norm_a, norm_b3, past, hpg, a_width)
        y_s = _out_proj(a_n.reshape(n_dec, a_width), b_n.reshape(n_dec, b_width), y_s, rows, l, w_out_b, post_mix)
        y_s, _ = _ffn(y_s, rows, l, pre_ffn, post_ffn, w_up_b, w_down_b)
        wts = [w[None] for w in wts_next]
        outs[0].append(kv_cmp.reshape(n_dec, 1, N_KV, 2, HEAD_DIM))
        outs[1].append(kv_slc.reshape(n_dec, 1, N_KV, 2, HEAD_DIM))
        outs[2].append(kv_win.reshape(n_dec, 1, N_KV, 2, HEAD_DIM))
        outs[3].append(v_n)
    return (y_p.reshape(batch, seq, d), y_s.reshape(n_dec, 1, d), *caches_p, *[jnp.stack(o) for o in outs])
```

```python
import functools

import jax
import jax.numpy as jnp
from jax import lax
from jax.experimental import pallas as pl
from jax.experimental.pallas import tpu as pltpu

F32 = jnp.float32
BF16 = jnp.bfloat16
I32 = jnp.int32

LANES = 128
ROW_STRIP = 16
HEAD_DIM = 128
N_KV = 2
PAGE_SIZE = 128
CHUNK = 128
CMP_BLOCK = 32
CMP_STRIDE = 16
SLC_BLOCK = 64
N_SEL = 16
WINDOW = 512
ROPE_DIM = HEAD_DIM // 4
ROPE_HALF = ROPE_DIM // 2
ROPE_THETA = 500000.0
EPS = 1e-6
FORCE = 1e4
SCALE = HEAD_DIM ** -0.5
EXP2_SCALE = SCALE * 1.4426950408889634
KV_COLS = N_KV * 2 * HEAD_DIM
CMP_STRIPS = CMP_BLOCK // CMP_STRIDE
STRIP_COLS = CMP_STRIDE * HEAD_DIM
CMP_PER_SLC = SLC_BLOCK // CMP_STRIDE
CMP_INSIDE = (SLC_BLOCK - CMP_BLOCK) // CMP_STRIDE + 1
VMEM_LIMIT = 56 * 2 ** 20


def _dot(a, b):
    return jnp.dot(a, b, preferred_element_type=F32)


def _dot_nt(a, b):
    return lax.dot_general(a, b, (((1,), (1,)), ((), ())), preferred_element_type=F32)


def _gelu(x):
    return 0.5 * x * (1.0 + jnp.tanh(0.7978845608028654 * (x + 0.044715 * (x * x * x))))


def _sigmoid(x):
    return 1.0 / (1.0 + jnp.exp(-x))


def _rms(x, g):
    return x * lax.rsqrt(jnp.mean(x * x, axis=-1, keepdims=True) + EPS) * g


def _softmax_neg_inf(s):
    m = jnp.max(s, axis=-1, keepdims=True)
    m = jnp.where(m > -jnp.inf, m, 0.0)
    e = jnp.exp(s - m)
    return e / jnp.maximum(jnp.sum(e, axis=-1, keepdims=True), 1e-30)


def _msoftmax(s, mask):
    return _softmax_neg_inf(jnp.where(mask, s, -jnp.inf))


def _attend(s, v):
    m = jnp.max(s, axis=-1, keepdims=True)
    m = jnp.where(m > -jnp.inf, m, 0.0)
    e = jnp.exp2(s - m)
    return _dot(e.astype(BF16), v) / jnp.maximum(jnp.sum(e, axis=-1, keepdims=True), 1e-30)


def _rope(x, c, sa, sb):
    return x * c + pltpu.roll(x, LANES - ROPE_HALF, axis=1) * sa + pltpu.roll(x, ROPE_HALF, axis=1) * sb


def _split_dot(x, m):
    hi = x.astype(BF16)
    r1 = x - hi.astype(F32)
    mid = r1.astype(BF16)
    lo = (r1 - mid.astype(F32)).astype(BF16)
    return _dot(hi, m) + _dot(mid, m) + _dot(lo, m)


def _row_strips(ref):
    n = ref.shape[0]
    step = ROW_STRIP if n % ROW_STRIP == 0 else n
    return [slice(r, r + step) for r in range(0, n, step)]


def _mod_rows(ref, rs):
    return ref[...] if ref.shape[0] == 1 else ref[rs, :]


def _modulated_norm(x_ref, g_ref, sc_ref, sh_ref, h_ref):
    for rs in _row_strips(x_ref):
        h = _rms(x_ref[rs, :], g_ref[...]) * (1.0 + _mod_rows(sc_ref, rs)) + _mod_rows(sh_ref, rs)
        h_ref[rs, :] = h.astype(h_ref.dtype)


def _gated_residual(x_ref, gate_ref, g_ref, o_ref):
    for rs in _row_strips(x_ref):
        o_ref[rs, :] = x_ref[rs, :] + _mod_rows(gate_ref, rs) * _rms(o_ref[rs, :], g_ref[...])


def _params(sem, vmem=None):
    return pltpu.CompilerParams(dimension_semantics=sem, vmem_limit_bytes=vmem)


def _ada_kernel(c_ref, w_ref, b_ref, o_ref):
    c = c_ref[...]
    o_ref[...] = _dot((c * _sigmoid(c)).astype(BF16), w_ref[...].astype(BF16)) + b_ref[...]


def _ada(c_all, w_ada, b_ada):
    depth, d, n = w_ada.shape
    rows = c_all.shape[0]
    tn = 1024
    return pl.pallas_call(
        _ada_kernel,
        grid=(depth, n // tn),
        in_specs=[pl.BlockSpec((rows, d), lambda l, j: (0, 0)),
                  pl.BlockSpec((None, d, tn), lambda l, j: (l, 0, j)),
                  pl.BlockSpec((None, 1, tn), lambda l, j: (l, 0, j))],
        out_specs=pl.BlockSpec((None, rows, tn), lambda l, j: (l, 0, j)),
        out_shape=jax.ShapeDtypeStruct((depth, rows, n), F32),
        compiler_params=_params(("arbitrary", "arbitrary"), VMEM_LIMIT),
        name="ada",
    )(c_all, w_ada, b_ada.reshape(depth, 1, n))


class _Rows:
    def __init__(self, mod, layer, d, m, tm, rows_per_batch, mod_row0):
        self.m, self.tm, self.d = m, tm, d
        depth, r, n = mod.shape
        if rows_per_batch is None:
            self.mod = mod
            self._spec = lambda k: pl.BlockSpec((None, tm, d), lambda i, *_: (layer, mod_row0 // tm, k))
        else:
            tiles = rows_per_batch // tm
            self.mod = mod.reshape(depth, r, 1, n)
            self._spec = lambda k: pl.BlockSpec((None, None, 1, d),
                                                lambda i, *_: (layer, mod_row0 + i // tiles, 0, k))

    def mod_spec(self, k):
        return self._spec(k)


def _vec_spec(layer, n):
    return pl.BlockSpec((None, 1, n), lambda *_: (layer, 0, 0))


def _in_kernel(x_ref, g_ref, sc_ref, sh_ref, w_ref, wg_ref, z_ref, gt_ref, h_ref):
    @pl.when(pl.program_id(1) == 0)
    def _():
        _modulated_norm(x_ref, g_ref, sc_ref, sh_ref, h_ref)
        gt_ref[...] = _dot(h_ref[...].astype(BF16), wg_ref[...])

    z_ref[...] = _dot(h_ref[...].astype(BF16), w_ref[...])


def _in_proj(x, rows, layer, norm_w, w_main, w_gate, n):
    m, tm, d = rows.m, rows.tm, rows.d
    tn = 768 if n % 768 == 0 else 512
    h_dtype = BF16 if tm % 16 == 0 else F32
    return pl.pallas_call(
        _in_kernel,
        grid=(m // tm, n // tn),
        in_specs=[pl.BlockSpec((tm, d), lambda i, j: (i, 0)),
                  _vec_spec(layer, d), rows.mod_spec(1), rows.mod_spec(0),
                  pl.BlockSpec((None, d, tn), lambda i, j: (layer, 0, j)),
                  pl.BlockSpec((None, d, LANES), lambda i, j: (layer, 0, 0))],
        out_specs=[pl.BlockSpec((tm, tn), lambda i, j: (i, j)),
                   pl.BlockSpec((tm, LANES), lambda i, j: (i, 0))],
        out_shape=[jax.ShapeDtypeStruct((m, n), F32), jax.ShapeDtypeStruct((m, LANES), F32)],
        scratch_shapes=[pltpu.VMEM((tm, d), h_dtype)],
        compiler_params=_params(("arbitrary", "arbitrary"), VMEM_LIMIT),
        name="in_proj",
    )(x, norm_w, rows.mod, rows.mod, w_main, w_gate)


def _layernorm(v, g, b):
    mu = jnp.mean(v, axis=-1, keepdims=True)
    vc = v - mu
    return vc * lax.rsqrt(jnp.mean(vc * vc, axis=-1, keepdims=True) + EPS) * g + b


def _sgu_kernel(u_ref, v_ref, lg_ref, lb_ref, w_ref, bt_ref, na_ref, o_ref, *, groups, chunks):
    row = lax.broadcasted_iota(I32, (CHUNK, CHUNK), 0)
    col = lax.broadcasted_iota(I32, (CHUNK, CHUNK), 1)
    ws = [jnp.where(row >= col, w_ref[g], 0.0).astype(BF16) for g in range(groups)]
    for c in range(chunks):
        rs = slice(c * CHUNK, (c + 1) * CHUNK)
        vn = _layernorm(_gelu(v_ref[rs, :]), lg_ref[...], lb_ref[...]).astype(BF16)
        u = _gelu(u_ref[rs, :])
        outs = []
        for g in range(groups):
            cs = slice(g * HEAD_DIM, (g + 1) * HEAD_DIM)
            outs.append(u[:, cs] * (_dot(ws[g], vn[:, cs]) + bt_ref[:, g:g + 1]))
        o_ref[rs, :] = _rms(jnp.concatenate(outs, axis=1), na_ref[...]).astype(BF16)


def _sgu(z, layer, a_width, ln_g, ln_b, sgu_w, sgu_bt, norm_a):
    m = z.shape[0]
    groups = a_width // HEAD_DIM
    chunks = 2
    tr = chunks * CHUNK
    return pl.pallas_call(
        functools.partial(_sgu_kernel, groups=groups, chunks=chunks),
        grid=(m // tr,),
        in_specs=[pl.BlockSpec((tr, a_width), lambda i: (i, 0)),
                  pl.BlockSpec((tr, a_width), lambda i: (i, 1)),
                  _vec_spec(layer, a_width), _vec_spec(layer, a_width),
                  pl.BlockSpec((None, groups, CHUNK, CHUNK), lambda i: (layer, 0, 0, 0)),
                  pl.BlockSpec((None, CHUNK, groups), lambda i: (layer, 0, 0)),
                  _vec_spec(layer, a_width)],
        out_specs=pl.BlockSpec((tr, a_width), lambda i: (i, 0)),
        out_shape=jax.ShapeDtypeStruct((m, a_width), BF16),
        compiler_params=_params(("arbitrary",), VMEM_LIMIT),
        name="sgu",
    )(z, z, ln_g, ln_b, sgu_w, sgu_bt, norm_a)


def _kv_kernel(kv_ref, rc_ref, rsa_ref, rsb_ref, cmp_ref, slc_ref, win_ref, slcb_ref, winb_ref):
    cmp_ref[...] = kv_ref[:, 0:KV_COLS]
    rc, rsa, rsb = rc_ref[...], rsa_ref[...], rsb_ref[...]
    for t, (o_ref, ob_ref) in enumerate(((slc_ref, slcb_ref), (win_ref, winb_ref))):
        for g in range(N_KV):
            k0 = (t + 1) * KV_COLS + g * 2 * HEAD_DIM
            k = _rope(kv_ref[:, k0:k0 + HEAD_DIM], rc, rsa, rsb)
            v = kv_ref[:, k0 + HEAD_DIM:k0 + 2 * HEAD_DIM]
            o0 = g * 2 * HEAD_DIM
            o_ref[:, o0:o0 + HEAD_DIM] = k
            o_ref[:, o0 + HEAD_DIM:o0 + 2 * HEAD_DIM] = v
            ob_ref[:, o0:o0 + HEAD_DIM] = k.astype(BF16)
            ob_ref[:, o0 + HEAD_DIM:o0 + 2 * HEAD_DIM] = v.astype(BF16)


def _kv_cache_kernel(kv_ref, rc_ref, rsa_ref, rsb_ref, *refs):
    cmp_ref, slc_ref, win_ref, slcb_ref, winb_ref = refs[-5:]
    rc, rsa, rsb = rc_ref[...], rsa_ref[...], rsb_ref[...]
    for g in range(N_KV):
        for t in range(2):
            c0 = (g * 2 + t) * HEAD_DIM
            cmp_ref[:, g, t, :] = kv_ref[:, c0:c0 + HEAD_DIM]
    for j, (o_ref, ob_ref) in enumerate(((slc_ref, slcb_ref), (win_ref, winb_ref))):
        for g in range(N_KV):
            k0 = (j + 1) * KV_COLS + g * 2 * HEAD_DIM
            k = _rope(kv_ref[:, k0:k0 + HEAD_DIM], rc, rsa, rsb)
            v = kv_ref[:, k0 + HEAD_DIM:k0 + 2 * HEAD_DIM]
            o0 = g * 2 * HEAD_DIM
            o_ref[:, g, 0, :] = k
            o_ref[:, g, 1, :] = v
            ob_ref[:, o0:o0 + HEAD_DIM] = k.astype(BF16)
            ob_ref[:, o0 + HEAD_DIM:o0 + 2 * HEAD_DIM] = v.astype(BF16)


def _kv_rope_prompt(z, kv_block, tables, tr, layer, depth, batch, seq, win_keep, prev):
    m = z.shape[0]
    nt = seq // tr
    first = (seq - win_keep) // tr
    tspec = pl.BlockSpec((tr, LANES), lambda i: (i % nt, 0))
    bspec = pl.BlockSpec((tr, KV_COLS), lambda i: (i, 0))
    cache_block = (None, None, tr, N_KV, 2, HEAD_DIM)
    cspec = pl.BlockSpec(cache_block, lambda i: (layer, i // nt, i % nt, 0, 0, 0))
    wspec = pl.BlockSpec(cache_block, lambda i: (layer, i // nt, jnp.maximum(i % nt - first, 0), 0, 0, 0))
    cache_shape = lambda rows: jax.ShapeDtypeStruct((depth, batch, rows, N_KV, 2, HEAD_DIM), F32)
    return pl.pallas_call(
        _kv_cache_kernel,
        grid=(m // tr,),
        in_specs=[pl.BlockSpec((tr, 3 * KV_COLS), lambda i: (i, kv_block)), tspec, tspec, tspec]
        + [pl.BlockSpec(memory_space=pl.ANY)] * len(prev),
        out_specs=[cspec, cspec, wspec, bspec, bspec],
        out_shape=[cache_shape(seq), cache_shape(seq), cache_shape(win_keep)]
        + [jax.ShapeDtypeStruct((m, KV_COLS), BF16)] * 2,
        input_output_aliases={4 + i: i for i in range(len(prev))},
        compiler_params=_params(("arbitrary",), VMEM_LIMIT),
        name="kv_rope_prompt",
    )(z, *tables, *prev)


def _kv_rope(z, kv_block, tables, tr):
    m = z.shape[0]
    t_rows = tables[0].shape[0]
    nt = t_rows // tr
    tspec = pl.BlockSpec((tr, LANES), lambda i: (i % nt, 0))
    ospec = pl.BlockSpec((tr, KV_COLS), lambda i: (i, 0))
    return pl.pallas_call(
        _kv_kernel,
        grid=(m // tr,),
        in_specs=[pl.BlockSpec((tr, 3 * KV_COLS), lambda i: (i, kv_block)), tspec, tspec, tspec],
        out_specs=[ospec] * 5,
        out_shape=[jax.ShapeDtypeStruct((m, KV_COLS), F32)] * 3 + [jax.ShapeDtypeStruct((m, KV_COLS), BF16)] * 2,
        compiler_params=_params(("arbitrary",), VMEM_LIMIT),
        name="kv_rope",
    )(z, *tables)


def _cmp_bias_kernel(pe_ref, w1_ref, o_ref):
    o_ref[...] = _dot(pe_ref[...].astype(BF16), w1_ref[...])


def _cmp_bias(pe, w1):
    depth, _, _, k = pe.shape
    hidden = w1.shape[-1]
    return pl.pallas_call(
        _cmp_bias_kernel,
        grid=(depth, 2),
        in_specs=[pl.BlockSpec((None, None, 8, k), lambda l, s: (l, s, 0, 0)),
                  pl.BlockSpec((None, None, k, hidden), lambda l, s: (l, s, 0, 0))],
        out_specs=pl.BlockSpec((None, None, 8, hidden), lambda l, s: (l, s, 0, 0)),
        out_shape=jax.ShapeDtypeStruct((depth, 2, 8, hidden), F32),
        compiler_params=_params(("arbitrary", "arbitrary"), VMEM_LIMIT),
        name="cmp_bias",
    )(pe, w1)


def _compress_strips(xf, w1_ref, sel):
    xb = xf.astype(BF16)
    return _dot(xb, w1_ref[sel, 0:STRIP_COLS, :]), _dot(xb, w1_ref[sel, STRIP_COLS:2 * STRIP_COLS, :])


def _cmp_kernel(x_ref, pe_ref, w1_ref, w2_ref, o_ref, *, n):
    sel = pl.program_id(1) % 2
    xf = jnp.concatenate([x_ref[pl.ds(r, n, stride=CMP_STRIDE), :] for r in range(CMP_STRIDE)], axis=1)
    p, q = _compress_strips(xf, w1_ref, sel)
    hid = _gelu(p + pltpu.roll(q, n - 1, axis=0) + pe_ref[sel, 0:1, :])
    o_ref[...] = _dot(hid.astype(BF16), w2_ref[sel]).astype(BF16)


def _compress_prompt(z, batch, seq, cmp_block, pe2, w1, w2):
    n = seq // CMP_STRIDE
    hidden = w1.shape[-1]
    heads = 2 * N_KV
    return pl.pallas_call(
        functools.partial(_cmp_kernel, n=n),
        grid=(batch, heads),
        in_specs=[pl.BlockSpec((seq, HEAD_DIM), lambda b, h: (b, cmp_block * heads + h)),
                  pl.BlockSpec((2, 8, hidden), lambda b, h: (0, 0, 0)),
                  pl.BlockSpec((2, 2 * STRIP_COLS, hidden), lambda b, h: (0, 0, 0)),
                  pl.BlockSpec((2, hidden, HEAD_DIM), lambda b, h: (0, 0, 0))],
        out_specs=pl.BlockSpec((None, None, n, HEAD_DIM), lambda b, h: (b, h, 0, 0)),
        out_shape=jax.ShapeDtypeStruct((batch, heads, n, HEAD_DIM), BF16),
        compiler_params=_params(("arbitrary", "arbitrary"), VMEM_LIMIT),
        name="compress_prompt",
    )(z, pe2, w1, w2)


def _attn_kernel(q_ref, gt_ref, cmp_ref, slc_ref, win_ref, rc_ref, rsa_ref, rsb_ref, nb_ref, sel_ref, exp_ref,
                 o_ref, acc_ref, *, tq, seq, hpg, n_cmp, kc):
    q0 = pl.program_id(1) * tq
    n_chunks = (q0 + tq + kc - 1) // kc
    kiota = lax.broadcasted_iota(I32, (1, kc), 1)
    pos = q0 + lax.broadcasted_iota(I32, (tq, 1), 0)
    posh = jnp.concatenate([pos] * hpg, axis=0)
    rc = jnp.concatenate([rc_ref[...]] * hpg, axis=0)
    rsa = jnp.concatenate([rsa_ref[...]] * hpg, axis=0)
    rsb = jnp.concatenate([rsb_ref[...]] * hpg, axis=0)
    sig = _sigmoid(gt_ref[...])
    clane = lax.broadcasted_iota(I32, (1, n_cmp), 1)
    n_blk = seq // SLC_BLOCK
    n_sel = min(N_SEL, n_blk)
    nb8 = -(-n_blk // 8) * 8
    brow = lax.broadcasted_iota(I32, (nb8, 1), 0)
    cur_t = (q0 + lax.broadcasted_iota(I32, (1, tq), 1)) // SLC_BLOCK
    forced_t = (brow == 0) | (brow == cur_t) | (brow == cur_t - 1)
    brow_q = lax.broadcasted_iota(I32, (nb8, tq), 0)
    wk = WINDOW + tq
    wstart = pl.multiple_of(jnp.maximum(q0 - WINDOW, 0), LANES)
    wpos = wstart + lax.broadcasted_iota(I32, (1, wk), 1)
    wdiff = posh - wpos
    wmask = (wdiff >= 0) & (wdiff < WINDOW)
    cmask = (clane * CMP_STRIDE + (CMP_BLOCK - 1)) <= posh
    for g in range(N_KV):
        heads = [g * hpg + h for h in range(hpg)]
        qc = jnp.concatenate([q_ref[:, hd * HEAD_DIM:(hd + 1) * HEAD_DIM] for hd in heads], axis=0)
        qrb = (_rope(qc, rc, rsa, rsb) * EXP2_SCALE).astype(BF16)
        k0 = g * 2 * HEAD_DIM
        v0 = k0 + HEAD_DIM
        p = _msoftmax(_dot_nt(qc.astype(BF16), cmp_ref[2 * g]) * SCALE, cmask)
        o_c = _dot(p.astype(BF16), cmp_ref[2 * g + 1])
        imp = p[0:tq]
        for h in range(1, hpg):
            imp = imp + p[h * tq:(h + 1) * tq]
        impb = _split_dot(imp, sel_ref[...])
        score = jnp.where(brow <= cur_t, jnp.where(forced_t, FORCE, impb.T[0:nb8]), -FORCE)
        rank = jnp.zeros((nb8, tq), F32)
        for i in range(n_blk):
            ci = score[i:i + 1, :]
            tie = jnp.where(brow_q > i, 1.0, 0.0)
            rank = rank + jnp.where(ci > score, 1.0, jnp.where(ci == score, tie, 0.0))
        chosen_t = jnp.where((rank < float(n_sel)) & (brow <= cur_t), 1.0, 0.0)
        chosen_b = jnp.concatenate([chosen_t, jnp.zeros((LANES - nb8, tq), F32)], axis=0).T.astype(BF16)
        rows = hpg * tq

        def chunk(c, carry, diagonal):
            m, l, acc = carry
            r0 = pl.multiple_of(c * kc, kc)
            open_key = _dot(chosen_b, exp_ref[c]) > 0.5
            if diagonal:
                open_key = open_key & ((r0 + kiota) <= pos)
            bias = jnp.where(open_key, 0.0, -jnp.inf)
            s = _dot_nt(qrb, slc_ref[pl.ds(r0, kc), k0:k0 + HEAD_DIM]).reshape(hpg, tq, kc) + bias[None]
            m_new = jnp.maximum(m, jnp.max(s, axis=-1, keepdims=True))
            m_ref = jnp.where(m_new > -jnp.inf, m_new, 0.0)
            alpha = jnp.exp2(m - m_ref)
            e = jnp.exp2(s - m_ref)
            l = alpha * l + jnp.sum(e, axis=-1, keepdims=True)
            pv = _dot(e.reshape(rows, kc).astype(BF16), slc_ref[pl.ds(r0, kc), v0:v0 + HEAD_DIM])
            return m_new, l, alpha.reshape(rows, 1) * acc + pv

        init = (jnp.full((hpg, tq, 1), -jnp.inf, F32), jnp.zeros((hpg, tq, 1), F32),
                jnp.zeros((rows, HEAD_DIM), F32))
        carry = lax.fori_loop(0, n_chunks - 1, functools.partial(chunk, diagonal=False), init)
        _, l, acc = chunk(n_chunks - 1, carry, diagonal=True)
        o_s = acc / jnp.maximum(l.reshape(rows, 1), 1e-30)
        s = jnp.where(wmask, _dot_nt(qrb, win_ref[pl.ds(wstart, wk), k0:k0 + HEAD_DIM]), -jnp.inf)
        o_w = _attend(s, win_ref[pl.ds(wstart, wk), v0:v0 + HEAD_DIM])
        for h, hd in enumerate(heads):
            rs = slice(h * tq, (h + 1) * tq)
            acc_ref[:, hd * HEAD_DIM:(hd + 1) * HEAD_DIM] = (
                sig[:, 3 * hd:3 * hd + 1] * o_c[rs] + sig[:, 3 * hd + 1:3 * hd + 2] * o_s[rs]
                + sig[:, 3 * hd + 2:3 * hd + 3] * o_w[rs])
    o_ref[...] = _rms(acc_ref[...], nb_ref[...]).astype(BF16)


def _attn_prompt(z, gt, kvc, slc_b, win_b, tables, layer, norm_b, sel_m, exp_m, batch, seq, b_width, q_block):
    tq = 256
    nq = seq // tq
    hpg = b_width // HEAD_DIM // N_KV
    n_cmp = kvc.shape[2]
    kc = exp_m.shape[2]
    tspec = pl.BlockSpec((tq, LANES), lambda b, i: (i, 0))
    kvspec = pl.BlockSpec((seq, KV_COLS), lambda b, i: (b, 0))
    return pl.pallas_call(
        functools.partial(_attn_kernel, tq=tq, seq=seq, hpg=hpg, n_cmp=n_cmp, kc=kc),
        grid=(batch, nq),
        in_specs=[pl.BlockSpec((tq, b_width), lambda b, i: (b * nq + i, q_block)),
                  pl.BlockSpec((tq, LANES), lambda b, i: (b * nq + i, 0)),
                  pl.BlockSpec((None, 2 * N_KV, n_cmp, HEAD_DIM), lambda b, i: (b, 0, 0, 0)),
                  kvspec, kvspec, tspec, tspec, tspec,
                  pl.BlockSpec((None, 1, b_width), lambda b, i: (layer, 0, 0)),
                  pl.BlockSpec(sel_m.shape, lambda b, i: (0, 0)),
                  pl.BlockSpec(exp_m.shape, lambda b, i: (0, 0, 0))],
        out_specs=pl.BlockSpec((tq, b_width), lambda b, i: (b * nq + i, 0)),
        out_shape=jax.ShapeDtypeStruct((batch * seq, b_width), BF16),
        scratch_shapes=[pltpu.VMEM((tq, b_width), F32)],
        compiler_params=_params(("arbitrary", "arbitrary"), VMEM_LIMIT),
        name="attn_prompt",
    )(z, gt, kvc, slc_b, win_b, *tables, norm_b, sel_m, exp_m)


def _out_kernel(a_ref, b_ref, w_ref, x_ref, gate_ref, g_ref, o_ref, *, a_width):
    o_ref[...] = _dot(a_ref[...], w_ref[0:a_width, :]) + _dot(b_ref[...], w_ref[a_width:, :])
    _gated_residual(x_ref, gate_ref, g_ref, o_ref)


def _out_proj(a_n, b_n, x, rows, layer, w_out, norm_w):
    m, tm, d = rows.m, rows.tm, rows.d
    a_width = a_n.shape[1]
    b_width = b_n.shape[1]
    return pl.pallas_call(
        functools.partial(_out_kernel, a_width=a_width),
        grid=(m // tm,),
        in_specs=[pl.BlockSpec((tm, a_width), lambda i: (i, 0)),
                  pl.BlockSpec((tm, b_width), lambda i: (i, 0)),
                  pl.BlockSpec((None, a_width + b_width, d), lambda i: (0, 0, 0)),
                  pl.BlockSpec((tm, d), lambda i: (i, 0)),
                  rows.mod_spec(2), _vec_spec(layer, d)],
        out_specs=pl.BlockSpec((tm, d), lambda i: (i, 0)),
        out_shape=jax.ShapeDtypeStruct((m, d), F32),
        compiler_params=_params(("arbitrary",), VMEM_LIMIT),
        name="out_proj",
    )(a_n, b_n, w_out, x, rows.mod, norm_w)


def _ffn_kernel(x_ref, g1_ref, sc_ref, sh_ref, wu_ref, wd_ref, gate_ref, g2_ref, *refs, n_cast):
    cast_in, o_ref, cast_out, h_ref = refs[:n_cast], refs[n_cast], refs[n_cast + 1:2 * n_cast + 1], refs[-1]
    k = pl.program_id(1)
    for src, dst in zip(cast_in, cast_out):
        dst[...] = src[...].astype(BF16)

    @pl.when(k == 0)
    def _():
        _modulated_norm(x_ref, g1_ref, sc_ref, sh_ref, h_ref)
        o_ref[...] = jnp.zeros_like(o_ref)

    up = jnp.maximum(_dot(h_ref[...].astype(BF16), wu_ref[...]), 0.0)
    o_ref[...] += _dot((up * up).astype(BF16), wd_ref[...])

    @pl.when(k == pl.num_programs(1) - 1)
    def _():
        _gated_residual(x_ref, gate_ref, g2_ref, o_ref)


def _ffn(x, rows, layer, norm_pre, norm_post, w_up, w_down, cast_next=(), next_layer=0):
    m, tm, d = rows.m, rows.tm, rows.d
    d_ff = w_up.shape[-1]
    tf = 512
    ni, nk = m // tm, d_ff // tf
    h_dtype = BF16 if tm % 16 == 0 else F32
    x_mode = dict(pipeline_mode=pl.Buffered(1)) if tm * d * 4 >= 8 * 2 ** 20 else {}
    cast_in, cast_out, cast_shape = [], [], []
    for w in cast_next:
        r, c = w.shape[1:]
        band = r // (ni * nk)
        assert band * ni * nk == r and band % 16 == 0
        cast_in.append(pl.BlockSpec((None, band, c), lambda i, k: (next_layer, i * nk + k, 0)))
        cast_out.append(pl.BlockSpec((band, c), lambda i, k: (i * nk + k, 0)))
        cast_shape.append(jax.ShapeDtypeStruct((r, c), BF16))
    y, *casts = pl.pallas_call(
        functools.partial(_ffn_kernel, n_cast=len(cast_next)),
        grid=(ni, nk),
        in_specs=[pl.BlockSpec((tm, d), lambda i, k: (i, 0), **x_mode),
                  _vec_spec(layer, d), rows.mod_spec(4), rows.mod_spec(3),
                  pl.BlockSpec((None, d, tf), lambda i, k: (0, 0, k)),
                  pl.BlockSpec((None, tf, d), lambda i, k: (0, k, 0)),
                  rows.mod_spec(5), _vec_spec(layer, d)] + cast_in,
        out_specs=[pl.BlockSpec((tm, d), lambda i, k: (i, 0))] + cast_out,
        out_shape=[jax.ShapeDtypeStruct((m, d), F32)] + cast_shape,
        scratch_shapes=[pltpu.VMEM((tm, d), h_dtype)],
        compiler_params=_params(("arbitrary", "arbitrary"), VMEM_LIMIT),
        name="ffn",
    )(x, norm_pre, rows.mod, rows.mod, w_up, w_down, rows.mod, norm_post, *cast_next)
    return y, casts


def _scmp_kernel(pt_ref, cache_ref, new_ref, pe_ref, w1_ref, w2_ref, o_ref, buf, sem, p_s, q_s,
                 *, layer, pages, n_chunk, n_batch):
    b = pl.program_id(0)
    c = pl.program_id(1)
    step = b * n_chunk + c
    slot = step % 2
    strips = pages * (PAGE_SIZE // CMP_STRIDE)
    n_past = n_chunk * strips

    def copies(bb, cc, sl):
        return [pltpu.make_async_copy(cache_ref.at[layer, pt_ref[bb, cc * pages + p], :, gs // 2, gs % 2, :],
                                      buf.at[sl, gs, p], sem.at[sl])
                for p in range(pages) for gs in range(2 * N_KV)]

    @pl.when(step == 0)
    def _():
        for n, cp in enumerate(copies(0, 0, 0)):
            cp.start(priority=n % 2)

    @pl.when(step + 1 < n_batch * n_chunk)
    def _():
        nxt = step + 1
        for n, cp in enumerate(copies(nxt // n_chunk, nxt % n_chunk, 1 - slot)):
            cp.start(priority=n % 2)

    for cp in copies(b, c, slot):
        cp.wait()

    cur = buf.at[slot]
    row0 = pl.multiple_of(c * strips, strips)
    for gs in range(2 * N_KV):
        sel = gs % 2
        xf = jnp.concatenate(
            [cur[gs, :, pl.ds(r, PAGE_SIZE // CMP_STRIDE, stride=CMP_STRIDE), :].reshape(strips, HEAD_DIM)
             for r in range(CMP_STRIDE)], axis=1)
        p, q = _compress_strips(xf, w1_ref, sel)
        p_s[gs, pl.ds(row0, strips), :] = p
        q_s[gs, pl.ds(row0, strips), :] = q

    @pl.when(c == n_chunk - 1)
    def _():
        for gs in range(2 * N_KV):
            sel = gs % 2
            new = new_ref[:, gs * HEAD_DIM:(gs + 1) * HEAD_DIM]
            xn = jnp.concatenate([new, jnp.zeros((1, STRIP_COLS - HEAD_DIM), F32)], axis=1)
            xn = jnp.broadcast_to(xn, (8, STRIP_COLS)).astype(BF16)
            q_s[gs, n_past:n_past + 8, :] = _dot(xn, w1_ref[sel, STRIP_COLS:2 * STRIP_COLS, :])
            hid = _gelu(p_s[gs] + q_s[gs, 1:n_past + 1, :] + pe_ref[sel, 0:1, :])
            o_ref[gs] = _dot(hid.astype(BF16), w2_ref[sel]).astype(BF16)


def _compress_sample(page_table, cache, layer, new_cmp, pe2, w1, w2):
    n_batch, n_pages = page_table.shape
    pages = 32 if n_pages % 32 == 0 else n_pages
    n_chunk = n_pages // pages
    n_past = n_pages * (PAGE_SIZE // CMP_STRIDE)
    hidden = w1.shape[-1]
    grid_spec = pltpu.PrefetchScalarGridSpec(
        num_scalar_prefetch=1,
        grid=(n_batch, n_chunk),
        in_specs=[pl.BlockSpec(memory_space=pl.ANY),
                  pl.BlockSpec((None, 1, KV_COLS), lambda b, c, pt: (b, 0, 0)),
                  pl.BlockSpec((2, 8, hidden), lambda b, c, pt: (0, 0, 0)),
                  pl.BlockSpec((2, 2 * STRIP_COLS, hidden), lambda b, c, pt: (0, 0, 0)),
                  pl.BlockSpec((2, hidden, HEAD_DIM), lambda b, c, pt: (0, 0, 0))],
        out_specs=pl.BlockSpec((None, 2 * N_KV, n_past, HEAD_DIM), lambda b, c, pt: (b, 0, 0, 0)),
        scratch_shapes=[pltpu.VMEM((2, 2 * N_KV, pages, PAGE_SIZE, HEAD_DIM), F32),
                        pltpu.SemaphoreType.DMA((2,)),
                        pltpu.VMEM((2 * N_KV, n_past, hidden), F32),
                        pltpu.VMEM((2 * N_KV, n_past + 8, hidden), F32)])
    return pl.pallas_call(
        functools.partial(_scmp_kernel, layer=layer, pages=pages, n_chunk=n_chunk, n_batch=n_batch),
        grid_spec=grid_spec,
        out_shape=jax.ShapeDtypeStruct((n_batch, 2 * N_KV, n_past, HEAD_DIM), BF16),
        compiler_params=_params(("arbitrary", "arbitrary"), VMEM_LIMIT),
        name="compress_sample",
    )(page_table, cache, new_cmp.reshape(n_batch, 1, KV_COLS), pe2, w1, w2)


def _ssel_kernel(q_ref, kvc_ref, sel_ref, oc_ref, idx_ref, *, pos, hpg, n_cmp, n_lane):
    qb = q_ref[...].astype(BF16)
    n_heads = qb.shape[0]
    head_row = lax.broadcasted_iota(I32, (n_heads, 1), 0)
    clane = lax.broadcasted_iota(I32, (1, n_cmp), 1)
    cmask = (clane * CMP_STRIDE + (CMP_BLOCK - 1)) <= pos
    lane = lax.broadcasted_iota(I32, (1, n_lane), 1)
    cur = pos // SLC_BLOCK
    forced = (lane == 0) | (lane == cur) | (lane == cur - 1)
    ri = lax.broadcasted_iota(I32, (n_lane, n_lane), 0)
    ci = lax.broadcasted_iota(I32, (n_lane, n_lane), 1)
    slot_id = lax.broadcasted_iota(I32, (N_SEL, 1), 0)
    o_c = jnp.zeros((n_heads, HEAD_DIM), F32)
    for g in range(N_KV):
        mine = (head_row // hpg) == g
        p = _msoftmax(_dot_nt(qb, kvc_ref[2 * g]) * SCALE, cmask)
        o_c = jnp.where(mine, _dot(p.astype(BF16), kvc_ref[2 * g + 1]), o_c)
        imp = jnp.sum(jnp.where(mine, p, 0.0), axis=0, keepdims=True)
        impb = _split_dot(jnp.broadcast_to(imp, (8, n_cmp)), sel_ref[...])[0:1]
        score = jnp.where(lane <= cur, jnp.where(forced, FORCE, impb), -FORCE)
        srow = jnp.broadcast_to(score, (n_lane, n_lane))
        scol = srow.T
        ahead = (scol > srow) | ((scol == srow) & (ri < ci))
        rank = jnp.sum(jnp.where(ahead, 1.0, 0.0), axis=0, keepdims=True)
        chosen = jnp.where((rank < float(N_SEL)) & (lane <= cur), 1.0, 0.0)
        ccol = jnp.broadcast_to(chosen, (n_lane, n_lane)).T
        before = jnp.sum(jnp.where(ri < ci, ccol, 0.0), axis=0, keepdims=True)
        hit = (chosen > 0.5) & (before == slot_id.astype(F32))
        idx = jnp.sum(jnp.where(hit, lane.astype(F32), 0.0), axis=1, keepdims=True)
        idx_ref[g] = jnp.broadcast_to(idx, (N_SEL, LANES)).astype(I32)
    oc_ref[...] = o_c


def _select_sample(q3, kvc, sel_m, pos, hpg):
    n_batch, n_heads, _ = q3.shape
    n_cmp = kvc.shape[2]
    n_lane = sel_m.shape[1]
    return pl.pallas_call(
        functools.partial(_ssel_kernel, pos=pos, hpg=hpg, n_cmp=n_cmp, n_lane=n_lane),
        grid=(n_batch,),
        in_specs=[pl.BlockSpec((None, n_heads, HEAD_DIM), lambda b: (b, 0, 0)),
                  pl.BlockSpec((None, 2 * N_KV, n_cmp, HEAD_DIM), lambda b: (b, 0, 0, 0)),
                  pl.BlockSpec(sel_m.shape, lambda b: (0, 0))],
        out_specs=[pl.BlockSpec((None, n_heads, HEAD_DIM), lambda b: (b, 0, 0)),
                   pl.BlockSpec((None, N_KV, N_SEL, LANES), lambda b: (b, 0, 0, 0))],
        out_shape=[jax.ShapeDtypeStruct((n_batch, n_heads, HEAD_DIM), F32),
                   jax.ShapeDtypeStruct((n_batch, N_KV, N_SEL, LANES), I32)],
        compiler_params=_params(("arbitrary",), VMEM_LIMIT),
        name="select_sample",
    )(q3, kvc, sel_m)


def _smix_kernel(pt_ref, idx_ref, cache_ref, wcache_ref, q_ref, oc_ref, gt_ref, slc_ref, win_ref, rope_ref,
                 u_ref, v_ref, lg_ref, lb_ref, w0_ref, b0_ref, na_ref, nb_ref,
                 a_ref, b_ref, vn_ref, buf, wbuf, sem, *, layer, pos, hpg, nb_past):
    b = pl.program_id(0)
    halves = PAGE_SIZE // SLC_BLOCK

    def copies():
        out = []
        for g in range(N_KV):
            for k in range(N_SEL):
                jp = jnp.minimum(idx_ref[b, g, k], nb_past - 1)
                phys = pt_ref[b, jp // halves]
                rows = pl.ds(pl.multiple_of((jp % halves) * SLC_BLOCK, SLC_BLOCK), SLC_BLOCK)
                for t in range(2):
                    out.append(pltpu.make_async_copy(cache_ref.at[layer, phys, rows, g, t, :], buf.at[t, g, k],
                                                     sem.at[0]))
            for t in range(2):
                out.append(pltpu.make_async_copy(wcache_ref.at[layer, b, :, g, t, :], wbuf.at[g, t], sem.at[1]))
        return out

    for n, cp in enumerate(copies()):
        cp.start(priority=n % 2)

    vn = _layernorm(_gelu(v_ref[...]), lg_ref[...], lb_ref[...])
    vn_ref[...] = vn
    a_out = _gelu(u_ref[...]) * (w0_ref[...] * vn + b0_ref[...])
    a_ref[...] = _rms(a_out, na_ref[...]).astype(BF16)

    q = q_ref[...]
    n_heads = q.shape[0]
    head_row = lax.broadcasted_iota(I32, (n_heads, 1), 0)
    qrb = _rope(q, rope_ref[0:1, :], rope_ref[1:2, :], rope_ref[2:3, :]).astype(BF16)
    sig = _sigmoid(gt_ref[...])
    n_keys = N_SEL * SLC_BLOCK
    klane = lax.broadcasted_iota(I32, (1, n_keys), 1)
    wb = wbuf.shape[2]
    wdiff = pos - (pos - wb + lax.broadcasted_iota(I32, (1, wb), 1))
    wmask = (wdiff >= 0) & (wdiff < WINDOW)

    for cp in copies():
        cp.wait()

    o_s = jnp.zeros((n_heads, HEAD_DIM), F32)
    o_w = jnp.zeros((n_heads, HEAD_DIM), F32)
    for g in range(N_KV):
        mine = (head_row // hpg) == g
        k0 = g * 2 * HEAD_DIM
        v0 = k0 + HEAD_DIM
        kmask = jnp.zeros((1, n_keys), F32)
        has_new = jnp.zeros((1, 1), F32)
        for k in range(N_SEL):
            blk = idx_ref[b, g, k]
            in_k = (klane // SLC_BLOCK) == k
            kmask = jnp.where(in_k & (blk < nb_past), 1.0, kmask)
            has_new = jnp.where(blk >= nb_past, 1.0, has_new)
        ks = buf[0, g].reshape(n_keys, HEAD_DIM).astype(BF16)
        vs = buf[1, g].reshape(n_keys, HEAD_DIM).astype(BF16)
        s = jnp.where(kmask > 0.5, _dot_nt(qrb, ks) * SCALE, -jnp.inf)
        k_new = slc_ref[:, k0:k0 + HEAD_DIM].astype(BF16)
        v_new = slc_ref[:, v0:v0 + HEAD_DIM].astype(BF16)
        s_new = jnp.sum(qrb.astype(F32) * k_new.astype(F32), axis=-1, keepdims=True) * SCALE
        s_new = jnp.where(has_new > 0.5, s_new, -jnp.inf)
        m = jnp.maximum(jnp.max(s, axis=-1, keepdims=True), s_new)
        m = jnp.where(m > -jnp.inf, m, 0.0)
        e = jnp.exp(s - m)
        e_new = jnp.exp(s_new - m)
        den = jnp.maximum(jnp.sum(e, axis=-1, keepdims=True) + e_new, 1e-30)
        og = _dot((e / den).astype(BF16), vs) + (e_new / den).astype(BF16).astype(F32) * v_new.astype(F32)
        o_s = jnp.where(mine, og, o_s)
        s = jnp.where(wmask, _dot_nt(qrb, wbuf[g, 0].astype(BF16)) * SCALE, -jnp.inf)
        k_new = win_ref[:, k0:k0 + HEAD_DIM].astype(BF16)
        v_new = win_ref[:, v0:v0 + HEAD_DIM].astype(BF16)
        s_new = jnp.sum(qrb.astype(F32) * k_new.astype(F32), axis=-1, keepdims=True) * SCALE
        m = jnp.maximum(jnp.max(s, axis=-1, keepdims=True), s_new)
        e = jnp.exp(s - m)
        e_new = jnp.exp(s_new - m)
        den = jnp.maximum(jnp.sum(e, axis=-1, keepdims=True) + e_new, 1e-30)
        og = (_dot((e / den).astype(BF16), wbuf[g, 1].astype(BF16))
              + (e_new / den).astype(BF16).astype(F32) * v_new.astype(F32))
        o_w = jnp.where(mine, og, o_w)
    b_out = sig[:, 0:1] * oc_ref[...] + sig[:, 1:2] * o_s + sig[:, 2:3] * o_w
    ms = jnp.sum(jnp.sum(b_out * b_out, axis=-1, keepdims=True), axis=0, keepdims=True) / float(n_heads * HEAD_DIM)
    b_ref[...] = (b_out * lax.rsqrt(ms + EPS) * nb_ref[...]).astype(BF16)


def _mix_sample(page_table, idx, cache_slc, cache_win, q3, o_c, gt3, slc_new, win_new, rope_s, z3, layer,
                ln_g, ln_b, w0, b0, norm_a, norm_b3, pos, hpg, a_width):
    n_batch, n_heads, _ = q3.shape
    wb = cache_win.shape[2]
    nb_past = page_table.shape[1] * (PAGE_SIZE // SLC_BLOCK)
    head_spec = pl.BlockSpec((None, n_heads, HEAD_DIM), lambda b, *_: (b, 0, 0))
    new_spec = pl.BlockSpec((None, 1, KV_COLS), lambda b, *_: (b, 0, 0))
    row_spec = lambda blk: pl.BlockSpec((None, 1, a_width), lambda b, *_: (b, 0, blk))
    vec = lambda: pl.BlockSpec((None, 1, a_width), lambda b, *_: (layer, 0, 0))
    grid_spec = pltpu.PrefetchScalarGridSpec(
        num_scalar_prefetch=2,
        grid=(n_batch,),
        in_specs=[pl.BlockSpec(memory_space=pl.ANY), pl.BlockSpec(memory_space=pl.ANY), head_spec, head_spec,
                  pl.BlockSpec((None, n_heads, 3), lambda b, *_: (b, 0, 0)),
                  new_spec, new_spec,
                  pl.BlockSpec(rope_s.shape, lambda b, *_: (0, 0)),
                  row_spec(0), row_spec(1), vec(), vec(), vec(), vec(), vec(),
                  pl.BlockSpec((None, n_heads, HEAD_DIM), lambda b, *_: (layer, 0, 0))],
        out_specs=[pl.BlockSpec((None, 1, a_width), lambda b, *_: (b, 0, 0)),
                   head_spec,
                   pl.BlockSpec((None, 1, a_width), lambda b, *_: (b, 0, 0))],
        scratch_shapes=[pltpu.VMEM((2, N_KV, N_SEL, SLC_BLOCK, HEAD_DIM), F32),
                        pltpu.VMEM((N_KV, 2, wb, HEAD_DIM), F32),
                        pltpu.SemaphoreType.DMA((2,))])
    return pl.pallas_call(
        functools.partial(_smix_kernel, layer=layer, pos=pos, hpg=hpg, nb_past=nb_past),
        grid_spec=grid_spec,
        out_shape=[jax.ShapeDtypeStruct((n_batch, 1, a_width), BF16),
                   jax.ShapeDtypeStruct((n_batch, n_heads, HEAD_DIM), BF16),
                   jax.ShapeDtypeStruct((n_batch, 1, a_width), F32)],
        compiler_params=_params(("arbitrary",), VMEM_LIMIT),
        name="mix_sample",
    )(page_table, idx, cache_slc, cache_win, q3, o_c, gt3, slc_new.reshape(n_batch, 1, KV_COLS),
      win_new.reshape(n_batch, 1, KV_COLS), rope_s, z3, z3,
      ln_g, ln_b, w0, b0, norm_a, norm_b3)


def _rope_tables(pos):
    inv = ROPE_THETA ** (-jnp.arange(ROPE_HALF, dtype=F32) / ROPE_HALF)
    ang = pos.astype(F32)[:, None] * inv[None, :]
    cos, sin = jnp.cos(ang), jnp.sin(ang)
    n = pos.shape[0]
    zeros = lambda w: jnp.zeros((n, w), F32)
    c = jnp.concatenate([cos, cos, jnp.ones((n, HEAD_DIM - ROPE_DIM), F32)], axis=1)
    sa = jnp.concatenate([-sin, zeros(HEAD_DIM - ROPE_HALF)], axis=1)
    sb = jnp.concatenate([zeros(ROPE_HALF), sin, zeros(HEAD_DIM - ROPE_DIM)], axis=1)
    return c, sa, sb


def _block_sum_matrix(n_cmp, n_lane):
    i = jnp.arange(n_cmp)[:, None]
    j = jnp.arange(n_lane)[None, :]
    return ((i // CMP_PER_SLC == j) & (i % CMP_PER_SLC < CMP_INSIDE)).astype(BF16)


def kernel(x_prompt, x_sample, cache_cmp, cache_slc, cache_win, page_table, c_prompt, c_sample, w_ada, b_ada, norm_pre_mix, norm_post_mix, norm_pre_ffn, norm_post_ffn, w_in, ln_v_g, ln_v_b, sgu_w, sgu_b, cmp_pe_k, cmp_pe_v, cmp_w1_k, cmp_w2_k, cmp_w1_v, cmp_w2_v, out_norm_a, out_norm_b, w_out, w_up, w_down):
    batch, seq, d = x_prompt.shape
    n_dec, dec_seq, _ = x_sample.shape
    depth = w_ada.shape[0]
    a_width = d // 2
    b_width = d - a_width
    groups = a_width // HEAD_DIM
    n_heads = b_width // HEAD_DIM
    hpg = n_heads // N_KV
    n_main = 2 * a_width + b_width + 3 * KV_COLS
    n_gate = 3 * n_heads
    n_pages = page_table.shape[1]
    past = n_pages * PAGE_SIZE
    assert dec_seq == 1 and n_dec % 8 == 0 and a_width == b_width and n_gate <= LANES
    assert (2 * a_width + b_width) % (3 * KV_COLS) == 0 and seq % 1024 == 0
    assert past % SLC_BLOCK == 0 and past // SLC_BLOCK + 1 > N_SEL and cache_win.shape[2] <= past
    q_block = 2 * a_width // b_width
    kv_block = (2 * a_width + b_width) // (3 * KV_COLS)
    cmp_block = (2 * a_width + b_width) // KV_COLS
    m_p = batch * seq

    vec3 = lambda a: a.reshape(depth, 1, a.shape[-1])
    w_main = w_in.astype(BF16)
    big = (w_out, w_up, w_down)
    wts = [w[0:1].astype(BF16) for w in big]
    w_gate = jnp.pad(w_in[:, :, n_main:], ((0, 0), (0, 0), (0, LANES - n_gate))).astype(BF16)
    cmp_w1 = jnp.stack([cmp_w1_k, cmp_w1_v], axis=1).astype(BF16)
    cmp_w2 = jnp.stack([cmp_w2_k, cmp_w2_v], axis=1).astype(BF16)
    cmp_pe = jnp.broadcast_to(jnp.stack([cmp_pe_k, cmp_pe_v], axis=1).reshape(depth, 2, 1, CMP_BLOCK * HEAD_DIM),
                              (depth, 2, 8, CMP_BLOCK * HEAD_DIM))
    cmp_pe = _cmp_bias(cmp_pe, cmp_w1)
    pre_mix, post_mix, pre_ffn, post_ffn = map(vec3, (norm_pre_mix, norm_post_mix, norm_pre_ffn, norm_post_ffn))
    ln_g, ln_b, norm_a, norm_b = map(vec3, (ln_v_g, ln_v_b, out_norm_a, out_norm_b))
    norm_b3 = out_norm_b.reshape(depth, n_heads, HEAD_DIM)
    sgu_bt = jnp.swapaxes(sgu_b, 1, 2)
    sgu_w0 = jnp.repeat(sgu_w[:, :, 0, 0], HEAD_DIM, axis=1).reshape(depth, 1, a_width)
    sgu_b0 = jnp.repeat(sgu_b[:, :, 0], HEAD_DIM, axis=1).reshape(depth, 1, a_width)

    mod_rows = -(-(n_dec + batch) // 8) * 8
    c_all = jnp.concatenate([c_sample, c_prompt, jnp.zeros((mod_rows - n_dec - batch, d), F32)], axis=0)
    mod = _ada(c_all, w_ada, b_ada)

    tables_p = _rope_tables(jnp.arange(seq))
    tables_s = _rope_tables(jnp.full((n_dec,), past))
    rope_s = jnp.concatenate([t[0:1] for t in tables_s] + [jnp.zeros((5, LANES), F32)], axis=0)
    n_cmp_p = seq // CMP_STRIDE
    sel_p = _block_sum_matrix(n_cmp_p, LANES)
    key_chunk = 512
    exp_p = (jnp.arange(LANES)[None, :, None]
             == (jnp.arange(seq) // SLC_BLOCK).reshape(seq // key_chunk, 1, key_chunk)).astype(BF16)
    n_cmp_s = past // CMP_STRIDE
    nb_lane = -(-(past // SLC_BLOCK + 1) // LANES) * LANES
    sel_s = _block_sum_matrix(n_cmp_s, nb_lane)

    y_p = x_prompt.reshape(m_p, d)
    y_s = x_sample.reshape(n_dec, d)
    caches_p = [jnp.zeros((depth, batch, rows, N_KV, 2, HEAD_DIM), F32) for rows in (seq, seq, min(WINDOW, seq))]
    outs = [[] for _ in range(4)]
    win_keep = min(WINDOW, seq)
    for l in range(depth):
        w_out_b, w_up_b, w_down_b = wts
        rows = _Rows(mod, l, d, m_p, 1024, seq, n_dec)
        z, gt = _in_proj(y_p, rows, l, pre_mix, w_main, w_gate, n_main)
        a_n = _sgu(z, l, a_width, ln_g, ln_b, sgu_w, sgu_bt, norm_a)
        *caches_p, slc_b, win_b = _kv_rope_prompt(z, kv_block, tables_p, 256, l, depth, batch, seq, win_keep,
                                                  caches_p)
        kvc = _compress_prompt(z, batch, seq, cmp_block, cmp_pe[l], cmp_w1[l], cmp_w2[l])
        b_n = _attn_prompt(z, gt, kvc, slc_b, win_b, tables_p, l, norm_b, sel_p, exp_p, batch, seq, b_width, q_block)
        y_p = _out_proj(a_n, b_n, y_p, _Rows(mod, l, d, m_p, 512, seq, n_dec), l, w_out_b, post_mix)
        y_p, wts_next = _ffn(y_p, rows, l, pre_ffn, post_ffn, w_up_b, w_down_b,
                             cast_next=big if l + 1 < depth else (), next_layer=l + 1)
        rows = _Rows(mod, l, d, n_dec, n_dec, None, 0)
        z, gt = _in_proj(y_s, rows, l, pre_mix, w_main, w_gate, n_main)
        kv_cmp, kv_slc, kv_win, _, _ = _kv_rope(z, kv_block, tables_s, n_dec)
        kvc = _compress_sample(page_table, cache_cmp, l, kv_cmp, cmp_pe[l], cmp_w1[l], cmp_w2[l])
        q3 = z[:, 2 * a_width:2 * a_width + b_width].reshape(n_dec, n_heads, HEAD_DIM)
        o_c, idx = _select_sample(q3, kvc, sel_s, past, hpg)
        a_n, b_n, v_n = _mix_sample(page_table, idx[:, :, :, 0], cache_slc, cache_win, q3, o_c,
                                    gt[:, :n_gate].reshape(n_dec, n_heads, 3), kv_slc, kv_win,
                                    rope_s, z.reshape(n_dec, 1, n_main), l, ln_g, ln_b, sgu_w0, sgu_b0,
                                    norm_a, norm_b3, past, hpg, a_width)
        y_s = _out_proj(a_n.reshape(n_dec, a_width), b_n.reshape(n_dec, b_width), y_s, rows, l, w_out_b, post_mix)
        y_s, _ = _ffn(y_s, rows, l, pre_ffn, post_ffn, w_up_b, w_down_b)
        wts = [w[None] for w in wts_next]
        outs[0].append(kv_cmp.reshape(n_dec, 1, N_KV, 2, HEAD_DIM))
        outs[1].append(kv_slc.reshape(n_dec, 1, N_KV, 2, HEAD_DIM))
        outs[2].append(kv_win.reshape(n_dec, 1, N_KV, 2, HEAD_DIM))
        outs[3].append(v_n)
    return (y_p.reshape(batch, seq, d), y_s.reshape(n_dec, 1, d), *caches_p, *[jnp.stack(o) for o in outs])
```

```python
import functools

import jax
import jax.numpy as jnp
from jax import lax
from jax.experimental import pallas as pl
from jax.experimental.pallas import tpu as pltpu

F32 = jnp.float32
BF16 = jnp.bfloat16
I32 = jnp.int32

LANES = 128
ROW_STRIP = 16
HEAD_DIM = 128
N_KV = 2
PAGE_SIZE = 128
CHUNK = 128
CMP_BLOCK = 32
CMP_STRIDE = 16
SLC_BLOCK = 64
N_SEL = 16
WINDOW = 512
ROPE_DIM = HEAD_DIM // 4
ROPE_HALF = ROPE_DIM // 2
ROPE_THETA = 500000.0
EPS = 1e-6
FORCE = 1e4
SCALE = HEAD_DIM ** -0.5
EXP2_SCALE = SCALE * 1.4426950408889634
KV_COLS = N_KV * 2 * HEAD_DIM
CMP_STRIPS = CMP_BLOCK // CMP_STRIDE
STRIP_COLS = CMP_STRIDE * HEAD_DIM
CMP_PER_SLC = SLC_BLOCK // CMP_STRIDE
CMP_INSIDE = (SLC_BLOCK - CMP_BLOCK) // CMP_STRIDE + 1
VMEM_LIMIT = 56 * 2 ** 20


def _dot(a, b):
    return jnp.dot(a, b, preferred_element_type=F32)


def _dot_nt(a, b):
    return lax.dot_general(a, b, (((1,), (1,)), ((), ())), preferred_element_type=F32)


def _gelu(x):
    return 0.5 * x * (1.0 + jnp.tanh(0.7978845608028654 * (x + 0.044715 * (x * x * x))))


def _sigmoid(x):
    return 1.0 / (1.0 + jnp.exp(-x))


def _rms(x, g):
    return x * lax.rsqrt(jnp.mean(x * x, axis=-1, keepdims=True) + EPS) * g


def _softmax_neg_inf(s):
    m = jnp.max(s, axis=-1, keepdims=True)
    m = jnp.where(m > -jnp.inf, m, 0.0)
    e = jnp.exp(s - m)
    return e / jnp.maximum(jnp.sum(e, axis=-1, keepdims=True), 1e-30)


def _msoftmax(s, mask):
    return _softmax_neg_inf(jnp.where(mask, s, -jnp.inf))


def _attend(s, v):
    m = jnp.max(s, axis=-1, keepdims=True)
    m = jnp.where(m > -jnp.inf, m, 0.0)
    e = jnp.exp2(s - m)
    return _dot(e.astype(BF16), v) / jnp.maximum(jnp.sum(e, axis=-1, keepdims=True), 1e-30)


def _rope(x, c, sa, sb):
    return x * c + pltpu.roll(x, LANES - ROPE_HALF, axis=1) * sa + pltpu.roll(x, ROPE_HALF, axis=1) * sb


def _split_dot(x, m):
    hi = x.astype(BF16)
    r1 = x - hi.astype(F32)
    mid = r1.astype(BF16)
    lo = (r1 - mid.astype(F32)).astype(BF16)
    return _dot(hi, m) + _dot(mid, m) + _dot(lo, m)


def _row_strips(ref):
    n = ref.shape[0]
    step = ROW_STRIP if n % ROW_STRIP == 0 else n
    return [slice(r, r + step) for r in range(0, n, step)]


def _mod_rows(ref, rs):
    return ref[...] if ref.shape[0] == 1 else ref[rs, :]


def _modulated_norm(x_ref, g_ref, sc_ref, sh_ref, h_ref):
    for rs in _row_strips(x_ref):
        h = _rms(x_ref[rs, :], g_ref[...]) * (1.0 + _mod_rows(sc_ref, rs)) + _mod_rows(sh_ref, rs)
        h_ref[rs, :] = h.astype(h_ref.dtype)


def _gated_residual(x_ref, gate_ref, g_ref, o_ref):
    for rs in _row_strips(x_ref):
        o_ref[rs, :] = x_ref[rs, :] + _mod_rows(gate_ref, rs) * _rms(o_ref[rs, :], g_ref[...])


def _params(sem, vmem=None):
    return pltpu.CompilerParams(dimension_semantics=sem, vmem_limit_bytes=vmem)


def _ada_kernel(c_ref, w_ref, b_ref, o_ref):
    c = c_ref[...]
    o_ref[...] = _dot((c * _sigmoid(c)).astype(BF16), w_ref[...].astype(BF16)) + b_ref[...]


def _ada(c_all, w_ada, b_ada):
    depth, d, n = w_ada.shape
    rows = c_all.shape[0]
    tn = 1024
    return pl.pallas_call(
        _ada_kernel,
        grid=(depth, n // tn),
        in_specs=[pl.BlockSpec((rows, d), lambda l, j: (0, 0)),
                  pl.BlockSpec((None, d, tn), lambda l, j: (l, 0, j)),
                  pl.BlockSpec((None, 1, tn), lambda l, j: (l, 0, j))],
        out_specs=pl.BlockSpec((None, rows, tn), lambda l, j: (l, 0, j)),
        out_shape=jax.ShapeDtypeStruct((depth, rows, n), F32),
        compiler_params=_params(("arbitrary", "arbitrary"), VMEM_LIMIT),
        name="ada",
    )(c_all, w_ada, b_ada.reshape(depth, 1, n))


class _Rows:
    def __init__(self, mod, layer, d, m, tm, rows_per_batch, mod_row0):
        self.m, self.tm, self.d = m, tm, d
        depth, r, n = mod.shape
        if rows_per_batch is None:
            self.mod = mod
            self._spec = lambda k: pl.BlockSpec((None, tm, d), lambda i, *_: (layer, mod_row0 // tm, k))
        else:
            tiles = rows_per_batch // tm
            self.mod = mod.reshape(depth, r, 1, n)
            self._spec = lambda k: pl.BlockSpec((None, None, 1, d),
                                                lambda i, *_: (layer, mod_row0 + i // tiles, 0, k))

    def mod_spec(self, k):
        return self._spec(k)


def _vec_spec(layer, n):
    return pl.BlockSpec((None, 1, n), lambda *_: (layer, 0, 0))


def _in_kernel(x_ref, g_ref, sc_ref, sh_ref, w_ref, wg_ref, z_ref, gt_ref, h_ref):
    @pl.when(pl.program_id(1) == 0)
    def _():
        _modulated_norm(x_ref, g_ref, sc_ref, sh_ref, h_ref)
        gt_ref[...] = _dot(h_ref[...].astype(BF16), wg_ref[...])

    z_ref[...] = _dot(h_ref[...].astype(BF16), w_ref[...])


def _in_proj(x, rows, layer, norm_w, w_main, w_gate, n):
    m, tm, d = rows.m, rows.tm, rows.d
    tn = 1152 if n % 1152 == 0 else 512
    h_dtype = BF16 if tm % 16 == 0 else F32
    return pl.pallas_call(
        _in_kernel,
        grid=(m // tm, n // tn),
        in_specs=[pl.BlockSpec((tm, d), lambda i, j: (i, 0)),
                  _vec_spec(layer, d), rows.mod_spec(1), rows.mod_spec(0),
                  pl.BlockSpec((None, d, tn), lambda i, j: (layer, 0, j)),
                  pl.BlockSpec((None, d, LANES), lambda i, j: (layer, 0, 0))],
        out_specs=[pl.BlockSpec((tm, tn), lambda i, j: (i, j)),
                   pl.BlockSpec((tm, LANES), lambda i, j: (i, 0))],
        out_shape=[jax.ShapeDtypeStruct((m, n), F32), jax.ShapeDtypeStruct((m, LANES), F32)],
        scratch_shapes=[pltpu.VMEM((tm, d), h_dtype)],
        compiler_params=_params(("arbitrary", "arbitrary"), VMEM_LIMIT),
        name="in_proj",
    )(x, norm_w, rows.mod, rows.mod, w_main, w_gate)


def _layernorm(v, g, b):
    mu = jnp.mean(v, axis=-1, keepdims=True)
    vc = v - mu
    return vc * lax.rsqrt(jnp.mean(vc * vc, axis=-1, keepdims=True) + EPS) * g + b


def _sgu_kernel(u_ref, v_ref, lg_ref, lb_ref, w_ref, bt_ref, na_ref, o_ref, *, groups, chunks):
    row = lax.broadcasted_iota(I32, (CHUNK, CHUNK), 0)
    col = lax.broadcasted_iota(I32, (CHUNK, CHUNK), 1)
    ws = [jnp.where(row >= col, w_ref[g], 0.0).astype(BF16) for g in range(groups)]
    for c in range(chunks):
        rs = slice(c * CHUNK, (c + 1) * CHUNK)
        vn = _layernorm(_gelu(v_ref[rs, :]), lg_ref[...], lb_ref[...]).astype(BF16)
        u = _gelu(u_ref[rs, :])
        outs = []
        for g in range(groups):
            cs = slice(g * HEAD_DIM, (g + 1) * HEAD_DIM)
            outs.append(u[:, cs] * (_dot(ws[g], vn[:, cs]) + bt_ref[:, g:g + 1]))
        o_ref[rs, :] = _rms(jnp.concatenate(outs, axis=1), na_ref[...]).astype(BF16)


def _sgu(z, layer, a_width, ln_g, ln_b, sgu_w, sgu_bt, norm_a):
    m = z.shape[0]
    groups = a_width // HEAD_DIM
    chunks = 4
    tr = chunks * CHUNK
    return pl.pallas_call(
        functools.partial(_sgu_kernel, groups=groups, chunks=chunks),
        grid=(m // tr,),
        in_specs=[pl.BlockSpec((tr, a_width), lambda i: (i, 0)),
                  pl.BlockSpec((tr, a_width), lambda i: (i, 1)),
                  _vec_spec(layer, a_width), _vec_spec(layer, a_width),
                  pl.BlockSpec((None, groups, CHUNK, CHUNK), lambda i: (layer, 0, 0, 0)),
                  pl.BlockSpec((None, CHUNK, groups), lambda i: (layer, 0, 0)),
                  _vec_spec(layer, a_width)],
        out_specs=pl.BlockSpec((tr, a_width), lambda i: (i, 0)),
        out_shape=jax.ShapeDtypeStruct((m, a_width), BF16),
        compiler_params=_params(("arbitrary",), VMEM_LIMIT),
        name="sgu",
    )(z, z, ln_g, ln_b, sgu_w, sgu_bt, norm_a)


def _kv_kernel(kv_ref, rc_ref, rsa_ref, rsb_ref, cmp_ref, slc_ref, win_ref, slcb_ref, winb_ref):
    cmp_ref[...] = kv_ref[:, 0:KV_COLS]
    rc, rsa, rsb = rc_ref[...], rsa_ref[...], rsb_ref[...]
    for t, (o_ref, ob_ref) in enumerate(((slc_ref, slcb_ref), (win_ref, winb_ref))):
        for g in range(N_KV):
            k0 = (t + 1) * KV_COLS + g * 2 * HEAD_DIM
            k = _rope(kv_ref[:, k0:k0 + HEAD_DIM], rc, rsa, rsb)
            v = kv_ref[:, k0 + HEAD_DIM:k0 + 2 * HEAD_DIM]
            o0 = g * 2 * HEAD_DIM
            o_ref[:, o0:o0 + HEAD_DIM] = k
            o_ref[:, o0 + HEAD_DIM:o0 + 2 * HEAD_DIM] = v
            ob_ref[:, o0:o0 + HEAD_DIM] = k.astype(BF16)
            ob_ref[:, o0 + HEAD_DIM:o0 + 2 * HEAD_DIM] = v.astype(BF16)


def _kv_cache_kernel(kv_ref, rc_ref, rsa_ref, rsb_ref, *refs):
    cmp_ref, slc_ref, win_ref, slcb_ref, winb_ref = refs[-5:]
    rc, rsa, rsb = rc_ref[...], rsa_ref[...], rsb_ref[...]
    for g in range(N_KV):
        for t in range(2):
            c0 = (g * 2 + t) * HEAD_DIM
            cmp_ref[:, g, t, :] = kv_ref[:, c0:c0 + HEAD_DIM]
    for j, (o_ref, ob_ref) in enumerate(((slc_ref, slcb_ref), (win_ref, winb_ref))):
        for g in range(N_KV):
            k0 = (j + 1) * KV_COLS + g * 2 * HEAD_DIM
            k = _rope(kv_ref[:, k0:k0 + HEAD_DIM], rc, rsa, rsb)
            v = kv_ref[:, k0 + HEAD_DIM:k0 + 2 * HEAD_DIM]
            o0 = g * 2 * HEAD_DIM
            o_ref[:, g, 0, :] = k
            o_ref[:, g, 1, :] = v
            ob_ref[:, o0:o0 + HEAD_DIM] = k.astype(BF16)
            ob_ref[:, o0 + HEAD_DIM:o0 + 2 * HEAD_DIM] = v.astype(BF16)


def _kv_rope_prompt(z, kv_block, tables, tr, layer, depth, batch, seq, win_keep, prev):
    m = z.shape[0]
    nt = seq // tr
    first = (seq - win_keep) // tr
    tspec = pl.BlockSpec((tr, LANES), lambda i: (i % nt, 0))
    bspec = pl.BlockSpec((tr, KV_COLS), lambda i: (i, 0))
    cache_block = (None, None, tr, N_KV, 2, HEAD_DIM)
    cspec = pl.BlockSpec(cache_block, lambda i: (layer, i // nt, i % nt, 0, 0, 0))
    wspec = pl.BlockSpec(cache_block, lambda i: (layer, i // nt, jnp.maximum(i % nt - first, 0), 0, 0, 0))
    cache_shape = lambda rows: jax.ShapeDtypeStruct((depth, batch, rows, N_KV, 2, HEAD_DIM), F32)
    return pl.pallas_call(
        _kv_cache_kernel,
        grid=(m // tr,),
        in_specs=[pl.BlockSpec((tr, 3 * KV_COLS), lambda i: (i, kv_block)), tspec, tspec, tspec]
        + [pl.BlockSpec(memory_space=pl.ANY)] * len(prev),
        out_specs=[cspec, cspec, wspec, bspec, bspec],
        out_shape=[cache_shape(seq), cache_shape(seq), cache_shape(win_keep)]
        + [jax.ShapeDtypeStruct((m, KV_COLS), BF16)] * 2,
        input_output_aliases={4 + i: i for i in range(len(prev))},
        compiler_params=_params(("arbitrary",), VMEM_LIMIT),
        name="kv_rope_prompt",
    )(z, *tables, *prev)


def _kv_rope(z, kv_block, tables, tr):
    m = z.shape[0]
    t_rows = tables[0].shape[0]
    nt = t_rows // tr
    tspec = pl.BlockSpec((tr, LANES), lambda i: (i % nt, 0))
    ospec = pl.BlockSpec((tr, KV_COLS), lambda i: (i, 0))
    return pl.pallas_call(
        _kv_kernel,
        grid=(m // tr,),
        in_specs=[pl.BlockSpec((tr, 3 * KV_COLS), lambda i: (i, kv_block)), tspec, tspec, tspec],
        out_specs=[ospec] * 5,
        out_shape=[jax.ShapeDtypeStruct((m, KV_COLS), F32)] * 3 + [jax.ShapeDtypeStruct((m, KV_COLS), BF16)] * 2,
        compiler_params=_params(("arbitrary",), VMEM_LIMIT),
        name="kv_rope",
    )(z, *tables)


def _cmp_bias_kernel(pe_ref, w1_ref, o_ref):
    o_ref[...] = _dot(pe_ref[...].astype(BF16), w1_ref[...])


def _cmp_bias(pe, w1):
    depth, _, _, k = pe.shape
    hidden = w1.shape[-1]
    return pl.pallas_call(
        _cmp_bias_kernel,
        grid=(depth, 2),
        in_specs=[pl.BlockSpec((None, None, 8, k), lambda l, s: (l, s, 0, 0)),
                  pl.BlockSpec((None, None, k, hidden), lambda l, s: (l, s, 0, 0))],
        out_specs=pl.BlockSpec((None, None, 8, hidden), lambda l, s: (l, s, 0, 0)),
        out_shape=jax.ShapeDtypeStruct((depth, 2, 8, hidden), F32),
        compiler_params=_params(("arbitrary", "arbitrary"), VMEM_LIMIT),
        name="cmp_bias",
    )(pe, w1)


def _compress_strips(xf, w1_ref, sel):
    xb = xf.astype(BF16)
    return _dot(xb, w1_ref[sel, 0:STRIP_COLS, :]), _dot(xb, w1_ref[sel, STRIP_COLS:2 * STRIP_COLS, :])


def _cmp_kernel(x_ref, pe_ref, w1_ref, w2_ref, o_ref, *, n):
    sel = pl.program_id(1) % 2
    xf = jnp.concatenate([x_ref[pl.ds(r, n, stride=CMP_STRIDE), :] for r in range(CMP_STRIDE)], axis=1)
    p, q = _compress_strips(xf, w1_ref, sel)
    hid = _gelu(p + pltpu.roll(q, n - 1, axis=0) + pe_ref[sel, 0:1, :])
    o_ref[...] = _dot(hid.astype(BF16), w2_ref[sel]).astype(BF16)


def _compress_prompt(z, batch, seq, cmp_block, pe2, w1, w2):
    n = seq // CMP_STRIDE
    hidden = w1.shape[-1]
    heads = 2 * N_KV
    return pl.pallas_call(
        functools.partial(_cmp_kernel, n=n),
        grid=(batch, heads),
        in_specs=[pl.BlockSpec((seq, HEAD_DIM), lambda b, h: (b, cmp_block * heads + h)),
                  pl.BlockSpec((2, 8, hidden), lambda b, h: (0, 0, 0)),
                  pl.BlockSpec((2, 2 * STRIP_COLS, hidden), lambda b, h: (0, 0, 0)),
                  pl.BlockSpec((2, hidden, HEAD_DIM), lambda b, h: (0, 0, 0))],
        out_specs=pl.BlockSpec((None, None, n, HEAD_DIM), lambda b, h: (b, h, 0, 0)),
        out_shape=jax.ShapeDtypeStruct((batch, heads, n, HEAD_DIM), BF16),
        compiler_params=_params(("arbitrary", "arbitrary"), VMEM_LIMIT),
        name="compress_prompt",
    )(z, pe2, w1, w2)


def _attn_kernel(q_ref, gt_ref, cmp_ref, slc_ref, win_ref, rc_ref, rsa_ref, rsb_ref, nb_ref, sel_ref, exp_ref,
                 o_ref, acc_ref, *, tq, seq, hpg, n_cmp, kc):
    q0 = pl.program_id(1) * tq
    n_chunks = (q0 + tq + kc - 1) // kc
    kiota = lax.broadcasted_iota(I32, (1, kc), 1)
    pos = q0 + lax.broadcasted_iota(I32, (tq, 1), 0)
    posh = jnp.concatenate([pos] * hpg, axis=0)
    rc = jnp.concatenate([rc_ref[...]] * hpg, axis=0)
    rsa = jnp.concatenate([rsa_ref[...]] * hpg, axis=0)
    rsb = jnp.concatenate([rsb_ref[...]] * hpg, axis=0)
    sig = _sigmoid(gt_ref[...])
    clane = lax.broadcasted_iota(I32, (1, n_cmp), 1)
    n_blk = seq // SLC_BLOCK
    n_sel = min(N_SEL, n_blk)
    nb8 = -(-n_blk // 8) * 8
    brow = lax.broadcasted_iota(I32, (nb8, 1), 0)
    cur_t = (q0 + lax.broadcasted_iota(I32, (1, tq), 1)) // SLC_BLOCK
    forced_t = (brow == 0) | (brow == cur_t) | (brow == cur_t - 1)
    brow_q = lax.broadcasted_iota(I32, (nb8, tq), 0)
    wk = WINDOW + tq
    wstart = pl.multiple_of(jnp.maximum(q0 - WINDOW, 0), LANES)
    wpos = wstart + lax.broadcasted_iota(I32, (1, wk), 1)
    wdiff = posh - wpos
    wmask = (wdiff >= 0) & (wdiff < WINDOW)
    cmask = (clane * CMP_STRIDE + (CMP_BLOCK - 1)) <= posh
    for g in range(N_KV):
        heads = [g * hpg + h for h in range(hpg)]
        qc = jnp.concatenate([q_ref[:, hd * HEAD_DIM:(hd + 1) * HEAD_DIM] for hd in heads], axis=0)
        qrb = (_rope(qc, rc, rsa, rsb) * EXP2_SCALE).astype(BF16)
        k0 = g * 2 * HEAD_DIM
        v0 = k0 + HEAD_DIM
        p = _msoftmax(_dot_nt(qc.astype(BF16), cmp_ref[2 * g]) * SCALE, cmask)
        o_c = _dot(p.astype(BF16), cmp_ref[2 * g + 1])
        imp = p[0:tq]
        for h in range(1, hpg):
            imp = imp + p[h * tq:(h + 1) * tq]
        impb = _split_dot(imp, sel_ref[...])
        score = jnp.where(brow <= cur_t, jnp.where(forced_t, FORCE, impb.T[0:nb8]), -FORCE)
        rank = jnp.zeros((nb8, tq), F32)
        for i in range(n_blk):
            ci = score[i:i + 1, :]
            tie = jnp.where(brow_q > i, 1.0, 0.0)
            rank = rank + jnp.where(ci > score, 1.0, jnp.where(ci == score, tie, 0.0))
        chosen_t = jnp.where((rank < float(n_sel)) & (brow <= cur_t), 1.0, 0.0)
        chosen_b = jnp.concatenate([chosen_t, jnp.zeros((LANES - nb8, tq), F32)], axis=0).T.astype(BF16)
        rows = hpg * tq

        def chunk(c, carry, diagonal):
            m, l, acc = carry
            r0 = pl.multiple_of(c * kc, kc)
            open_key = _dot(chosen_b, exp_ref[c]) > 0.5
            if diagonal:
                open_key = open_key & ((r0 + kiota) <= pos)
            bias = jnp.where(open_key, 0.0, -jnp.inf)
            s = _dot_nt(qrb, slc_ref[pl.ds(r0, kc), k0:k0 + HEAD_DIM]).reshape(hpg, tq, kc) + bias[None]
            m_new = jnp.maximum(m, jnp.max(s, axis=-1, keepdims=True))
            m_ref = jnp.where(m_new > -jnp.inf, m_new, 0.0)
            alpha = jnp.exp2(m - m_ref)
            e = jnp.exp2(s - m_ref)
            l = alpha * l + jnp.sum(e, axis=-1, keepdims=True)
            pv = _dot(e.reshape(rows, kc).astype(BF16), slc_ref[pl.ds(r0, kc), v0:v0 + HEAD_DIM])
            return m_new, l, alpha.reshape(rows, 1) * acc + pv

        init = (jnp.full((hpg, tq, 1), -jnp.inf, F32), jnp.zeros((hpg, tq, 1), F32),
                jnp.zeros((rows, HEAD_DIM), F32))
        carry = lax.fori_loop(0, n_chunks - 1, functools.partial(chunk, diagonal=False), init)
        _, l, acc = chunk(n_chunks - 1, carry, diagonal=True)
        o_s = acc / jnp.maximum(l.reshape(rows, 1), 1e-30)
        s = jnp.where(wmask, _dot_nt(qrb, win_ref[pl.ds(wstart, wk), k0:k0 + HEAD_DIM]), -jnp.inf)
        o_w = _attend(s, win_ref[pl.ds(wstart, wk), v0:v0 + HEAD_DIM])
        for h, hd in enumerate(heads):
            rs = slice(h * tq, (h + 1) * tq)
            acc_ref[:, hd * HEAD_DIM:(hd + 1) * HEAD_DIM] = (
                sig[:, 3 * hd:3 * hd + 1] * o_c[rs] + sig[:, 3 * hd + 1:3 * hd + 2] * o_s[rs]
                + sig[:, 3 * hd + 2:3 * hd + 3] * o_w[rs])
    o_ref[...] = _rms(acc_ref[...], nb_ref[...]).astype(BF16)


def _attn_prompt(z, gt, kvc, slc_b, win_b, tables, layer, norm_b, sel_m, exp_m, batch, seq, b_width, q_block):
    tq = 256
    nq = seq // tq
    hpg = b_width // HEAD_DIM // N_KV
    n_cmp = kvc.shape[2]
    kc = exp_m.shape[2]
    tspec = pl.BlockSpec((tq, LANES), lambda b, i: (i, 0))
    kvspec = pl.BlockSpec((seq, KV_COLS), lambda b, i: (b, 0))
    return pl.pallas_call(
        functools.partial(_attn_kernel, tq=tq, seq=seq, hpg=hpg, n_cmp=n_cmp, kc=kc),
        grid=(batch, nq),
        in_specs=[pl.BlockSpec((tq, b_width), lambda b, i: (b * nq + i, q_block)),
                  pl.BlockSpec((tq, LANES), lambda b, i: (b * nq + i, 0)),
                  pl.BlockSpec((None, 2 * N_KV, n_cmp, HEAD_DIM), lambda b, i: (b, 0, 0, 0)),
                  kvspec, kvspec, tspec, tspec, tspec,
                  pl.BlockSpec((None, 1, b_width), lambda b, i: (layer, 0, 0)),
                  pl.BlockSpec(sel_m.shape, lambda b, i: (0, 0)),
                  pl.BlockSpec(exp_m.shape, lambda b, i: (0, 0, 0))],
        out_specs=pl.BlockSpec((tq, b_width), lambda b, i: (b * nq + i, 0)),
        out_shape=jax.ShapeDtypeStruct((batch * seq, b_width), BF16),
        scratch_shapes=[pltpu.VMEM((tq, b_width), F32)],
        compiler_params=_params(("arbitrary", "arbitrary"), VMEM_LIMIT),
        name="attn_prompt",
    )(z, gt, kvc, slc_b, win_b, *tables, norm_b, sel_m, exp_m)


def _out_kernel(a_ref, b_ref, w_ref, x_ref, gate_ref, g_ref, o_ref, *, a_width):
    o_ref[...] = _dot(a_ref[...], w_ref[0:a_width, :]) + _dot(b_ref[...], w_ref[a_width:, :])
    _gated_residual(x_ref, gate_ref, g_ref, o_ref)


def _out_proj(a_n, b_n, x, rows, layer, w_out, norm_w):
    m, tm, d = rows.m, rows.tm, rows.d
    a_width = a_n.shape[1]
    b_width = b_n.shape[1]
    return pl.pallas_call(
        functools.partial(_out_kernel, a_width=a_width),
        grid=(m // tm,),
        in_specs=[pl.BlockSpec((tm, a_width), lambda i: (i, 0)),
                  pl.BlockSpec((tm, b_width), lambda i: (i, 0)),
                  pl.BlockSpec((None, a_width + b_width, d), lambda i: (0, 0, 0)),
                  pl.BlockSpec((tm, d), lambda i: (i, 0)),
                  rows.mod_spec(2), _vec_spec(layer, d)],
        out_specs=pl.BlockSpec((tm, d), lambda i: (i, 0)),
        out_shape=jax.ShapeDtypeStruct((m, d), F32),
        compiler_params=_params(("arbitrary",), VMEM_LIMIT),
        name="out_proj",
    )(a_n, b_n, w_out, x, rows.mod, norm_w)


def _ffn_kernel(x_ref, g1_ref, sc_ref, sh_ref, wu_ref, wd_ref, gate_ref, g2_ref, *refs, n_cast):
    cast_in, o_ref, cast_out, h_ref = refs[:n_cast], refs[n_cast], refs[n_cast + 1:2 * n_cast + 1], refs[-1]
    k = pl.program_id(1)
    for src, dst in zip(cast_in, cast_out):
        dst[...] = src[...].astype(BF16)

    @pl.when(k == 0)
    def _():
        _modulated_norm(x_ref, g1_ref, sc_ref, sh_ref, h_ref)
        o_ref[...] = jnp.zeros_like(o_ref)

    up = jnp.maximum(_dot(h_ref[...].astype(BF16), wu_ref[...]), 0.0)
    o_ref[...] += _dot((up * up).astype(BF16), wd_ref[...])

    @pl.when(k == pl.num_programs(1) - 1)
    def _():
        _gated_residual(x_ref, gate_ref, g2_ref, o_ref)


def _ffn(x, rows, layer, norm_pre, norm_post, w_up, w_down, cast_next=(), next_layer=0):
    m, tm, d = rows.m, rows.tm, rows.d
    d_ff = w_up.shape[-1]
    tf = 512
    ni, nk = m // tm, d_ff // tf
    h_dtype = BF16 if tm % 16 == 0 else F32
    x_mode = dict(pipeline_mode=pl.Buffered(1)) if tm * d * 4 >= 8 * 2 ** 20 else {}
    cast_in, cast_out, cast_shape = [], [], []
    for w in cast_next:
        r, c = w.shape[1:]
        band = r // (ni * nk)
        assert band * ni * nk == r and band % 16 == 0
        cast_in.append(pl.BlockSpec((None, band, c), lambda i, k: (next_layer, i * nk + k, 0)))
        cast_out.append(pl.BlockSpec((band, c), lambda i, k: (i * nk + k, 0)))
        cast_shape.append(jax.ShapeDtypeStruct((r, c), BF16))
    y, *casts = pl.pallas_call(
        functools.partial(_ffn_kernel, n_cast=len(cast_next)),
        grid=(ni, nk),
        in_specs=[pl.BlockSpec((tm, d), lambda i, k: (i, 0), **x_mode),
                  _vec_spec(layer, d), rows.mod_spec(4), rows.mod_spec(3),
                  pl.BlockSpec((None, d, tf), lambda i, k: (0, 0, k)),
                  pl.BlockSpec((None, tf, d), lambda i, k: (0, k, 0)),
                  rows.mod_spec(5), _vec_spec(layer, d)] + cast_in,
        out_specs=[pl.BlockSpec((tm, d), lambda i, k: (i, 0))] + cast_out,
        out_shape=[jax.ShapeDtypeStruct((m, d), F32)] + cast_shape,
        scratch_shapes=[pltpu.VMEM((tm, d), h_dtype)],
        compiler_params=_params(("arbitrary", "arbitrary"), VMEM_LIMIT),
        name="ffn",
    )(x, norm_pre, rows.mod, rows.mod, w_up, w_down, rows.mod, norm_post, *cast_next)
    return y, casts


def _scmp_kernel(pt_ref, cache_ref, new_ref, pe_ref, w1_ref, w2_ref, o_ref, buf, sem, p_s, q_s,
                 *, layer, pages, n_chunk, n_batch):
    b = pl.program_id(0)
    c = pl.program_id(1)
    step = b * n_chunk + c
    slot = step % 2
    strips = pages * (PAGE_SIZE // CMP_STRIDE)
    n_past = n_chunk * strips

    def copies(bb, cc, sl):
        return [pltpu.make_async_copy(cache_ref.at[layer, pt_ref[bb, cc * pages + p], :, gs // 2, gs % 2, :],
                                      buf.at[sl, gs, p], sem.at[sl])
                for p in range(pages) for gs in range(2 * N_KV)]

    @pl.when(step == 0)
    def _():
        for n, cp in enumerate(copies(0, 0, 0)):
            cp.start(priority=n % 2)

    @pl.when(step + 1 < n_batch * n_chunk)
    def _():
        nxt = step + 1
        for n, cp in enumerate(copies(nxt // n_chunk, nxt % n_chunk, 1 - slot)):
            cp.start(priority=n % 2)

    for cp in copies(b, c, slot):
        cp.wait()

    cur = buf.at[slot]
    row0 = pl.multiple_of(c * strips, strips)
    for gs in range(2 * N_KV):
        sel = gs % 2
        xf = jnp.concatenate(
            [cur[gs, :, pl.ds(r, PAGE_SIZE // CMP_STRIDE, stride=CMP_STRIDE), :].reshape(strips, HEAD_DIM)
             for r in range(CMP_STRIDE)], axis=1)
        p, q = _compress_strips(xf, w1_ref, sel)
        p_s[gs, pl.ds(row0, strips), :] = p
        q_s[gs, pl.ds(row0, strips), :] = q

    @pl.when(c == n_chunk - 1)
    def _():
        for gs in range(2 * N_KV):
            sel = gs % 2
            new = new_ref[:, gs * HEAD_DIM:(gs + 1) * HEAD_DIM]
            xn = jnp.concatenate([new, jnp.zeros((1, STRIP_COLS - HEAD_DIM), F32)], axis=1)
            xn = jnp.broadcast_to(xn, (8, STRIP_COLS)).astype(BF16)
            q_s[gs, n_past:n_past + 8, :] = _dot(xn, w1_ref[sel, STRIP_COLS:2 * STRIP_COLS, :])
            hid = _gelu(p_s[gs] + q_s[gs, 1:n_past + 1, :] + pe_ref[sel, 0:1, :])
            o_ref[gs] = _dot(hid.astype(BF16), w2_ref[sel]).astype(BF16)


def _compress_sample(page_table, cache, layer, new_cmp, pe2, w1, w2):
    n_batch, n_pages = page_table.shape
    pages = 32 if n_pages % 32 == 0 else n_pages
    n_chunk = n_pages // pages
    n_past = n_pages * (PAGE_SIZE // CMP_STRIDE)
    hidden = w1.shape[-1]
    grid_spec = pltpu.PrefetchScalarGridSpec(
        num_scalar_prefetch=1,
        grid=(n_batch, n_chunk),
        in_specs=[pl.BlockSpec(memory_space=pl.ANY),
                  pl.BlockSpec((None, 1, KV_COLS), lambda b, c, pt: (b, 0, 0)),
                  pl.BlockSpec((2, 8, hidden), lambda b, c, pt: (0, 0, 0)),
                  pl.BlockSpec((2, 2 * STRIP_COLS, hidden), lambda b, c, pt: (0, 0, 0)),
                  pl.BlockSpec((2, hidden, HEAD_DIM), lambda b, c, pt: (0, 0, 0))],
        out_specs=pl.BlockSpec((None, 2 * N_KV, n_past, HEAD_DIM), lambda b, c, pt: (b, 0, 0, 0)),
        scratch_shapes=[pltpu.VMEM((2, 2 * N_KV, pages, PAGE_SIZE, HEAD_DIM), F32),
                        pltpu.SemaphoreType.DMA((2,)),
                        pltpu.VMEM((2 * N_KV, n_past, hidden), F32),
                        pltpu.VMEM((2 * N_KV, n_past + 8, hidden), F32)])
    return pl.pallas_call(
        functools.partial(_scmp_kernel, layer=layer, pages=pages, n_chunk=n_chunk, n_batch=n_batch),
        grid_spec=grid_spec,
        out_shape=jax.ShapeDtypeStruct((n_batch, 2 * N_KV, n_past, HEAD_DIM), BF16),
        compiler_params=_params(("arbitrary", "arbitrary"), VMEM_LIMIT),
        name="compress_sample",
    )(page_table, cache, new_cmp.reshape(n_batch, 1, KV_COLS), pe2, w1, w2)


def _ssel_kernel(q_ref, kvc_ref, sel_ref, oc_ref, idx_ref, *, pos, hpg, n_cmp, n_lane):
    qb = q_ref[...].astype(BF16)
    n_heads = qb.shape[0]
    head_row = lax.broadcasted_iota(I32, (n_heads, 1), 0)
    clane = lax.broadcasted_iota(I32, (1, n_cmp), 1)
    cmask = (clane * CMP_STRIDE + (CMP_BLOCK - 1)) <= pos
    lane = lax.broadcasted_iota(I32, (1, n_lane), 1)
    cur = pos // SLC_BLOCK
    forced = (lane == 0) | (lane == cur) | (lane == cur - 1)
    ri = lax.broadcasted_iota(I32, (n_lane, n_lane), 0)
    ci = lax.broadcasted_iota(I32, (n_lane, n_lane), 1)
    slot_id = lax.broadcasted_iota(I32, (N_SEL, 1), 0)
    o_c = jnp.zeros((n_heads, HEAD_DIM), F32)
    for g in range(N_KV):
        mine = (head_row // hpg) == g
        p = _msoftmax(_dot_nt(qb, kvc_ref[2 * g]) * SCALE, cmask)
        o_c = jnp.where(mine, _dot(p.astype(BF16), kvc_ref[2 * g + 1]), o_c)
        imp = jnp.sum(jnp.where(mine, p, 0.0), axis=0, keepdims=True)
        impb = _split_dot(jnp.broadcast_to(imp, (8, n_cmp)), sel_ref[...])[0:1]
        score = jnp.where(lane <= cur, jnp.where(forced, FORCE, impb), -FORCE)
        srow = jnp.broadcast_to(score, (n_lane, n_lane))
        scol = srow.T
        ahead = (scol > srow) | ((scol == srow) & (ri < ci))
        rank = jnp.sum(jnp.where(ahead, 1.0, 0.0), axis=0, keepdims=True)
        chosen = jnp.where((rank < float(N_SEL)) & (lane <= cur), 1.0, 0.0)
        ccol = jnp.broadcast_to(chosen, (n_lane, n_lane)).T
        before = jnp.sum(jnp.where(ri < ci, ccol, 0.0), axis=0, keepdims=True)
        hit = (chosen > 0.5) & (before == slot_id.astype(F32))
        idx = jnp.sum(jnp.where(hit, lane.astype(F32), 0.0), axis=1, keepdims=True)
        idx_ref[g] = jnp.broadcast_to(idx, (N_SEL, LANES)).astype(I32)
    oc_ref[...] = o_c


def _select_sample(q3, kvc, sel_m, pos, hpg):
    n_batch, n_heads, _ = q3.shape
    n_cmp = kvc.shape[2]
    n_lane = sel_m.shape[1]
    return pl.pallas_call(
        functools.partial(_ssel_kernel, pos=pos, hpg=hpg, n_cmp=n_cmp, n_lane=n_lane),
        grid=(n_batch,),
        in_specs=[pl.BlockSpec((None, n_heads, HEAD_DIM), lambda b: (b, 0, 0)),
                  pl.BlockSpec((None, 2 * N_KV, n_cmp, HEAD_DIM), lambda b: (b, 0, 0, 0)),
                  pl.BlockSpec(sel_m.shape, lambda b: (0, 0))],
        out_specs=[pl.BlockSpec((None, n_heads, HEAD_DIM), lambda b: (b, 0, 0)),
                   pl.BlockSpec((None, N_KV, N_SEL, LANES), lambda b: (b, 0, 0, 0))],
        out_shape=[jax.ShapeDtypeStruct((n_batch, n_heads, HEAD_DIM), F32),
                   jax.ShapeDtypeStruct((n_batch, N_KV, N_SEL, LANES), I32)],
        compiler_params=_params(("arbitrary",), VMEM_LIMIT),
        name="select_sample",
    )(q3, kvc, sel_m)


def _smix_kernel(pt_ref, idx_ref, cache_ref, wcache_ref, q_ref, oc_ref, gt_ref, slc_ref, win_ref, rope_ref,
                 u_ref, v_ref, lg_ref, lb_ref, w0_ref, b0_ref, na_ref, nb_ref,
                 a_ref, b_ref, vn_ref, buf, wbuf, sem, *, layer, pos, hpg, nb_past):
    b = pl.program_id(0)
    slot = b % 2
    halves = PAGE_SIZE // SLC_BLOCK

    def copies(bb, sl):
        out = []
        for g in range(N_KV):
            for k in range(N_SEL):
                jp = jnp.minimum(idx_ref[bb, g, k], nb_past - 1)
                phys = pt_ref[bb, jp // halves]
                rows = pl.ds(pl.multiple_of((jp % halves) * SLC_BLOCK, SLC_BLOCK), SLC_BLOCK)
                for t in range(2):
                    out.append(pltpu.make_async_copy(cache_ref.at[layer, phys, rows, g, t, :],
                                                     buf.at[sl, t, g, k], sem.at[sl]))
            for t in range(2):
                out.append(pltpu.make_async_copy(wcache_ref.at[layer, bb, :, g, t, :], wbuf.at[sl, g, t],
                                                 sem.at[sl]))
        return out

    @pl.when(b == 0)
    def _():
        for n, cp in enumerate(copies(0, 0)):
            cp.start(priority=n % 2)

    @pl.when(b + 1 < pl.num_programs(0))
    def _():
        for n, cp in enumerate(copies(b + 1, 1 - slot)):
            cp.start(priority=n % 2)

    vn = _layernorm(_gelu(v_ref[...]), lg_ref[...], lb_ref[...])
    vn_ref[...] = vn
    a_out = _gelu(u_ref[...]) * (w0_ref[...] * vn + b0_ref[...])
    a_ref[...] = _rms(a_out, na_ref[...]).astype(BF16)

    q = q_ref[...]
    n_heads = q.shape[0]
    head_row = lax.broadcasted_iota(I32, (n_heads, 1), 0)
    qrb = _rope(q, rope_ref[0:1, :], rope_ref[1:2, :], rope_ref[2:3, :]).astype(BF16)
    sig = _sigmoid(gt_ref[...])
    n_keys = N_SEL * SLC_BLOCK
    klane = lax.broadcasted_iota(I32, (1, n_keys), 1)
    wb = wbuf.shape[3]
    wdiff = pos - (pos - wb + lax.broadcasted_iota(I32, (1, wb), 1))
    wmask = (wdiff >= 0) & (wdiff < WINDOW)

    for cp in copies(b, slot):
        cp.wait()

    o_s = jnp.zeros((n_heads, HEAD_DIM), F32)
    o_w = jnp.zeros((n_heads, HEAD_DIM), F32)
    for g in range(N_KV):
        mine = (head_row // hpg) == g
        k0 = g * 2 * HEAD_DIM
        v0 = k0 + HEAD_DIM
        kmask = jnp.zeros((1, n_keys), F32)
        has_new = jnp.zeros((1, 1), F32)
        for k in range(N_SEL):
            blk = idx_ref[b, g, k]
            in_k = (klane // SLC_BLOCK) == k
            kmask = jnp.where(in_k & (blk < nb_past), 1.0, kmask)
            has_new = jnp.where(blk >= nb_past, 1.0, has_new)
        ks = buf[slot, 0, g].reshape(n_keys, HEAD_DIM).astype(BF16)
        vs = buf[slot, 1, g].reshape(n_keys, HEAD_DIM).astype(BF16)
        s = jnp.where(kmask > 0.5, _dot_nt(qrb, ks) * SCALE, -jnp.inf)
        k_new = slc_ref[:, k0:k0 + HEAD_DIM].astype(BF16)
        v_new = slc_ref[:, v0:v0 + HEAD_DIM].astype(BF16)
        s_new = jnp.sum(qrb.astype(F32) * k_new.astype(F32), axis=-1, keepdims=True) * SCALE
        s_new = jnp.where(has_new > 0.5, s_new, -jnp.inf)
        m = jnp.maximum(jnp.max(s, axis=-1, keepdims=True), s_new)
        m = jnp.where(m > -jnp.inf, m, 0.0)
        e = jnp.exp(s - m)
        e_new = jnp.exp(s_new - m)
        den = jnp.maximum(jnp.sum(e, axis=-1, keepdims=True) + e_new, 1e-30)
        og = _dot((e / den).astype(BF16), vs) + (e_new / den).astype(BF16).astype(F32) * v_new.astype(F32)
        o_s = jnp.where(mine, og, o_s)
        s = jnp.where(wmask, _dot_nt(qrb, wbuf[slot, g, 0].astype(BF16)) * SCALE, -jnp.inf)
        k_new = win_ref[:, k0:k0 + HEAD_DIM].astype(BF16)
        v_new = win_ref[:, v0:v0 + HEAD_DIM].astype(BF16)
        s_new = jnp.sum(qrb.astype(F32) * k_new.astype(F32), axis=-1, keepdims=True) * SCALE
        m = jnp.maximum(jnp.max(s, axis=-1, keepdims=True), s_new)
        e = jnp.exp(s - m)
        e_new = jnp.exp(s_new - m)
        den = jnp.maximum(jnp.sum(e, axis=-1, keepdims=True) + e_new, 1e-30)
        og = (_dot((e / den).astype(BF16), wbuf[slot, g, 1].astype(BF16))
              + (e_new / den).astype(BF16).astype(F32) * v_new.astype(F32))
        o_w = jnp.where(mine, og, o_w)
    b_out = sig[:, 0:1] * oc_ref[...] + sig[:, 1:2] * o_s + sig[:, 2:3] * o_w
    ms = jnp.sum(jnp.sum(b_out * b_out, axis=-1, keepdims=True), axis=0, keepdims=True) / float(n_heads * HEAD_DIM)
    b_ref[...] = (b_out * lax.rsqrt(ms + EPS) * nb_ref[...]).astype(BF16)


def _mix_sample(page_table, idx, cache_slc, cache_win, q3, o_c, gt3, slc_new, win_new, rope_s, z3, layer,
                ln_g, ln_b, w0, b0, norm_a, norm_b3, pos, hpg, a_width):
    n_batch, n_heads, _ = q3.shape
    wb = cache_win.shape[2]
    nb_past = page_table.shape[1] * (PAGE_SIZE // SLC_BLOCK)
    head_spec = pl.BlockSpec((None, n_heads, HEAD_DIM), lambda b, *_: (b, 0, 0))
    new_spec = pl.BlockSpec((None, 1, KV_COLS), lambda b, *_: (b, 0, 0))
    row_spec = lambda blk: pl.BlockSpec((None, 1, a_width), lambda b, *_: (b, 0, blk))
    vec = lambda: pl.BlockSpec((None, 1, a_width), lambda b, *_: (layer, 0, 0))
    grid_spec = pltpu.PrefetchScalarGridSpec(
        num_scalar_prefetch=2,
        grid=(n_batch,),
        in_specs=[pl.BlockSpec(memory_space=pl.ANY), pl.BlockSpec(memory_space=pl.ANY), head_spec, head_spec,
                  pl.BlockSpec((None, n_heads, 3), lambda b, *_: (b, 0, 0)),
                  new_spec, new_spec,
                  pl.BlockSpec(rope_s.shape, lambda b, *_: (0, 0)),
                  row_spec(0), row_spec(1), vec(), vec(), vec(), vec(), vec(),
                  pl.BlockSpec((None, n_heads, HEAD_DIM), lambda b, *_: (layer, 0, 0))],
        out_specs=[pl.BlockSpec((None, 1, a_width), lambda b, *_: (b, 0, 0)),
                   head_spec,
                   pl.BlockSpec((None, 1, a_width), lambda b, *_: (b, 0, 0))],
        scratch_shapes=[pltpu.VMEM((2, 2, N_KV, N_SEL, SLC_BLOCK, HEAD_DIM), F32),
                        pltpu.VMEM((2, N_KV, 2, wb, HEAD_DIM), F32),
                        pltpu.SemaphoreType.DMA((2,))])
    return pl.pallas_call(
        functools.partial(_smix_kernel, layer=layer, pos=pos, hpg=hpg, nb_past=nb_past),
        grid_spec=grid_spec,
        out_shape=[jax.ShapeDtypeStruct((n_batch, 1, a_width), BF16),
                   jax.ShapeDtypeStruct((n_batch, n_heads, HEAD_DIM), BF16),
                   jax.ShapeDtypeStruct((n_batch, 1, a_width), F32)],
        compiler_params=_params(("arbitrary",), VMEM_LIMIT),
        name="mix_sample",
    )(page_table, idx, cache_slc, cache_win, q3, o_c, gt3, slc_new.reshape(n_batch, 1, KV_COLS),
      win_new.reshape(n_batch, 1, KV_COLS), rope_s, z3, z3,
      ln_g, ln_b, w0, b0, norm_a, norm_b3)


def _rope_tables(pos):
    inv = ROPE_THETA ** (-jnp.arange(ROPE_HALF, dtype=F32) / ROPE_HALF)
    ang = pos.astype(F32)[:, None] * inv[None, :]
    cos, sin = jnp.cos(ang), jnp.sin(ang)
    n = pos.shape[0]
    zeros = lambda w: jnp.zeros((n, w), F32)
    c = jnp.concatenate([cos, cos, jnp.ones((n, HEAD_DIM - ROPE_DIM), F32)], axis=1)
    sa = jnp.concatenate([-sin, zeros(HEAD_DIM - ROPE_HALF)], axis=1)
    sb = jnp.concatenate([zeros(ROPE_HALF), sin, zeros(HEAD_DIM - ROPE_DIM)], axis=1)
    return c, sa, sb


def _block_sum_matrix(n_cmp, n_lane):
    i = jnp.arange(n_cmp)[:, None]
    j = jnp.arange(n_lane)[None, :]
    return ((i // CMP_PER_SLC == j) & (i % CMP_PER_SLC < CMP_INSIDE)).astype(BF16)


def kernel(x_prompt, x_sample, cache_cmp, cache_slc, cache_win, page_table, c_prompt, c_sample, w_ada, b_ada, norm_pre_mix, norm_post_mix, norm_pre_ffn, norm_post_ffn, w_in, ln_v_g, ln_v_b, sgu_w, sgu_b, cmp_pe_k, cmp_pe_v, cmp_w1_k, cmp_w2_k, cmp_w1_v, cmp_w2_v, out_norm_a, out_norm_b, w_out, w_up, w_down):
    batch, seq, d = x_prompt.shape
    n_dec, dec_seq, _ = x_sample.shape
    depth = w_ada.shape[0]
    a_width = d // 2
    b_width = d - a_width
    groups = a_width // HEAD_DIM
    n_heads = b_width // HEAD_DIM
    hpg = n_heads // N_KV
    n_main = 2 * a_width + b_width + 3 * KV_COLS
    n_gate = 3 * n_heads
    n_pages = page_table.shape[1]
    past = n_pages * PAGE_SIZE
    assert dec_seq == 1 and n_dec % 8 == 0 and a_width == b_width and n_gate <= LANES
    assert (2 * a_width + b_width) % (3 * KV_COLS) == 0 and seq % 1024 == 0
    assert past % SLC_BLOCK == 0 and past // SLC_BLOCK + 1 > N_SEL and cache_win.shape[2] <= past
    q_block = 2 * a_width // b_width
    kv_block = (2 * a_width + b_width) // (3 * KV_COLS)
    cmp_block = (2 * a_width + b_width) // KV_COLS
    m_p = batch * seq

    vec3 = lambda a: a.reshape(depth, 1, a.shape[-1])
    w_main = w_in.astype(BF16)
    big = (w_out, w_up, w_down)
    wts = [w[0:1].astype(BF16) for w in big]
    w_gate = jnp.pad(w_in[:, :, n_main:], ((0, 0), (0, 0), (0, LANES - n_gate))).astype(BF16)
    cmp_w1 = jnp.stack([cmp_w1_k, cmp_w1_v], axis=1).astype(BF16)
    cmp_w2 = jnp.stack([cmp_w2_k, cmp_w2_v], axis=1).astype(BF16)
    cmp_pe = jnp.broadcast_to(jnp.stack([cmp_pe_k, cmp_pe_v], axis=1).reshape(depth, 2, 1, CMP_BLOCK * HEAD_DIM),
                              (depth, 2, 8, CMP_BLOCK * HEAD_DIM))
    cmp_pe = _cmp_bias(cmp_pe, cmp_w1)
    pre_mix, post_mix, pre_ffn, post_ffn = map(vec3, (norm_pre_mix, norm_post_mix, norm_pre_ffn, norm_post_ffn))
    ln_g, ln_b, norm_a, norm_b = map(vec3, (ln_v_g, ln_v_b, out_norm_a, out_norm_b))
    norm_b3 = out_norm_b.reshape(depth, n_heads, HEAD_DIM)
    sgu_bt = jnp.swapaxes(sgu_b, 1, 2)
    sgu_w0 = jnp.repeat(sgu_w[:, :, 0, 0], HEAD_DIM, axis=1).reshape(depth, 1, a_width)
    sgu_b0 = jnp.repeat(sgu_b[:, :, 0], HEAD_DIM, axis=1).reshape(depth, 1, a_width)

    mod_rows = -(-(n_dec + batch) // 8) * 8
    c_all = jnp.concatenate([c_sample, c_prompt, jnp.zeros((mod_rows - n_dec - batch, d), F32)], axis=0)
    mod = _ada(c_all, w_ada, b_ada)

    tables_p = _rope_tables(jnp.arange(seq))
    tables_s = _rope_tables(jnp.full((n_dec,), past))
    rope_s = jnp.concatenate([t[0:1] for t in tables_s] + [jnp.zeros((5, LANES), F32)], axis=0)
    n_cmp_p = seq // CMP_STRIDE
    sel_p = _block_sum_matrix(n_cmp_p, LANES)
    key_chunk = 512
    exp_p = (jnp.arange(LANES)[None, :, None]
             == (jnp.arange(seq) // SLC_BLOCK).reshape(seq // key_chunk, 1, key_chunk)).astype(BF16)
    n_cmp_s = past // CMP_STRIDE
    nb_lane = -(-(past // SLC_BLOCK + 1) // LANES) * LANES
    sel_s = _block_sum_matrix(n_cmp_s, nb_lane)

    y_p = x_prompt.reshape(m_p, d)
    y_s = x_sample.reshape(n_dec, d)
    caches_p = [jnp.zeros((depth, batch, rows, N_KV, 2, HEAD_DIM), F32) for rows in (seq, seq, min(WINDOW, seq))]
    outs = [[] for _ in range(4)]
    win_keep = min(WINDOW, seq)
    for l in range(depth):
        w_out_b, w_up_b, w_down_b = wts
        rows = _Rows(mod, l, d, m_p, 1024, seq, n_dec)
        z, gt = _in_proj(y_p, rows, l, pre_mix, w_main, w_gate, n_main)
        a_n = _sgu(z, l, a_width, ln_g, ln_b, sgu_w, sgu_bt, norm_a)
        *caches_p, slc_b, win_b = _kv_rope_prompt(z, kv_block, tables_p, 256, l, depth, batch, seq, win_keep,
                                                  caches_p)
        kvc = _compress_prompt(z, batch, seq, cmp_block, cmp_pe[l], cmp_w1[l], cmp_w2[l])
        b_n = _attn_prompt(z, gt, kvc, slc_b, win_b, tables_p, l, norm_b, sel_p, exp_p, batch, seq, b_width, q_block)
        y_p = _out_proj(a_n, b_n, y_p, _Rows(mod, l, d, m_p, 512, seq, n_dec), l, w_out_b, post_mix)
        y_p, wts_next = _ffn(y_p, rows, l, pre_ffn, post_ffn, w_up_b, w_down_b,
                             cast_next=big if l + 1 < depth else (), next_layer=l + 1)
        rows = _Rows(mod, l, d, n_dec, n_dec, None, 0)
        z, gt = _in_proj(y_s, rows, l, pre_mix, w_main, w_gate, n_main)
        kv_cmp, kv_slc, kv_win, _, _ = _kv_rope(z, kv_block, tables_s, n_dec)
        kvc = _compress_sample(page_table, cache_cmp, l, kv_cmp, cmp_pe[l], cmp_w1[l], cmp_w2[l])
        q3 = z[:, 2 * a_width:2 * a_width + b_width].reshape(n_dec, n_heads, HEAD_DIM)
        o_c, idx = _select_sample(q3, kvc, sel_s, past, hpg)
        a_n, b_n, v_n = _mix_sample(page_table, idx[:, :, :, 0], cache_slc, cache_win, q3, o_c,
                                    gt[:, :n_gate].reshape(n_dec, n_heads, 3), kv_slc, kv_win,
                                    rope_s, z.reshape(n_dec, 1, n_main), l, ln_g, ln_b, sgu_w0, sgu_b0,
                                    norm_a, norm_b3, past, hpg, a_width)
        y_s = _out_proj(a_n.reshape(n_dec, a_width), b_n.reshape(n_dec, b_width), y_s, rows, l, w_out_b, post_mix)
        y_s, _ = _ffn(y_s, rows, l, pre_ffn, post_ffn, w_up_b, w_down_b)
        wts = [w[None] for w in wts_next]
        outs[0].append(kv_cmp.reshape(n_dec, 1, N_KV, 2, HEAD_DIM))
        outs[1].append(kv_slc.reshape(n_dec, 1, N_KV, 2, HEAD_DIM))
        outs[2].append(kv_win.reshape(n_dec, 1, N_KV, 2, HEAD_DIM))
        outs[3].append(v_n)
    return (y_p.reshape(batch, seq, d), y_s.reshape(n_dec, 1, d), *caches_p, *[jnp.stack(o) for o in outs])
```
